```python
import jax
import jax.numpy as jnp
from jax import lax
import numpy as np

D_MODEL = 1024
BATCH = 8
SEQ = 2048
DEPTH = 2

GRID_W = 64
CTX_LEN = 256
N_MIXERS = 2
RET_HEADS = 4
RET_DK = 256
RET_DV = 512
RET_QK = RET_HEADS * RET_DK
RET_V = RET_HEADS * RET_DV
RET_IN = 2 * RET_QK + 2 * RET_V
RET_CHUNK = 128
HG_HEADS = 8
HG_DK = 128
HG_DV = D_MODEL // HG_HEADS
HG_K = HG_HEADS * HG_DK
HG_V = HG_HEADS * HG_DV
HG_IN = 3 * HG_K + 2 * HG_V
HG_CHUNK = 64
N_EXPERTS = 16
EC_CAPACITY = 2
EXPERT_FF = 2816
ROPE_THETA = 10000.0
EPS = 1e-6
F32 = jnp.float32

kernel_name = 'hybrid_retention_hgrn2_ec_moe_dit'


def rmsnorm(x, w=None):
    xf = x.astype(F32)
    y = xf * lax.rsqrt(jnp.mean(xf * xf, axis=-1, keepdims=True) + EPS)
    if w is not None:
        y = y * w.astype(F32)
    return y.astype(x.dtype)


def modulate(h, shift, scale):
    return h * (1 + scale) + shift


def split_heads(a, n_heads, d):
    B, T, _ = a.shape
    return a.reshape(B, T, n_heads, d).transpose(0, 2, 1, 3)


def merge_heads(a):
    B, H, T, d = a.shape
    return a.transpose(0, 2, 1, 3).reshape(B, T, H * d)


def grid_angles(n):
    rows = n // GRID_W
    r = jnp.repeat(jnp.arange(rows), GRID_W).astype(F32)
    col = jnp.tile(jnp.arange(GRID_W), rows).astype(F32)
    half = RET_DK // 2
    inv = ROPE_THETA ** (-jnp.arange(0, half, 2, dtype=F32) / half)
    return r[:, None] * inv, col[:, None] * inv


def rope(x, ang):
    x1, x2 = jnp.split(x, 2, axis=-1)
    cos = jnp.cos(ang).astype(x.dtype)
    sin = jnp.sin(ang).astype(x.dtype)
    return jnp.concatenate([x1 * cos - x2 * sin, x1 * sin + x2 * cos], axis=-1)


def rope2d(x, ang_r, ang_c):
    xr, xc = jnp.split(x, 2, axis=-1)
    return jnp.concatenate([rope(xr, ang_r), rope(xc, ang_c)], axis=-1)


def retention_log_decay():
    j = jnp.arange(2 * RET_HEADS, dtype=F32)
    lg = jnp.log1p(-jnp.exp2(-5.0 - j / 2))
    return lg[0::2], lg[1::2]


def chunk_recurrence(q, k, v, g, s0, chunk):
    B, H, T, dk = q.shape
    dv = v.shape[-1]
    dg = g.shape[-1]
    n = T // chunk

    def blocks(a):
        return a.reshape(B, H, n, chunk, a.shape[-1]).transpose(2, 0, 1, 3, 4)

    lower = jnp.tril(jnp.ones((chunk, chunk), dtype=bool))[:, :, None]

    def step(S, blk):
        qc, kc, vc, gc = blk
        b = jnp.cumsum(gc.astype(F32), axis=2)
        b_end = b[:, :, -1:, :]
        o = jnp.einsum('bhik,bhkv->bhiv', qc * jnp.exp(b), S)
        decay = jnp.exp(jnp.where(lower, b[:, :, :, None, :] - b[:, :, None, :, :], -jnp.inf))
        if dg == 1:
            scores = jnp.einsum('bhik,bhjk->bhij', qc, kc) * decay[..., 0]
        else:
            scores = jnp.einsum('bhijk,bhjk->bhij', qc[:, :, :, None, :] * decay, kc)
        o = o + jnp.einsum('bhij,bhjv->bhiv', scores, vc)
        S = S * jnp.exp(jnp.swapaxes(b_end, -1, -2)) + jnp.einsum('bhjk,bhjv->bhkv', kc * jnp.exp(b_end - b), vc)
        return S, o

    S, o = lax.scan(step, s0, (blocks(q), blocks(k), blocks(v), blocks(g)))
    return o.transpose(1, 2, 0, 3, 4).reshape(B, H, T, dv), S


def recurrence_final_state(k, v, g):
    b = jnp.cumsum(g.astype(F32), axis=2)
    return jnp.einsum('bhtk,bhtv->bhkv', k * jnp.exp(b[:, :, -1:, :] - b), v)


def bidirectional_recurrence(lat, ctx, chunk, need_ctx_out):
    rev = lambda a: jnp.flip(a, axis=2)
    q, kf, gf, kb, gb, v = ctx
    B, H, _, dk = q.shape
    dv = v.shape[-1]
    if need_ctx_out:
        s0 = jnp.zeros((B, H, dk, dv), F32)
        oc_f, sf = chunk_recurrence(q, kf, v, gf, s0, chunk)
        oc_b, sb = chunk_recurrence(rev(q), rev(kb), rev(v), rev(gb), s0, chunk)
        oc = oc_f + rev(oc_b)
    else:
        sf = recurrence_final_state(kf, v, gf)
        sb = recurrence_final_state(rev(kb), rev(v), rev(gb))
        oc = None
    q, kf, gf, kb, gb, v = lat
    ox_f, _ = chunk_recurrence(q, kf, v, gf, sf, chunk)
    ox_b, _ = chunk_recurrence(rev(q), rev(kb), rev(v), rev(gb), sb, chunk)
    return ox_f + rev(ox_b), oc


def retention_mixer(hx, hc, w_in, w_out, need_ctx_out):
    ang_r, ang_c = grid_angles(hx.shape[1])
    log_gf, log_gb = retention_log_decay()

    def project(h, rotate):
        B, T, _ = h.shape
        q, k, v, gt = jnp.split(h @ w_in, [RET_QK, 2 * RET_QK, 2 * RET_QK + RET_V], axis=-1)
        q = split_heads(q, RET_HEADS, RET_DK)
        k = split_heads(k, RET_HEADS, RET_DK) * (RET_DK ** -0.5)
        v = split_heads(v, RET_HEADS, RET_DV)
        if rotate:
            q = rope2d(q, ang_r, ang_c)
            k = rope2d(k, ang_r, ang_c)
        gf = jnp.broadcast_to(log_gf[None, :, None, None], (B, RET_HEADS, T, 1))
        gb = jnp.broadcast_to(log_gb[None, :, None, None], (B, RET_HEADS, T, 1))
        return (q, k, gf, k, gb, v), gt

    def readout(o, gt):
        o = merge_heads(rmsnorm(o))
        return (jax.nn.silu(gt) * o.astype(gt.dtype)) @ w_out

    lat, gt_x = project(hx, True)
    ctx, gt_c = project(hc, False)
    ox, oc = bidirectional_recurrence(lat, ctx, RET_CHUNK, need_ctx_out)
    yx = readout(ox, gt_x)
    yc = readout(oc, gt_c) if need_ctx_out else None
    return yx, yc


def hgrn2_mixer(hx, hc, w_in, lb_f, lb_b, g_norm, w_out, need_ctx_out):
    def project(h):
        q, ff, fb, i, gt = jnp.split(h @ w_in, [HG_K, 2 * HG_K, 3 * HG_K, 3 * HG_K + HG_V], axis=-1)
        q = split_heads(jax.nn.silu(q), HG_HEADS, HG_DK)
        v = split_heads(i, HG_HEADS, HG_DV)

        def gate(fl, lb):
            f = lb + (1 - lb) * jax.nn.sigmoid(fl.astype(F32))
            return split_heads(1 - f, HG_HEADS, HG_DK), split_heads(jnp.log(f), HG_HEADS, HG_DK)

        kf, gf = gate(ff, lb_f)
        kb, gb = gate(fb, lb_b)
        return (q, kf, gf, kb, gb, v), gt

    def readout(o, gt):
        o = merge_heads(rmsnorm(o, g_norm))
        return (jax.nn.silu(gt) * o.astype(gt.dtype)) @ w_out

    lat, gt_x = project(hx)
    ctx, gt_c = project(hc)
    ox, oc = bidirectional_recurrence(lat, ctx, HG_CHUNK, need_ctx_out)
    yx = readout(ox, gt_x)
    yc = readout(oc, gt_c) if need_ctx_out else None
    return yx, yc


def expert_choice_ffn(h, w_router, w_gate, w_up, w_down):
    B, N, D = h.shape
    cap = EC_CAPACITY * N // N_EXPERTS
    aff = jax.nn.softmax(jnp.einsum('bnd,de->bne', h, w_router).astype(F32), axis=-1)
    gate, idx = lax.top_k(jnp.swapaxes(aff, 1, 2), cap)
    xe = jax.vmap(lambda hb, ib: hb[ib])(h, idx)
    a = jnp.einsum('becd,edf->becf', xe, w_gate)
    u = jnp.einsum('becd,edf->becf', xe, w_up)
    y = jnp.einsum('becf,efd->becd', jax.nn.silu(a) * u, w_down) * gate[..., None].astype(h.dtype)
    out = jax.vmap(lambda ib, yb: jnp.zeros((N, D), yb.dtype).at[ib].add(yb))(idx, y)
    return out.astype(h.dtype)


def setup_inputs(seed: int = 0) -> dict:
    key = jax.random.key(seed)
    ks = jax.random.split(key, 20)
    n_ret = (DEPTH + N_MIXERS - 1) // N_MIXERS
    n_hg = (DEPTH + N_MIXERS - 2) // N_MIXERS
    nrm = lambda k, shape, scale: jax.random.normal(k, shape, F32) * scale
    D = D_MODEL
    return {
        'x': nrm(ks[0], (BATCH, SEQ, D), 1.0),
        'c': nrm(ks[1], (BATCH, D), 1.0),
        'ctx': nrm(ks[2], (BATCH, CTX_LEN, D), 1.0),
        'c_ctx': nrm(ks[3], (D,), 1.0),
        'w_ada': nrm(ks[4], (DEPTH, D, 6 * D), 0.5 * D ** -0.5),
        'b_ada': nrm(ks[5], (DEPTH, 6 * D), 0.02),
        'norm_mix': 1.0 + nrm(ks[6], (DEPTH, D), 0.02),
        'norm_ffn': 1.0 + nrm(ks[7], (DEPTH, D), 0.02),
        'ret_w_in': nrm(ks[8], (n_ret, D, RET_IN), D ** -0.5),
        'ret_w_out': nrm(ks[9], (n_ret, RET_V, D), RET_V ** -0.5),
        'hg_w_in': nrm(ks[10], (n_hg, D, HG_IN), D ** -0.5),
        'hg_g_norm': 1.0 + nrm(ks[11], (n_hg, HG_DV), 0.02),
        'hg_lower_bounds': nrm(ks[12], (DEPTH, 2, HG_K), 0.1),
        'hg_w_out': nrm(ks[13], (n_hg, HG_V, D), HG_V ** -0.5),
        'moe_router': nrm(ks[14], (DEPTH, D, N_EXPERTS), D ** -0.5),
        'moe_w_gate': nrm(ks[15], (DEPTH, N_EXPERTS, D, EXPERT_FF), D ** -0.5),
        'moe_w_up': nrm(ks[16], (DEPTH, N_EXPERTS, D, EXPERT_FF), D ** -0.5),
        'moe_w_down': nrm(ks[17], (DEPTH, N_EXPERTS, EXPERT_FF, D), EXPERT_FF ** -0.5),
        'norm_final': 1.0 + nrm(ks[18], (D,), 0.02),
    }


def reference(x, c, ctx, c_ctx, w_ada, b_ada, norm_mix, norm_ffn, ret_w_in, ret_w_out,
              hg_w_in, hg_g_norm, hg_lower_bounds, hg_w_out, moe_router, moe_w_gate,
              moe_w_up, moe_w_down, norm_final):
    lb_w = jax.nn.softmax(hg_lower_bounds.astype(F32), axis=0)
    lower_bound = jnp.cumsum(lb_w, axis=0) - lb_w[0]
    cx = ctx
    for layer in range(DEPTH):
        last = layer == DEPTH - 1
        mix_id, slot = layer % N_MIXERS, layer // N_MIXERS
        mod_x = jnp.split((jax.nn.silu(c) @ w_ada[layer] + b_ada[layer])[:, None, :], 6, axis=-1)
        mod_c = jnp.split(jax.nn.silu(c_ctx) @ w_ada[layer] + b_ada[layer], 6, axis=-1)
        hx = modulate(rmsnorm(x, norm_mix[layer]), mod_x[0], mod_x[1])
        hc = modulate(rmsnorm(cx, norm_mix[layer]), mod_c[0], mod_c[1])
        if mix_id == 0:
            ox, oc = retention_mixer(hx, hc, ret_w_in[slot], ret_w_out[slot], not last)
        else:
            ox, oc = hgrn2_mixer(hx, hc, hg_w_in[slot], lower_bound[layer, 0], lower_bound[layer, 1],
                                 hg_g_norm[slot], hg_w_out[slot], not last)
        x = x + mod_x[2] * ox
        hx = modulate(rmsnorm(x, norm_ffn[layer]), mod_x[3], mod_x[4])
        x = x + mod_x[5] * expert_choice_ffn(hx, moe_router[layer], moe_w_gate[layer],
                                              moe_w_up[layer], moe_w_down[layer])
        if not last:
            cx = cx + mod_c[2] * oc
            hc = modulate(rmsnorm(cx, norm_ffn[layer]), mod_c[3], mod_c[4])
            cx = cx + mod_c[5] * expert_choice_ffn(hc, moe_router[layer], moe_w_gate[layer],
                                                    moe_w_up[layer], moe_w_down[layer])
    return rmsnorm(x, norm_final)
```

```python
import functools

import numpy as np
import jax
import jax.numpy as jnp
from jax import lax
from jax.experimental import pallas as pl
from jax.experimental.pallas import tpu as pltpu

F32 = jnp.float32
BF16 = jnp.bfloat16
EPS = 1e-6
ROPE_THETA = 10000.0
GRID_W = 64

LANES = 128
TOKEN_BLOCK = 256
RET_HEADS, RET_DK, RET_DV, RET_CHUNK = 4, 256, 512, 256
HG_HEADS, HG_DK, HG_DV, HG_CHUNK, HG_SUB = 8, 128, 128, 128, 8
LOG2_E = 1.4426950408889634
N_EXPERTS = 16
EC_CAPACITY = 2
FF_TILE = 256
PROJ_COLS = 1024
VMEM_LIMIT = 56 * 1024 * 1024

HIGHEST = lax.Precision.HIGHEST


def _cparams(sem):
    return pltpu.CompilerParams(dimension_semantics=sem, vmem_limit_bytes=VMEM_LIMIT)


def _dot(a, b, **kw):
    return jnp.dot(a, b, preferred_element_type=F32, **kw)


def _dot_nt(a, b):
    return lax.dot_general(a, b, (((1,), (1,)), ((), ())), preferred_element_type=F32)


def _silu(x):
    return x * jax.nn.sigmoid(x)


def _rms_mod(x, nw, shift, scale):
    ms = jnp.mean(x * x, axis=-1, keepdims=True)
    h = x * lax.rsqrt(ms + EPS) * nw
    return h * (1.0 + scale) + shift


def _ada_kernel(c_ref, w_ref, b_ref, o_ref):
    o_ref[...] = _dot(_silu(c_ref[...]), w_ref[...], precision=HIGHEST) + b_ref[...]


def _ada(cc, w_ada, b_ada):
    depth, d, d6 = w_ada.shape
    rows = cc.shape[0]
    return pl.pallas_call(
        _ada_kernel,
        grid=(depth, d6 // d),
        in_specs=[
            pl.BlockSpec((rows, d), lambda l, j: (0, 0)),
            pl.BlockSpec((None, d, d), lambda l, j: (l, 0, j)),
            pl.BlockSpec((None, 1, d), lambda l, j: (l, 0, j)),
        ],
        out_specs=pl.BlockSpec((None, rows, d), lambda l, j: (l, 0, j)),
        out_shape=jax.ShapeDtypeStruct((depth, rows, d6), F32),
        compiler_params=_cparams(("parallel", "parallel")),
    )(cc, w_ada, b_ada.reshape(depth, 1, d6))


def _prenorm_kernel(x_ref, mods_ref, nw_ref, h_ref, *, seq, shift_row):
    ctx_row = mods_ref.shape[0] - 1
    nw = nw_ref[...]
    n_ctx = x_ref.shape[0] - seq
    for r0, n, row in ((0, seq, pl.program_id(0)), (seq, n_ctx, ctx_row)):
        m = mods_ref[row]
        shift, scale = m[shift_row:shift_row + 1, :], m[shift_row + 1:shift_row + 2, :]

        def body(i, carry):
            rows = pl.ds(pl.multiple_of(r0 + i * TOKEN_BLOCK, TOKEN_BLOCK), TOKEN_BLOCK)
            h_ref[rows, :] = _rms_mod(x_ref[rows, :], nw, shift, scale).astype(BF16)
            return carry

        lax.fori_loop(0, n // TOKEN_BLOCK, body, 0)


def _prenorm(x, mods, nw, seq, shift_row):
    bsz, t, d = x.shape
    return pl.pallas_call(
        functools.partial(_prenorm_kernel, seq=seq, shift_row=shift_row),
        grid=(bsz,),
        in_specs=[pl.BlockSpec((None, t, d), lambda b: (b, 0, 0)),
                  pl.BlockSpec(mods.shape, lambda b: (0, 0, 0)),
                  pl.BlockSpec((1, d), lambda b: (0, 0))],
        out_specs=pl.BlockSpec((None, t, d), lambda b: (b, 0, 0)),
        out_shape=jax.ShapeDtypeStruct((bsz, t, d), BF16),
        compiler_params=_cparams(("parallel",)),
    )(x, mods, nw)


def _proj_kernel(h_ref, w_ref, *rest, epilogue, n_extra):
    extras, outs = rest[:n_extra], rest[n_extra:]
    epilogue(_dot(h_ref[...], w_ref[...]), extras, outs)


def _ep_plain(acc, extras, outs):
    outs[0][...] = acc.astype(outs[0].dtype)


def _ep_silu(acc, extras, outs):
    outs[0][...] = _silu(acc).astype(outs[0].dtype)


def _ep_rope(acc, extras, outs):
    c_row, s_row, c_col, s_col = (r[...] for r in extras)
    for s in range(acc.shape[1] // LANES):
        slab = acc[:, s * LANES:(s + 1) * LANES]
        cos, sin = (c_row, s_row) if s % 2 == 0 else (c_col, s_col)
        rot = slab * cos + pltpu.roll(slab, LANES // 2, 1) * sin
        outs[0][:, s * LANES:(s + 1) * LANES] = rot.astype(outs[0].dtype)


def _ep_gate(acc, extras, outs, *, layer):
    raw = extras[0][...]
    e = jnp.exp(raw - jnp.max(raw, axis=0, keepdims=True))
    sm = e / jnp.sum(e, axis=0, keepdims=True)
    lb = -sm[0:1, :]
    for l in range(layer + 1):
        lb = lb + sm[l:l + 1, :]
    f = lb + (1.0 - lb) * jax.nn.sigmoid(acc)
    outs[0][...] = 1.0 - f
    outs[1][...] = jnp.log(f)


def _proj(h, w, col_block0, ncol, epilogue, out_dtypes, extras=(), extra_specs=()):
    bsz, t, d = h.shape
    tm, tn = t // 2, PROJ_COLS
    in_specs = [
        pl.BlockSpec((None, tm, d), lambda b, i, j: (b, i, 0)),
        pl.BlockSpec((d, tn), lambda b, i, j: (0, col_block0 + j)),
    ] + list(extra_specs)
    out_specs = [pl.BlockSpec((None, tm, tn), lambda b, i, j: (b, i, j)) for _ in out_dtypes]
    out_shape = [jax.ShapeDtypeStruct((bsz, t, tn * ncol), dt) for dt in out_dtypes]
    return pl.pallas_call(
        functools.partial(_proj_kernel, epilogue=epilogue, n_extra=len(extras)),
        grid=(bsz, 2, ncol), in_specs=in_specs, out_specs=out_specs, out_shape=out_shape,
        compiler_params=_cparams(("parallel", "parallel", "parallel")),
    )(h, w, *extras)


def _ret_tables():
    c = RET_CHUNK
    j = np.arange(2 * RET_HEADS, dtype=np.float64)
    lg = np.log1p(-np.exp2(-5.0 - j / 2))
    lg_f, lg_b = lg[0::2], lg[1::2]
    i = np.arange(c, dtype=np.float64)
    diff = i[:, None] - i[None, :]
    mats = np.zeros((RET_HEADS, c, c), np.float64)
    tabs = np.zeros((RET_HEADS, c, LANES), np.float64)
    for h in range(RET_HEADS):
        mats[h] = np.where(diff >= 0, np.exp(diff * lg_f[h]), 0.0) + np.where(diff <= 0, np.exp(-diff * lg_b[h]), 0.0)
        tabs[h, :, 0] = np.exp((i + 1) * lg_f[h])
        tabs[h, :, 1] = np.exp((c - 1 - i) * lg_f[h])
        tabs[h, :, 2] = np.exp((c - i) * lg_b[h])
        tabs[h, :, 3] = np.exp(i * lg_b[h])
        tabs[h, :, 4] = np.exp(c * lg_f[h])
        tabs[h, :, 5] = np.exp(c * lg_b[h])
    return jnp.asarray(mats, F32), jnp.asarray(tabs, F32)


def _ret_kernel(q_ref, k_ref, v_ref, m_ref, tab_ref, o_ref, vt_ref, sf_ref, sb_ref, *, n_chunks):
    c = RET_CHUNK
    decay = m_ref[...]
    tab = tab_ref[...]
    qf, kfs, qb, kbs = tab[:, 0:1], tab[:, 1:2], tab[:, 2:3], tab[:, 3:4]
    end_f, end_b = tab[0:1, 4:5], tab[0:1, 5:6]

    def rows(ci):
        return slice(ci * c, (ci + 1) * c)

    for ci in range(n_chunks):
        q, k, v = q_ref[rows(ci), :], k_ref[rows(ci), :], v_ref[rows(ci), :]
        scores = (_dot_nt(q, k) * decay).astype(BF16)
        o_ref[rows(ci), :] = _dot(scores, v)
        vt_ref[ci] = v.astype(F32).T.astype(BF16)

    ctx = n_chunks - 1
    order_f = [ctx] + list(range(ctx))
    order_b = [ctx] + list(range(ctx - 1, -1, -1))
    sf_ref[...] = jnp.zeros_like(sf_ref)
    sb_ref[...] = jnp.zeros_like(sb_ref)
    for step in range(n_chunks):
        for order, s_ref, qd, kd, end in ((order_f, sf_ref, qf, kfs, end_f), (order_b, sb_ref, qb, kbs, end_b)):
            ci = order[step]
            if step > 0:
                o_ref[rows(ci), :] += _dot_nt(q_ref[rows(ci), :], s_ref[...].astype(BF16)) * qd
            if step < n_chunks - 1:
                ks = (k_ref[rows(ci), :].astype(F32) * kd).astype(BF16)
                upd = _dot(vt_ref[ci], ks)
                s_ref[...] = upd if step == 0 else s_ref[...] * end + upd


def _retention(qk, v):
    bsz, t, _ = qk.shape
    n_chunks = t // RET_CHUNK
    mats, tabs = _ret_tables()
    return pl.pallas_call(
        functools.partial(_ret_kernel, n_chunks=n_chunks),
        grid=(bsz, RET_HEADS),
        in_specs=[
            pl.BlockSpec((None, t, RET_DK), lambda b, h: (b, 0, h)),
            pl.BlockSpec((None, t, RET_DK), lambda b, h: (b, 0, RET_HEADS + h)),
            pl.BlockSpec((None, t, RET_DV), lambda b, h: (b, 0, h)),
            pl.BlockSpec((None, RET_CHUNK, RET_CHUNK), lambda b, h: (h, 0, 0)),
            pl.BlockSpec((None, RET_CHUNK, LANES), lambda b, h: (h, 0, 0)),
        ],
        out_specs=pl.BlockSpec((None, t, RET_DV), lambda b, h: (b, 0, h)),
        out_shape=jax.ShapeDtypeStruct((bsz, t, RET_HEADS * RET_DV), F32),
        scratch_shapes=[
            pltpu.VMEM((n_chunks, RET_DV, RET_CHUNK), BF16),
            pltpu.VMEM((RET_DV, RET_DK), F32),
            pltpu.VMEM((RET_DV, RET_DK), F32),
        ],
        compiler_params=_cparams(("parallel", "parallel")),
    )(qk, qk, v, mats, tabs)


def _hg_tables():
    c, sub = HG_CHUNK, HG_SUB
    r = np.arange(c)[:, None]
    col = np.arange(c)[None, :]
    masks = []
    for lower in (True, False):
        s = c // 2
        while s >= sub:
            same = (r // (2 * s)) == (col // (2 * s))
            rh, ch = (r // s) % 2, (col // s) % 2
            masks.append(same & ((rh == 1) & (ch == 0) if lower else (rh == 0) & (ch == 1)))
            s //= 2
    blockdiag = (r // sub) == (col // sub)
    masks.append(blockdiag & (col <= r))
    masks.append(blockdiag & (col >= r))
    tri = np.stack([col <= r, col >= r]).astype(np.float32)
    place = np.stack([np.broadcast_to((np.arange(c) % sub) == j, (c, c)) for j in range(sub)]).astype(np.float32)
    return jnp.asarray(np.stack(masks).astype(np.float32)), jnp.asarray(tri, BF16), jnp.asarray(place, BF16)


def _split3(g):
    hi = g.astype(BF16)
    r1 = g - hi.astype(F32)
    mid = r1.astype(BF16)
    lo = (r1 - mid.astype(F32)).astype(BF16)
    return hi, mid, lo


def _hg_intra(q, k, b, masks, place, lower):
    c, sub = HG_CHUNK, HG_SUB
    n_levels = int(np.log2(c // sub))
    total = jnp.zeros((c, c), F32)
    s = c // 2
    for level in range(n_levels):
        pieces = []
        for m in range(0, c, 2 * s):
            r = m + s - 1 if lower else m + s
            pieces.append(jnp.broadcast_to(b[r:r + 1, :], (2 * s, b.shape[1])))
        ref = pieces[0] if len(pieces) == 1 else jnp.concatenate(pieces, axis=0)
        diff = b - ref
        w = jnp.exp2(jnp.minimum(diff, -diff))
        total = total + _dot_nt((q * w).astype(BF16), (k * w).astype(BF16)) * masks[(0 if lower else n_levels) + level]
        s //= 2
    shape3 = (c // sub, sub, q.shape[1])
    q3, b3 = q.reshape(shape3), b.reshape(shape3)
    a3 = (jnp.log2(jnp.maximum(k, 0.0)) - b).reshape(shape3)
    acc = jnp.zeros((c, c), F32)
    for jj in range(sub):
        aj = jnp.broadcast_to(a3[:, jj:jj + 1, :], shape3)
        p = q3 * jnp.exp2(jnp.minimum(b3 + aj, 0.0))
        acc = acc + _dot(p.reshape(c, q.shape[1]).astype(BF16), place[jj])
    return total + acc * masks[2 * n_levels + (0 if lower else 1)]


def _hg_kernel(q_ref, kf_ref, kb_ref, gf_ref, gb_ref, v_ref, mask_ref, tri_ref, place_ref, o_ref,
               bf_ref, bb_ref, vt_ref, s_ref, *, n_chunks):
    c = HG_CHUNK
    masks = [mask_ref[i] for i in range(mask_ref.shape[0])]
    place = [place_ref[i] for i in range(place_ref.shape[0])]
    tri_lo, tri_up = tri_ref[0], tri_ref[1]

    def cumsum(tri, g):
        hi, mid, lo = _split3(g)
        return (_dot(tri, hi) + _dot(tri, mid) + _dot(tri, lo)) * LOG2_E

    def chunk_rows(ci):
        return pl.ds(ci * c if isinstance(ci, int) else pl.multiple_of(ci * c, c), c)

    def finish(ci, scores, v):
        o_ref[chunk_rows(ci), :] = _dot(scores, v)
        vt_ref[ci] = v.astype(F32).T.astype(BF16)

    first = chunk_rows(0)
    bf_ref[first, :] = cumsum(tri_lo, gf_ref[first, :])
    bb_ref[first, :] = cumsum(tri_up, gb_ref[first, :])
    s_ref[first, :] = jnp.zeros((c, c), BF16)

    def intra(ci, carry):
        prev, nxt = jnp.maximum(ci - 1, 0), jnp.minimum(ci + 1, n_chunks - 1)
        rows, rows_p, rows_n = chunk_rows(ci), chunk_rows(prev), chunk_rows(nxt)
        bf, bb = bf_ref[rows, :], bb_ref[rows, :]
        q, kf, kb = q_ref[rows, :].astype(F32), kf_ref[rows, :], kb_ref[rows, :]
        s_prev, v_prev = s_ref[rows_p, :], v_ref[rows_p, :]
        gf_n, gb_n = gf_ref[rows_n, :], gb_ref[rows_n, :]
        finish(prev, s_prev, v_prev)
        bf_ref[rows_n, :] = cumsum(tri_lo, gf_n)
        bb_ref[rows_n, :] = cumsum(tri_up, gb_n)
        scores = _hg_intra(q, kf, bf, masks, place, True) + _hg_intra(q, kb, bb, masks, place, False)
        s_ref[rows, :] = scores.astype(BF16)
        return carry

    lax.fori_loop(0, n_chunks, intra, 0)
    last = chunk_rows(n_chunks - 1)
    finish(n_chunks - 1, s_ref[last, :], v_ref[last, :])

    n_ctx = TOKEN_BLOCK // c
    n_lat = n_chunks - n_ctx

    def scan_step(ci, st, k_ref, b_ref, end_row):
        rows = pl.ds(pl.multiple_of(ci * c, c), c)
        q = q_ref[rows, :].astype(F32)
        b = b_ref[rows, :]
        qs = (q * jnp.exp2(b)).astype(BF16)
        o_ref[rows, :] += _dot_nt(qs, st.astype(BF16))
        b_end = b[end_row:end_row + 1, :]
        ks = (k_ref[rows, :] * jnp.exp2(b_end - b)).astype(BF16)
        return st * jnp.exp2(b_end) + _dot(vt_ref[ci], ks)

    def scan(step, carry):
        st_f, st_b = carry
        cf = jnp.where(step < n_ctx, n_lat + step, step - n_ctx)
        cb = n_chunks - 1 - step
        st_f = scan_step(cf, st_f, kf_ref, bf_ref, c - 1)
        st_b = scan_step(cb, st_b, kb_ref, bb_ref, 0)
        return st_f, st_b

    zero = jnp.zeros((HG_DV, HG_DK), F32)
    lax.fori_loop(0, n_chunks, scan, (zero, zero))


def _hgrn2(q, kf, kb, gf, gb, v):
    bsz, t, _ = q.shape
    n_chunks = t // HG_CHUNK
    masks, tri, place = _hg_tables()
    head = lambda b, h: (b, 0, h)
    spec = pl.BlockSpec((None, t, HG_DK), head)
    return pl.pallas_call(
        functools.partial(_hg_kernel, n_chunks=n_chunks),
        grid=(bsz, HG_HEADS),
        in_specs=[spec, spec, spec, spec, spec, spec,
                  pl.BlockSpec(masks.shape, lambda b, h: (0, 0, 0)),
                  pl.BlockSpec(tri.shape, lambda b, h: (0, 0, 0)),
                  pl.BlockSpec(place.shape, lambda b, h: (0, 0, 0))],
        out_specs=pl.BlockSpec((None, t, HG_DV), head),
        out_shape=jax.ShapeDtypeStruct((bsz, t, HG_HEADS * HG_DV), F32),
        scratch_shapes=[
            pltpu.VMEM((t, HG_DK), F32),
            pltpu.VMEM((t, HG_DK), F32),
            pltpu.VMEM((n_chunks, HG_DV, HG_CHUNK), BF16),
            pltpu.VMEM((t, HG_CHUNK), BF16),
        ],
        compiler_params=_cparams(("parallel", "parallel")),
    )(q, kf, kb, gf, gb, v, masks, tri, place)


def _split_lanes(aff):
    hi, mid, lo = _split3(aff)
    packed = hi.astype(F32) + pltpu.roll(mid.astype(F32), N_EXPERTS, 1) + pltpu.roll(lo.astype(F32), 2 * N_EXPERTS, 1)
    return packed.astype(BF16)


def _readout_kernel(o_ref, gt_ref, x_ref, mod_ref, wout_ref, gn_ref, nw_ref, wra_ref, wrb_ref,
                    xo_ref, hb_ref, aff_ref, asplit_ref, *, n_heads, dv):
    o = o_ref[...]
    gn = gn_ref[...]
    parts = []
    for h in range(n_heads):
        oh = o[:, h * dv:(h + 1) * dv]
        ms = jnp.mean(oh * oh, axis=-1, keepdims=True)
        parts.append(oh * lax.rsqrt(ms + EPS) * gn)
    z = _silu(gt_ref[...]) * jnp.concatenate(parts, axis=1)
    x = x_ref[...] + mod_ref[2:3, :] * _dot(z.astype(BF16), wout_ref[...])
    xo_ref[...] = x
    h2 = _rms_mod(x, nw_ref[...], mod_ref[3:4, :], mod_ref[4:5, :])
    h_hi = h2.astype(BF16)
    hb_ref[...] = h_hi
    h_lo = (h2 - h_hi.astype(F32)).astype(BF16)
    both = _dot(h_hi, wra_ref[...])
    logits = both[:, :LANES] + both[:, LANES:] + _dot(h_lo, wrb_ref[...])
    lane = lax.broadcasted_iota(jnp.int32, logits.shape, 1)
    logits = jnp.where(lane < N_EXPERTS, logits, -1e30)
    e = jnp.exp(logits - jnp.max(logits, axis=-1, keepdims=True))
    aff = e / jnp.sum(e, axis=-1, keepdims=True)
    aff_ref[...] = aff
    asplit_ref[...] = _split_lanes(aff)


def _readout(o, gt, x, mods, w_out, g_norm, nw_ffn, w_router, n_heads, dv):
    w_router_hi = w_router.astype(BF16)
    w_router_lo = (w_router - w_router_hi.astype(F32)).astype(BF16)
    w_router_hi_lo = jnp.concatenate([w_router_hi, w_router_lo], axis=1)
    bsz, t, d = x.shape
    hv = n_heads * dv
    nt = t // TOKEN_BLOCK
    ctx_row = mods.shape[0] - 1
    tok = lambda b, i: (b, i, 0)
    full2 = lambda b, i: (0, 0)
    return pl.pallas_call(
        functools.partial(_readout_kernel, n_heads=n_heads, dv=dv),
        grid=(bsz, nt),
        in_specs=[
            pl.BlockSpec((None, TOKEN_BLOCK, hv), tok),
            pl.BlockSpec((None, TOKEN_BLOCK, hv), tok),
            pl.BlockSpec((None, TOKEN_BLOCK, d), tok),
            pl.BlockSpec((None, 6, d), lambda b, i: (jnp.where(i == nt - 1, ctx_row, b), 0, 0)),
            pl.BlockSpec((hv, d), full2),
            pl.BlockSpec((1, dv), full2),
            pl.BlockSpec((1, d), full2),
            pl.BlockSpec((d, 2 * LANES), full2),
            pl.BlockSpec((d, LANES), full2),
        ],
        out_specs=[
            pl.BlockSpec((None, TOKEN_BLOCK, d), tok),
            pl.BlockSpec((None, TOKEN_BLOCK, d), tok),
            pl.BlockSpec((None, TOKEN_BLOCK, LANES), tok),
            pl.BlockSpec((None, TOKEN_BLOCK, LANES), tok),
        ],
        out_shape=[
            jax.ShapeDtypeStruct((bsz, t, d), F32),
            jax.ShapeDtypeStruct((bsz, t, d), BF16),
            jax.ShapeDtypeStruct((bsz, t, LANES), F32),
            jax.ShapeDtypeStruct((bsz, t, LANES), BF16),
        ],
        compiler_params=_cparams(("parallel", "parallel")),
    )(o, gt, x, mods, w_out, g_norm, nw_ffn, w_router_hi_lo, w_router_hi)


def _excl_cumsum_rows(x01, tri_strict):
    blk = tri_strict.shape[0]
    carry = jnp.zeros((1, x01.shape[1]), F32)
    out = []
    for r0 in range(0, x01.shape[0], blk):
        xb = x01[r0:r0 + blk, :]
        out.append(_dot(tri_strict, xb.astype(BF16)) + carry)
        carry = carry + jnp.sum(xb, axis=0, keepdims=True)
    return out[0] if len(out) == 1 else jnp.concatenate(out, axis=0)


def _route_kernel(aff_ref, tri_ref, pos_ref, *, segments):
    tri_strict = tri_ref[...]
    for r0, n, cap in segments:
        if cap == 0:
            pos_ref[r0:r0 + n, :] = jnp.full((n, LANES), -1.0, F32)
            continue
        aff = aff_ref[r0:r0 + n, :]

        def as_float(pattern):
            return lax.bitcast_convert_type(pattern, F32)

        def search(i, thr):
            cand = thr | jnp.left_shift(jnp.int32(1), 30 - i)
            cnt = jnp.sum(jnp.where(aff >= as_float(cand), 1.0, 0.0), axis=0, keepdims=True)
            return jnp.where(cnt >= cap, cand, thr)

        thr = lax.fori_loop(0, 31, search, jnp.zeros((1, LANES), jnp.int32))
        beyond = as_float(thr + 1)
        above = jnp.where(aff >= beyond, 1.0, 0.0)
        tied = jnp.where((aff >= as_float(thr)) & (aff < beyond), 1.0, 0.0)
        need = cap - jnp.sum(above, axis=0, keepdims=True)
        tie_rank = _excl_cumsum_rows(tied, tri_strict)
        sel = above + tied * jnp.where(tie_rank < need, 1.0, 0.0)
        slot = _excl_cumsum_rows(sel, tri_strict)
        pos_ref[r0:r0 + n, :] = jnp.where(sel > 0.0, slot, -1.0)


def _route(aff, segments):
    bsz, t, _ = aff.shape
    r = np.arange(TOKEN_BLOCK)
    tri_strict = jnp.asarray((r[None, :] < r[:, None]).astype(np.float32), BF16)
    return pl.pallas_call(
        functools.partial(_route_kernel, segments=segments),
        grid=(bsz,),
        in_specs=[pl.BlockSpec((None, t, LANES), lambda b: (b, 0, 0)),
                  pl.BlockSpec(tri_strict.shape, lambda b: (0, 0))],
        out_specs=pl.BlockSpec((None, t, LANES), lambda b: (b, 0, 0)),
        out_shape=jax.ShapeDtypeStruct((bsz, t, LANES), F32),
        compiler_params=_cparams(("parallel",)),
    )(aff, tri_strict)


def _onehot(cond):
    return jnp.where(cond, 1.0, 0.0).astype(BF16)


def _gather_kernel(hb_ref, asplit_ref, pos_ref, xe_ref, gate_ref, *, segments):
    e = pl.program_id(1)
    pos = pos_ref[...]
    out0 = 0
    for r0, n, cap in segments:
        if cap == 0:
            continue
        slot = lax.broadcasted_iota(jnp.int32, (cap, n), 0).astype(F32)
        g = _onehot(pos[:, r0:r0 + n] == slot)
        xe_ref[out0:out0 + cap, :] = _dot(g, hb_ref[r0:r0 + n, :]).astype(BF16)
        pieces = _dot(g, asplit_ref[r0:r0 + n, :])
        lane = lax.broadcasted_iota(jnp.int32, pieces.shape, 1)
        mine = (lane < 3 * N_EXPERTS) & ((lane & (N_EXPERTS - 1)) == e)
        gate = jnp.sum(jnp.where(mine, pieces, 0.0), axis=1, keepdims=True)
        gate_ref[out0:out0 + cap, :] = jnp.broadcast_to(gate, pieces.shape)
        out0 += cap


def _gather(hb, asplit, pos_rows, segments):
    bsz, t, d = hb.shape
    rows = sum(cap for _, _, cap in segments)
    slots = lambda b, e: (e * bsz + b, 0, 0)
    return pl.pallas_call(
        functools.partial(_gather_kernel, segments=segments),
        grid=(bsz, N_EXPERTS),
        in_specs=[pl.BlockSpec((None, t, d), lambda b, e: (b, 0, 0)),
                  pl.BlockSpec((None, t, LANES), lambda b, e: (b, 0, 0)),
                  pl.BlockSpec((None, 1, t), lambda b, e: (b * N_EXPERTS + e, 0, 0))],
        out_specs=[pl.BlockSpec((None, rows, d), slots), pl.BlockSpec((None, rows, LANES), slots)],
        out_shape=[jax.ShapeDtypeStruct((N_EXPERTS * bsz, rows, d), BF16),
                   jax.ShapeDtypeStruct((N_EXPERTS * bsz, rows, LANES), F32)],
        compiler_params=_cparams(("parallel", "parallel")),
    )(hb, asplit, pos_rows)


def _ffn_kernel(xe_ref, gate_ref, wg_ref, wu_ref, wd_ref, y_ref, acc_ref, *, row_chunk):
    f = pl.program_id(1)
    wg = wg_ref[...].astype(BF16)
    wu = wu_ref[...].astype(BF16)
    wd = wd_ref[...].astype(BF16)

    @pl.when(f == 0)
    def _():
        acc_ref[...] = jnp.zeros_like(acc_ref)

    for r0 in range(0, xe_ref.shape[0], row_chunk):
        xe = xe_ref[r0:r0 + row_chunk, :]
        hidden = (_silu(_dot(xe, wg)) * _dot(xe, wu)).astype(BF16)
        acc_ref[r0:r0 + row_chunk, :] += _dot(hidden, wd)

    @pl.when(f == pl.num_programs(1) - 1)
    def _():
        y_ref[...] = (acc_ref[...] * gate_ref[:, 0:1]).astype(y_ref.dtype)


def _ffn(xe, gate, w_gate, w_up, w_down, layer):
    n_rows, d = xe.shape[1], xe.shape[2]
    ff = w_gate.shape[-1]
    row_chunk = n_rows // 2 if n_rows % 2 == 0 and (n_rows // 2) % 16 == 0 else n_rows
    return pl.pallas_call(
        functools.partial(_ffn_kernel, row_chunk=row_chunk),
        grid=(N_EXPERTS, ff // FF_TILE),
        in_specs=[
            pl.BlockSpec((None, n_rows, d), lambda e, f: (e, 0, 0)),
            pl.BlockSpec((None, n_rows, LANES), lambda e, f: (e, 0, 0)),
            pl.BlockSpec((None, None, d, FF_TILE), lambda e, f: (layer, e, 0, f)),
            pl.BlockSpec((None, None, d, FF_TILE), lambda e, f: (layer, e, 0, f)),
            pl.BlockSpec((None, None, FF_TILE, d), lambda e, f: (layer, e, f, 0)),
        ],
        out_specs=pl.BlockSpec((None, n_rows, d), lambda e, f: (e, 0, 0)),
        out_shape=jax.ShapeDtypeStruct((N_EXPERTS, n_rows, d), BF16),
        scratch_shapes=[pltpu.VMEM((n_rows, d), F32)],
        compiler_params=_cparams(("parallel", "arbitrary")),
    )(xe, gate, w_gate, w_up, w_down)


def _scatter_kernel(x_ref, y_ref, pos_ref, mods_ref, nw_ref, *rest, cap, slot0, ctx_mods, final, sub):
    out_ref = rest[-1]
    mod_row = mods_ref.shape[0] - 1 if ctx_mods else pl.program_id(0)
    gate = mods_ref[mod_row][5:6, :]
    d = y_ref.shape[-1]
    y_all = y_ref[:, slot0:slot0 + cap, :].reshape(N_EXPERTS * cap, d)
    slot = lax.broadcasted_iota(jnp.int32, (sub, cap), 1).astype(F32)
    for c0 in range(0, x_ref.shape[0], sub):
        pos = pos_ref[c0:c0 + sub, :]
        onehots = jnp.concatenate([_onehot(pos[:, e:e + 1] == slot) for e in range(N_EXPERTS)], axis=1)
        x = x_ref[c0:c0 + sub, :] + gate * _dot(onehots, y_all)
        if final:
            x = x * lax.rsqrt(jnp.mean(x * x, axis=-1, keepdims=True) + EPS) * nw_ref[...]
        out_ref[c0:c0 + sub, :] = x


def _scatter(x, y, pos, mods, nw_final, r0, n, cap, slot0, tm, out_rows, ctx_mods, final, carry=None):
    bsz, t, d = x.shape
    n_slots = y.shape[2]
    blk0 = r0 // tm
    tok = lambda b, i: (b, blk0 + i, 0)
    in_specs = [
        pl.BlockSpec((None, tm, d), tok),
        pl.BlockSpec((N_EXPERTS, None, n_slots, d), lambda b, i: (0, b, 0, 0)),
        pl.BlockSpec((None, tm, LANES), tok),
        pl.BlockSpec(mods.shape, lambda b, i: (0, 0, 0)),
        pl.BlockSpec((1, d), lambda b, i: (0, 0)),
    ]
    args = [x, y, pos, mods, nw_final]
    aliases = {}
    if carry is not None:
        in_specs.append(pl.BlockSpec(memory_space=pl.ANY))
        args.append(carry)
        aliases = {len(args) - 1: 0}
    return pl.pallas_call(
        functools.partial(_scatter_kernel, cap=cap, slot0=slot0, ctx_mods=ctx_mods, final=final, sub=min(tm, 256)),
        grid=(bsz, n // tm),
        in_specs=in_specs,
        out_specs=pl.BlockSpec((None, tm, d), tok),
        out_shape=jax.ShapeDtypeStruct((bsz, out_rows, d), F32),
        input_output_aliases=aliases,
        compiler_params=_cparams(("parallel", "parallel")),
    )(*args)


def _moe(x, hb, aff, asplit, mods, w_gate, w_up, w_down, layer, nw_final, with_ctx, final):
    bsz, t, d = x.shape
    seq = t - TOKEN_BLOCK
    cap_lat = EC_CAPACITY * seq // N_EXPERTS
    cap_ctx = EC_CAPACITY * TOKEN_BLOCK // N_EXPERTS if with_ctx else 0
    segments = ((0, seq, cap_lat), (seq, TOKEN_BLOCK, cap_ctx))
    pos = _route(aff, segments)
    pos_rows = jnp.swapaxes(pos[:, :, :N_EXPERTS], 1, 2).reshape(bsz * N_EXPERTS, 1, t)
    xe, gate = _gather(hb, asplit, pos_rows, segments)
    rows = xe.shape[1]
    y = _ffn(xe.reshape(N_EXPERTS, bsz * rows, d), gate.reshape(N_EXPERTS, bsz * rows, LANES),
             w_gate, w_up, w_down, layer)
    y = y.reshape(N_EXPERTS, bsz, rows, d)
    out = _scatter(x, y, pos, mods, nw_final, 0, seq, cap_lat, 0, 512, seq if final else t, False, final)
    if with_ctx:
        out = _scatter(x, y, pos, mods, nw_final, seq, TOKEN_BLOCK, cap_ctx, cap_lat, TOKEN_BLOCK, t, True, final,
                       carry=out)
    return out


def _rope_tables(seq, n_ctx):
    half = RET_DK // 2
    t = jnp.arange(seq)
    inv = ROPE_THETA ** (-jnp.arange(0, half, 2, dtype=F32) / half)
    tabs = []
    for pos in ((t // GRID_W).astype(F32), (t % GRID_W).astype(F32)):
        ang = pos[:, None] * inv
        cos, sin = jnp.cos(ang), jnp.sin(ang)
        tabs.append(jnp.concatenate([cos, cos], axis=1))
        tabs.append(jnp.concatenate([-sin, sin], axis=1))
    ident = [jnp.ones, jnp.zeros, jnp.ones, jnp.zeros]
    return [jnp.concatenate([tab, fn((n_ctx, 2 * (half // 2)), F32)], axis=0) for tab, fn in zip(tabs, ident)]


def kernel(x, c, ctx, c_ctx, w_ada, b_ada, norm_mix, norm_ffn, ret_w_in, ret_w_out, hg_w_in, hg_g_norm,
           hg_lower_bounds, hg_w_out, moe_router, moe_w_gate, moe_w_up, moe_w_down, norm_final):
    bsz, seq, d = x.shape
    n_ctx = ctx.shape[1]
    depth = w_ada.shape[0]
    assert n_ctx == TOKEN_BLOCK and seq % TOKEN_BLOCK == 0 and d == RET_HEADS * RET_DK == HG_HEADS * HG_DK

    xs = jnp.concatenate([x, ctx], axis=1)
    cc = jnp.concatenate([c, c_ctx[None, :], jnp.zeros((16 - bsz - 1, d), F32)], axis=0)
    mods_all = _ada(cc, w_ada, b_ada)[:, :bsz + 1].reshape(depth, bsz + 1, 6, d)
    t = seq + n_ctx
    rope = _rope_tables(seq, n_ctx)
    rope_specs = [pl.BlockSpec((t // 2, LANES), lambda b, i, j: (i, 0))] * 4
    w_router = jnp.pad(moe_router, ((0, 0), (0, 0), (0, LANES - N_EXPERTS)))
    ones_dv = jnp.ones((1, RET_DV), F32)

    for layer in range(depth):
        last = layer == depth - 1
        slot = layer // 2
        mods = mods_all[layer]
        h = _prenorm(xs, mods, norm_mix[layer][None, :], seq, 0)
        if layer % 2 == 0:
            assert RET_HEADS * RET_DK == PROJ_COLS
            w = ret_w_in[slot]
            w = jnp.concatenate([w[:, :PROJ_COLS], w[:, PROJ_COLS:2 * PROJ_COLS] * (RET_DK ** -0.5),
                                 w[:, 2 * PROJ_COLS:]], axis=1).astype(BF16)
            q, = _proj(h, w, 0, 1, _ep_rope, [BF16], rope, rope_specs)
            k, = _proj(h, w, 1, 1, _ep_rope, [BF16], rope, rope_specs)
            v, = _proj(h, w, 2, 2, _ep_plain, [BF16])
            gt, = _proj(h, w, 4, 2, _ep_plain, [F32])
            o = _retention(jnp.concatenate([q, k], axis=-1), v)
            w_out, g_norm, n_heads, dv = ret_w_out[slot].astype(BF16), ones_dv, RET_HEADS, RET_DV
        else:
            assert HG_HEADS * HG_DK == PROJ_COLS
            w = hg_w_in[slot].astype(BF16)
            lb_specs = [pl.BlockSpec((depth, PROJ_COLS), lambda b, i, j: (0, 0))]
            ep_gate = functools.partial(_ep_gate, layer=layer)
            q, = _proj(h, w, 0, 1, _ep_silu, [BF16])
            kf, gf = _proj(h, w, 1, 1, ep_gate, [F32, F32], (hg_lower_bounds[:, 0, :],), lb_specs)
            kb, gb = _proj(h, w, 2, 1, ep_gate, [F32, F32], (hg_lower_bounds[:, 1, :],), lb_specs)
            v, = _proj(h, w, 3, 1, _ep_plain, [BF16])
            gt, = _proj(h, w, 4, 1, _ep_plain, [F32])
            o = _hgrn2(q, kf, kb, gf, gb, v)
            w_out, g_norm, n_heads, dv = hg_w_out[slot].astype(BF16), hg_g_norm[slot][None, :], HG_HEADS, HG_DV
        xs, hb, aff, asplit = _readout(o, gt, xs, mods, w_out, g_norm, norm_ffn[layer][None, :], w_router[layer],
                                       n_heads, dv)
        xs = _moe(xs, hb, aff, asplit, mods, moe_w_gate, moe_w_up, moe_w_down, layer, norm_final[None, :],
                  with_ctx=not last, final=last)
    return xs
```

```python
import functools

import numpy as np
import jax
import jax.numpy as jnp
from jax import lax
from jax.experimental import pallas as pl
from jax.experimental.pallas import tpu as pltpu

F32 = jnp.float32
BF16 = jnp.bfloat16
EPS = 1e-6
ROPE_THETA = 10000.0
GRID_W = 64

LANES = 128
TOKEN_BLOCK = 256
RET_HEADS, RET_DK, RET_DV, RET_CHUNK = 4, 256, 512, 256
HG_HEADS, HG_DK, HG_DV, HG_CHUNK, HG_SUB = 8, 128, 128, 128, 8
HG_HEADS_PER_STEP = 2
READOUT_SUBBLOCKS = 3
LOG2_E = 1.4426950408889634
N_EXPERTS = 16
EC_CAPACITY = 2
FF_TILE = 256
PROJ_COLS = 1024
VMEM_LIMIT = 56 * 1024 * 1024

HIGHEST = lax.Precision.HIGHEST


def _cparams(sem):
    return pltpu.CompilerParams(dimension_semantics=sem, vmem_limit_bytes=VMEM_LIMIT)


def _dot(a, b, **kw):
    return jnp.dot(a, b, preferred_element_type=F32, **kw)


def _dot_nt(a, b):
    return lax.dot_general(a, b, (((1,), (1,)), ((), ())), preferred_element_type=F32)


def _silu(x):
    return x * jax.nn.sigmoid(x)


def _rms_mod(x, nw, shift, scale):
    ms = jnp.mean(x * x, axis=-1, keepdims=True)
    h = x * lax.rsqrt(ms + EPS) * nw
    return h * (1.0 + scale) + shift


def _ada_kernel(c_ref, w_ref, b_ref, o_ref):
    o_ref[...] = _dot(_silu(c_ref[...]), w_ref[...], precision=HIGHEST) + b_ref[...]


def _ada(cc, w_ada, b_ada):
    depth, d, d6 = w_ada.shape
    rows = cc.shape[0]
    return pl.pallas_call(
        _ada_kernel,
        grid=(depth, d6 // d),
        in_specs=[
            pl.BlockSpec((rows, d), lambda l, j: (0, 0)),
            pl.BlockSpec((None, d, d), lambda l, j: (l, 0, j)),
            pl.BlockSpec((None, 1, d), lambda l, j: (l, 0, j)),
        ],
        out_specs=pl.BlockSpec((None, rows, d), lambda l, j: (l, 0, j)),
        out_shape=jax.ShapeDtypeStruct((depth, rows, d6), F32),
        compiler_params=_cparams(("parallel", "parallel")),
    )(cc, w_ada, b_ada.reshape(depth, 1, d6))


def _prenorm_kernel(x_ref, mods_ref, nw_ref, h_ref, *, seq, shift_row):
    ctx_row = mods_ref.shape[0] - 1
    nw = nw_ref[...]
    n_ctx = x_ref.shape[0] - seq
    for r0, n, row in ((0, seq, pl.program_id(0)), (seq, n_ctx, ctx_row)):
        m = mods_ref[row]
        shift, scale = m[shift_row:shift_row + 1, :], m[shift_row + 1:shift_row + 2, :]

        def body(i, carry):
            rows = pl.ds(pl.multiple_of(r0 + i * TOKEN_BLOCK, TOKEN_BLOCK), TOKEN_BLOCK)
            h_ref[rows, :] = _rms_mod(x_ref[rows, :], nw, shift, scale).astype(BF16)
            return carry

        lax.fori_loop(0, n // TOKEN_BLOCK, body, 0)


def _prenorm(x, mods, nw, seq, shift_row):
    bsz, t, d = x.shape
    return pl.pallas_call(
        functools.partial(_prenorm_kernel, seq=seq, shift_row=shift_row),
        grid=(bsz,),
        in_specs=[pl.BlockSpec((None, t, d), lambda b: (b, 0, 0)),
                  pl.BlockSpec(mods.shape, lambda b: (0, 0, 0)),
                  pl.BlockSpec((1, d), lambda b: (0, 0))],
        out_specs=pl.BlockSpec((None, t, d), lambda b: (b, 0, 0)),
        out_shape=jax.ShapeDtypeStruct((bsz, t, d), BF16),
        compiler_params=_cparams(("parallel",)),
    )(x, mods, nw)


def _proj_kernel(h_ref, w_ref, *rest, epilogue, n_extra):
    extras, outs = rest[:n_extra], rest[n_extra:]
    epilogue(_dot(h_ref[...], w_ref[...]), extras, outs)


def _ep_plain(acc, extras, outs):
    outs[0][...] = acc.astype(outs[0].dtype)


def _ep_silu(acc, extras, outs):
    outs[0][...] = _silu(acc).astype(outs[0].dtype)


def _ep_rope(acc, extras, outs):
    c_row, s_row, c_col, s_col = (r[...] for r in extras)
    for s in range(acc.shape[1] // LANES):
        slab = acc[:, s * LANES:(s + 1) * LANES]
        cos, sin = (c_row, s_row) if s % 2 == 0 else (c_col, s_col)
        rot = slab * cos + pltpu.roll(slab, LANES // 2, 1) * sin
        outs[0][:, s * LANES:(s + 1) * LANES] = rot.astype(outs[0].dtype)


def _ep_gate(acc, extras, outs, *, layer):
    raw = extras[0][...]
    e = jnp.exp(raw - jnp.max(raw, axis=0, keepdims=True))
    sm = e / jnp.sum(e, axis=0, keepdims=True)
    lb = -sm[0:1, :]
    for l in range(layer + 1):
        lb = lb + sm[l:l + 1, :]
    f = lb + (1.0 - lb) * jax.nn.sigmoid(acc)
    outs[0][...] = 1.0 - f
    outs[1][...] = jnp.log(f)


def _proj(h, w, col_block0, ncol, epilogue, out_dtypes, extras=(), extra_specs=()):
    bsz, t, d = h.shape
    tm, tn = t // 2, PROJ_COLS
    in_specs = [
        pl.BlockSpec((None, tm, d), lambda b, i, j: (b, i, 0)),
        pl.BlockSpec((d, tn), lambda b, i, j: (0, col_block0 + j)),
    ] + list(extra_specs)
    out_specs = [pl.BlockSpec((None, tm, tn), lambda b, i, j: (b, i, j)) for _ in out_dtypes]
    out_shape = [jax.ShapeDtypeStruct((bsz, t, tn * ncol), dt) for dt in out_dtypes]
    return pl.pallas_call(
        functools.partial(_proj_kernel, epilogue=epilogue, n_extra=len(extras)),
        grid=(bsz, 2, ncol), in_specs=in_specs, out_specs=out_specs, out_shape=out_shape,
        compiler_params=_cparams(("parallel", "parallel", "parallel")),
    )(h, w, *extras)


def _ret_tables():
    c = RET_CHUNK
    j = np.arange(2 * RET_HEADS, dtype=np.float64)
    lg = np.log1p(-np.exp2(-5.0 - j / 2))
    lg_f, lg_b = lg[0::2], lg[1::2]
    i = np.arange(c, dtype=np.float64)
    diff = i[:, None] - i[None, :]
    mats = np.zeros((RET_HEADS, c, c), np.float64)
    tabs = np.zeros((RET_HEADS, c, LANES), np.float64)
    for h in range(RET_HEADS):
        mats[h] = np.where(diff >= 0, np.exp(diff * lg_f[h]), 0.0) + np.where(diff <= 0, np.exp(-diff * lg_b[h]), 0.0)
        tabs[h, :, 0] = np.exp((i + 1) * lg_f[h])
        tabs[h, :, 1] = np.exp((c - 1 - i) * lg_f[h])
        tabs[h, :, 2] = np.exp((c - i) * lg_b[h])
        tabs[h, :, 3] = np.exp(i * lg_b[h])
        tabs[h, :, 4] = np.exp(c * lg_f[h])
        tabs[h, :, 5] = np.exp(c * lg_b[h])
    return jnp.asarray(mats, F32), jnp.asarray(tabs, F32)


def _ret_kernel(q_ref, k_ref, v_ref, m_ref, tab_ref, o_ref, vt_ref, sf_ref, sb_ref, *, n_chunks):
    c = RET_CHUNK
    decay = m_ref[...]
    tab = tab_ref[...]
    qf, kfs, qb, kbs = tab[:, 0:1], tab[:, 1:2], tab[:, 2:3], tab[:, 3:4]
    end_f, end_b = tab[0:1, 4:5], tab[0:1, 5:6]

    def rows(ci):
        return slice(ci * c, (ci + 1) * c)

    for ci in range(n_chunks):
        q, k, v = q_ref[rows(ci), :], k_ref[rows(ci), :], v_ref[rows(ci), :]
        scores = (_dot_nt(q, k) * decay).astype(BF16)
        o_ref[rows(ci), :] = _dot(scores, v)
        vt_ref[ci] = v.astype(F32).T.astype(BF16)

    ctx = n_chunks - 1
    order_f = [ctx] + list(range(ctx))
    order_b = [ctx] + list(range(ctx - 1, -1, -1))
    sf_ref[...] = jnp.zeros_like(sf_ref)
    sb_ref[...] = jnp.zeros_like(sb_ref)
    for step in range(n_chunks):
        for order, s_ref, qd, kd, end in ((order_f, sf_ref, qf, kfs, end_f), (order_b, sb_ref, qb, kbs, end_b)):
            ci = order[step]
            if step > 0:
                o_ref[rows(ci), :] += _dot_nt(q_ref[rows(ci), :], s_ref[...].astype(BF16)) * qd
            if step < n_chunks - 1:
                ks = (k_ref[rows(ci), :].astype(F32) * kd).astype(BF16)
                upd = _dot(vt_ref[ci], ks)
                s_ref[...] = upd if step == 0 else s_ref[...] * end + upd


def _retention(q, k, v):
    bsz, t, _ = q.shape
    n_chunks = t // RET_CHUNK
    mats, tabs = _ret_tables()
    return pl.pallas_call(
        functools.partial(_ret_kernel, n_chunks=n_chunks),
        grid=(bsz, RET_HEADS),
        in_specs=[
            pl.BlockSpec((None, t, RET_DK), lambda b, h: (b, 0, h)),
            pl.BlockSpec((None, t, RET_DK), lambda b, h: (b, 0, h)),
            pl.BlockSpec((None, t, RET_DV), lambda b, h: (b, 0, h)),
            pl.BlockSpec((None, RET_CHUNK, RET_CHUNK), lambda b, h: (h, 0, 0)),
            pl.BlockSpec((None, RET_CHUNK, LANES), lambda b, h: (h, 0, 0)),
        ],
        out_specs=pl.BlockSpec((None, t, RET_DV), lambda b, h: (b, 0, h)),
        out_shape=jax.ShapeDtypeStruct((bsz, t, RET_HEADS * RET_DV), F32),
        scratch_shapes=[
            pltpu.VMEM((n_chunks, RET_DV, RET_CHUNK), BF16),
            pltpu.VMEM((RET_DV, RET_DK), F32),
            pltpu.VMEM((RET_DV, RET_DK), F32),
        ],
        compiler_params=_cparams(("parallel", "parallel")),
    )(q, k, v, mats, tabs)


def _hg_tables():
    c, sub = HG_CHUNK, HG_SUB
    r = np.arange(c)[:, None]
    col = np.arange(c)[None, :]
    masks = []
    for lower in (True, False):
        s = c // 2
        while s >= sub:
            same = (r // (2 * s)) == (col // (2 * s))
            rh, ch = (r // s) % 2, (col // s) % 2
            masks.append(same & ((rh == 1) & (ch == 0) if lower else (rh == 0) & (ch == 1)))
            s //= 2
    blockdiag = (r // sub) == (col // sub)
    masks.append(blockdiag & (col <= r))
    masks.append(blockdiag & (col >= r))
    tri = np.stack([col <= r, col >= r]).astype(np.float32)
    place = np.stack([np.broadcast_to((np.arange(c) % sub) == j, (c, c)) for j in range(sub)]).astype(np.float32)
    return jnp.asarray(np.stack(masks).astype(np.float32)), jnp.asarray(tri, BF16), jnp.asarray(place, BF16)


def _split3(g):
    hi = g.astype(BF16)
    r1 = g - hi.astype(F32)
    mid = r1.astype(BF16)
    lo = (r1 - mid.astype(F32)).astype(BF16)
    return hi, mid, lo


def _hg_intra(q, k, b, masks, place, lower):
    c, sub = HG_CHUNK, HG_SUB
    n_levels = int(np.log2(c // sub))
    total = jnp.zeros((c, c), F32)
    s = c // 2
    for level in range(n_levels):
        pieces = []
        for m in range(0, c, 2 * s):
            r = m + s - 1 if lower else m + s
            pieces.append(jnp.broadcast_to(b[r:r + 1, :], (2 * s, b.shape[1])))
        ref = pieces[0] if len(pieces) == 1 else jnp.concatenate(pieces, axis=0)
        diff = b - ref
        w = jnp.exp2(jnp.minimum(diff, -diff))
        total = total + _dot_nt((q * w).astype(BF16), (k * w).astype(BF16)) * masks[(0 if lower else n_levels) + level]
        s //= 2
    shape3 = (c // sub, sub, q.shape[1])
    q3, b3 = q.reshape(shape3), b.reshape(shape3)
    a3 = (jnp.log2(jnp.maximum(k, 0.0)) - b).reshape(shape3)
    acc = jnp.zeros((c, c), F32)
    for jj in range(sub):
        aj = jnp.broadcast_to(a3[:, jj:jj + 1, :], shape3)
        p = q3 * jnp.exp2(jnp.minimum(b3 + aj, 0.0))
        acc = acc + _dot(p.reshape(c, q.shape[1]).astype(BF16), place[jj])
    return total + acc * masks[2 * n_levels + (0 if lower else 1)]


def _hg_kernel(q_ref, kf_ref, kb_ref, gf_ref, gb_ref, v_ref, mask_ref, tri_ref, place_ref, o_ref,
               bf_ref, bb_ref, vt_ref, s_ref, *, n_chunks):
    c = HG_CHUNK
    masks = [mask_ref[i] for i in range(mask_ref.shape[0])]
    place = [place_ref[i] for i in range(place_ref.shape[0])]
    tri_lo, tri_up = tri_ref[0], tri_ref[1]

    def cumsum(tri, g):
        hi, mid, lo = _split3(g)
        return (_dot(tri, hi) + _dot(tri, mid) + _dot(tri, lo)) * LOG2_E

    def chunk_rows(ci):
        return pl.ds(ci * c if isinstance(ci, int) else pl.multiple_of(ci * c, c), c)

    slabs = [slice(hh * HG_DK, (hh + 1) * HG_DK) for hh in range(HG_HEADS_PER_STEP)]

    def finish(ci, hh, scores, v):
        o_ref[chunk_rows(ci), slabs[hh]] = _dot(scores, v)
        vt_ref[hh, ci] = v.astype(F32).T.astype(BF16)

    first = chunk_rows(0)
    bf_ref[first, :] = jnp.concatenate([cumsum(tri_lo, gf_ref[first, sl]) for sl in slabs], axis=1)
    bb_ref[first, :] = jnp.concatenate([cumsum(tri_up, gb_ref[first, sl]) for sl in slabs], axis=1)
    s_ref[first, :] = jnp.zeros((c, s_ref.shape[1]), BF16)

    def intra(ci, carry):
        prev, nxt = jnp.maximum(ci - 1, 0), jnp.minimum(ci + 1, n_chunks - 1)
        rows, rows_p, rows_n = chunk_rows(ci), chunk_rows(prev), chunk_rows(nxt)
        loaded = []
        for sl in slabs:
            loaded.append((bf_ref[rows, sl], bb_ref[rows, sl], q_ref[rows, sl].astype(F32), kf_ref[rows, sl],
                           kb_ref[rows, sl], s_ref[rows_p, sl], v_ref[rows_p, sl], gf_ref[rows_n, sl],
                           gb_ref[rows_n, sl]))
        for hh, (bf, bb, q, kf, kb, s_prev, v_prev, gf_n, gb_n) in enumerate(loaded):
            finish(prev, hh, s_prev, v_prev)
            bf_ref[rows_n, slabs[hh]] = cumsum(tri_lo, gf_n)
            bb_ref[rows_n, slabs[hh]] = cumsum(tri_up, gb_n)
            scores = _hg_intra(q, kf, bf, masks, place, True) + _hg_intra(q, kb, bb, masks, place, False)
            s_ref[rows, slabs[hh]] = scores.astype(BF16)
        return carry

    lax.fori_loop(0, n_chunks, intra, 0)
    last = chunk_rows(n_chunks - 1)
    for hh, sl in enumerate(slabs):
        finish(n_chunks - 1, hh, s_ref[last, sl], v_ref[last, sl])

    n_ctx = TOKEN_BLOCK // c
    n_lat = n_chunks - n_ctx

    def scan_step(ci, hh, st, k_ref, b_ref, end_row):
        rows, sl = chunk_rows(ci), slabs[hh]
        q = q_ref[rows, sl].astype(F32)
        b = b_ref[rows, sl]
        qs = (q * jnp.exp2(b)).astype(BF16)
        o_ref[rows, sl] += _dot_nt(qs, st.astype(BF16))
        b_end = b[end_row:end_row + 1, :]
        ks = (k_ref[rows, sl] * jnp.exp2(b_end - b)).astype(BF16)
        return st * jnp.exp2(b_end) + _dot(vt_ref[hh, ci], ks)

    def scan(step, carry):
        cf = jnp.where(step < n_ctx, n_lat + step, step - n_ctx)
        cb = n_chunks - 1 - step
        out = []
        for hh in range(HG_HEADS_PER_STEP):
            out.append(scan_step(cf, hh, carry[2 * hh], kf_ref, bf_ref, c - 1))
            out.append(scan_step(cb, hh, carry[2 * hh + 1], kb_ref, bb_ref, 0))
        return tuple(out)

    zero = jnp.zeros((HG_DV, HG_DK), F32)
    lax.fori_loop(0, n_chunks, scan, (zero,) * (2 * HG_HEADS_PER_STEP))


def _hgrn2(q, kf, kb, gf, gb, v):
    bsz, t, _ = q.shape
    n_chunks = t // HG_CHUNK
    hps = HG_HEADS_PER_STEP
    masks, tri, place = _hg_tables()
    head = lambda b, h: (b, 0, h)
    spec = pl.BlockSpec((None, t, hps * HG_DK), head)
    return pl.pallas_call(
        functools.partial(_hg_kernel, n_chunks=n_chunks),
        grid=(bsz, HG_HEADS // hps),
        in_specs=[spec, spec, spec, spec, spec, spec,
                  pl.BlockSpec(masks.shape, lambda b, h: (0, 0, 0)),
                  pl.BlockSpec(tri.shape, lambda b, h: (0, 0, 0)),
                  pl.BlockSpec(place.shape, lambda b, h: (0, 0, 0))],
        out_specs=pl.BlockSpec((None, t, hps * HG_DV), head),
        out_shape=jax.ShapeDtypeStruct((bsz, t, HG_HEADS * HG_DV), F32),
        scratch_shapes=[
            pltpu.VMEM((t, hps * HG_DK), F32),
            pltpu.VMEM((t, hps * HG_DK), F32),
            pltpu.VMEM((hps, n_chunks, HG_DV, HG_CHUNK), BF16),
            pltpu.VMEM((t, hps * HG_CHUNK), BF16),
        ],
        compiler_params=_cparams(("parallel", "parallel")),
    )(q, kf, kb, gf, gb, v, masks, tri, place)


def _split_lanes(aff):
    hi, mid, lo = _split3(aff)
    packed = hi.astype(F32) + pltpu.roll(mid.astype(F32), N_EXPERTS, 1) + pltpu.roll(lo.astype(F32), 2 * N_EXPERTS, 1)
    return packed.astype(BF16)


def _readout_kernel(o_ref, gt_ref, x_ref, mods_ref, wout_ref, gn_ref, nw_ref, wra_ref, wrb_ref,
                    xo_ref, hb_ref, aff_ref, asplit_ref, *, n_heads, dv):
    gn = gn_ref[...]
    n_sub = x_ref.shape[0] // TOKEN_BLOCK
    ctx_row = mods_ref.shape[0] - 1
    on_last_block = pl.program_id(1) == pl.num_programs(1) - 1
    for s in range(n_sub):
        rows = slice(s * TOKEN_BLOCK, (s + 1) * TOKEN_BLOCK)
        mod_row = jnp.where(on_last_block, ctx_row, pl.program_id(0)) if s == n_sub - 1 else pl.program_id(0)
        mod = mods_ref[mod_row]
        o = o_ref[rows, :]
        parts = []
        for h in range(n_heads):
            oh = o[:, h * dv:(h + 1) * dv]
            ms = jnp.mean(oh * oh, axis=-1, keepdims=True)
            parts.append(oh * lax.rsqrt(ms + EPS) * gn)
        z = _silu(gt_ref[rows, :]) * jnp.concatenate(parts, axis=1)
        x = x_ref[rows, :] + mod[2:3, :] * _dot(z.astype(BF16), wout_ref[...])
        xo_ref[rows, :] = x
        h2 = _rms_mod(x, nw_ref[...], mod[3:4, :], mod[4:5, :])
        h_hi = h2.astype(BF16)
        hb_ref[rows, :] = h_hi
        h_lo = (h2 - h_hi.astype(F32)).astype(BF16)
        both = _dot(h_hi, wra_ref[...])
        logits = both[:, :LANES] + both[:, LANES:] + _dot(h_lo, wrb_ref[...])
        lane = lax.broadcasted_iota(jnp.int32, logits.shape, 1)
        logits = jnp.where(lane < N_EXPERTS, logits, -1e30)
        e = jnp.exp(logits - jnp.max(logits, axis=-1, keepdims=True))
        aff = e / jnp.sum(e, axis=-1, keepdims=True)
        aff_ref[rows, :] = aff
        asplit_ref[rows, :] = _split_lanes(aff)


def _readout(o, gt, x, mods, w_out, g_norm, nw_ffn, w_router, n_heads, dv):
    w_router_hi = w_router.astype(BF16)
    w_router_lo = (w_router - w_router_hi.astype(F32)).astype(BF16)
    w_router_hi_lo = jnp.concatenate([w_router_hi, w_router_lo], axis=1)
    bsz, t, d = x.shape
    hv = n_heads * dv
    tm = READOUT_SUBBLOCKS * TOKEN_BLOCK
    assert t % tm == 0
    tok = lambda b, i: (b, i, 0)
    full2 = lambda b, i: (0, 0)
    return pl.pallas_call(
        functools.partial(_readout_kernel, n_heads=n_heads, dv=dv),
        grid=(bsz, t // tm),
        in_specs=[
            pl.BlockSpec((None, tm, hv), tok),
            pl.BlockSpec((None, tm, hv), tok),
            pl.BlockSpec((None, tm, d), tok),
            pl.BlockSpec(mods.shape, lambda b, i: (0, 0, 0)),
            pl.BlockSpec((hv, d), full2),
            pl.BlockSpec((1, dv), full2),
            pl.BlockSpec((1, d), full2),
            pl.BlockSpec((d, 2 * LANES), full2),
            pl.BlockSpec((d, LANES), full2),
        ],
        out_specs=[
            pl.BlockSpec((None, tm, d), tok),
            pl.BlockSpec((None, tm, d), tok),
            pl.BlockSpec((None, tm, LANES), tok),
            pl.BlockSpec((None, tm, LANES), tok),
        ],
        out_shape=[
            jax.ShapeDtypeStruct((bsz, t, d), F32),
            jax.ShapeDtypeStruct((bsz, t, d), BF16),
            jax.ShapeDtypeStruct((bsz, t, LANES), F32),
            jax.ShapeDtypeStruct((bsz, t, LANES), BF16),
        ],
        compiler_params=_cparams(("parallel", "parallel")),
    )(o, gt, x, mods, w_out, g_norm, nw_ffn, w_router_hi_lo, w_router_hi)


def _excl_cumsum_rows(x01, tri_strict):
    blk = tri_strict.shape[0]
    carry = jnp.zeros((1, x01.shape[1]), F32)
    out = []
    for r0 in range(0, x01.shape[0], blk):
        xb = x01[r0:r0 + blk, :]
        out.append(_dot(tri_strict, xb.astype(BF16)) + carry)
        carry = carry + jnp.sum(xb, axis=0, keepdims=True)
    return out[0] if len(out) == 1 else jnp.concatenate(out, axis=0)


def _route_kernel(aff_ref, packed_ref, tri_ref, pos_ref, *, segments):
    tri_strict = tri_ref[...]
    per_row = LANES // N_EXPERTS
    for r0, n, cap in segments:
        if cap == 0:
            pos_ref[r0:r0 + n, :] = jnp.full((n, LANES), -1.0, F32)
            continue
        aff = aff_ref[r0:r0 + n, :]
        packed = packed_ref[r0 // per_row:(r0 + n) // per_row, :]

        def as_float(pattern):
            return lax.bitcast_convert_type(pattern, F32)

        def refine(thr, low_bit, n_bits):
            counts = []
            for digit in range(1, 1 << n_bits):
                cand = thr | jnp.left_shift(jnp.int32(digit), low_bit)
                counts.append(jnp.sum(jnp.where(packed >= as_float(cand), 1.0, 0.0), axis=0, keepdims=True))
            cnt = jnp.concatenate(counts + [jnp.zeros((1, LANES), F32)], axis=0)
            shift = LANES // 2
            while shift >= N_EXPERTS:
                cnt = cnt + pltpu.roll(cnt, shift, 1)
                shift //= 2
            digit = jnp.sum(jnp.where(cnt >= cap, 1.0, 0.0), axis=0, keepdims=True).astype(jnp.int32)
            return thr | jnp.left_shift(digit, low_bit)

        thr = refine(jnp.zeros((1, LANES), jnp.int32), 28, 3)
        thr = lax.fori_loop(0, 7, lambda i, th: refine(th, 24 - 4 * i, 4), thr)
        beyond = as_float(thr + 1)
        above = jnp.where(aff >= beyond, 1.0, 0.0)
        tied = jnp.where((aff >= as_float(thr)) & (aff < beyond), 1.0, 0.0)
        need = cap - jnp.sum(above, axis=0, keepdims=True)
        tie_rank = _excl_cumsum_rows(tied, tri_strict)
        sel = above + tied * jnp.where(tie_rank < need, 1.0, 0.0)
        slot = _excl_cumsum_rows(sel, tri_strict)
        pos_ref[r0:r0 + n, :] = jnp.where(sel > 0.0, slot, -1.0)


def _route(aff, segments):
    bsz, t, _ = aff.shape
    r = np.arange(TOKEN_BLOCK)
    tri_strict = jnp.asarray((r[None, :] < r[:, None]).astype(np.float32), BF16)
    per_row = LANES // N_EXPERTS
    packed = aff[:, :, :N_EXPERTS].reshape(bsz, t // per_row, LANES)
    return pl.pallas_call(
        functools.partial(_route_kernel, segments=segments),
        grid=(bsz,),
        in_specs=[pl.BlockSpec((None, t, LANES), lambda b: (b, 0, 0)),
                  pl.BlockSpec((None, t // per_row, LANES), lambda b: (b, 0, 0)),
                  pl.BlockSpec(tri_strict.shape, lambda b: (0, 0))],
        out_specs=pl.BlockSpec((None, t, LANES), lambda b: (b, 0, 0)),
        out_shape=jax.ShapeDtypeStruct((bsz, t, LANES), F32),
        compiler_params=_cparams(("parallel",)),
    )(aff, packed, tri_strict)


def _onehot(cond):
    return jnp.where(cond, 1.0, 0.0).astype(BF16)


def _gather_kernel(hb_ref, asplit_ref, pos_ref, xe_ref, gate_ref, *, segments):
    e = pl.program_id(1)
    pos = pos_ref[...]
    out0 = 0
    for r0, n, cap in segments:
        if cap == 0:
            continue
        slot = lax.broadcasted_iota(jnp.int32, (cap, n), 0).astype(F32)
        g = _onehot(pos[:, r0:r0 + n] == slot)
        xe_ref[out0:out0 + cap, :] = _dot(g, hb_ref[r0:r0 + n, :]).astype(BF16)
        pieces = _dot(g, asplit_ref[r0:r0 + n, :])
        lane = lax.broadcasted_iota(jnp.int32, pieces.shape, 1)
        mine = (lane < 3 * N_EXPERTS) & ((lane & (N_EXPERTS - 1)) == e)
        gate = jnp.sum(jnp.where(mine, pieces, 0.0), axis=1, keepdims=True)
        gate_ref[out0:out0 + cap, :] = jnp.broadcast_to(gate, pieces.shape)
        out0 += cap


def _gather(hb, asplit, pos_rows, segments):
    bsz, t, d = hb.shape
    rows = sum(cap for _, _, cap in segments)
    slots = lambda b, e: (e * bsz + b, 0, 0)
    return pl.pallas_call(
        functools.partial(_gather_kernel, segments=segments),
        grid=(bsz, N_EXPERTS),
        in_specs=[pl.BlockSpec((None, t, d), lambda b, e: (b, 0, 0)),
                  pl.BlockSpec((None, t, LANES), lambda b, e: (b, 0, 0)),
                  pl.BlockSpec((None, 1, t), lambda b, e: (b * N_EXPERTS + e, 0, 0))],
        out_specs=[pl.BlockSpec((None, rows, d), slots), pl.BlockSpec((None, rows, LANES), slots)],
        out_shape=[jax.ShapeDtypeStruct((N_EXPERTS * bsz, rows, d), BF16),
                   jax.ShapeDtypeStruct((N_EXPERTS * bsz, rows, LANES), F32)],
        compiler_params=_cparams(("parallel", "parallel")),
    )(hb, asplit, pos_rows)


def _ffn_kernel(xe_ref, gate_ref, wg_ref, wu_ref, wd_ref, y_ref, acc_ref, *, row_chunk):
    f = pl.program_id(1)
    wg = wg_ref[...].astype(BF16)
    wu = wu_ref[...].astype(BF16)
    wd = wd_ref[...].astype(BF16)

    @pl.when(f == 0)
    def _():
        acc_ref[...] = jnp.zeros_like(acc_ref)

    for r0 in range(0, xe_ref.shape[0], row_chunk):
        xe = xe_ref[r0:r0 + row_chunk, :]
        hidden = (_silu(_dot(xe, wg)) * _dot(xe, wu)).astype(BF16)
        acc_ref[r0:r0 + row_chunk, :] += _dot(hidden, wd)

    @pl.when(f == pl.num_programs(1) - 1)
    def _():
        y_ref[...] = (acc_ref[...] * gate_ref[:, 0:1]).astype(y_ref.dtype)


def _ffn(xe, gate, w_gate, w_up, w_down, layer):
    n_rows, d = xe.shape[1], xe.shape[2]
    ff = w_gate.shape[-1]
    row_chunk = n_rows // 2 if n_rows % 2 == 0 and (n_rows // 2) % 16 == 0 else n_rows
    return pl.pallas_call(
        functools.partial(_ffn_kernel, row_chunk=row_chunk),
        grid=(N_EXPERTS, ff // FF_TILE),
        in_specs=[
            pl.BlockSpec((None, n_rows, d), lambda e, f: (e, 0, 0)),
            pl.BlockSpec((None, n_rows, LANES), lambda e, f: (e, 0, 0)),
            pl.BlockSpec((None, None, d, FF_TILE), lambda e, f: (layer, e, 0, f)),
            pl.BlockSpec((None, None, d, FF_TILE), lambda e, f: (layer, e, 0, f)),
            pl.BlockSpec((None, None, FF_TILE, d), lambda e, f: (layer, e, f, 0)),
        ],
        out_specs=pl.BlockSpec((None, n_rows, d), lambda e, f: (e, 0, 0)),
        out_shape=jax.ShapeDtypeStruct((N_EXPERTS, n_rows, d), BF16),
        scratch_shapes=[pltpu.VMEM((n_rows, d), F32)],
        compiler_params=_cparams(("parallel", "arbitrary")),
    )(xe, gate, w_gate, w_up, w_down)


def _scatter_kernel(x_ref, y_ref, pos_ref, mods_ref, nw_ref, *rest, cap, slot0, ctx_mods, final, sub):
    out_ref = rest[-1]
    mod_row = mods_ref.shape[0] - 1 if ctx_mods else pl.program_id(0)
    gate = mods_ref[mod_row][5:6, :]
    d = y_ref.shape[-1]
    y_all = y_ref[:, slot0:slot0 + cap, :].reshape(N_EXPERTS * cap, d)
    slot = lax.broadcasted_iota(jnp.int32, (sub, cap), 1).astype(F32)
    for c0 in range(0, x_ref.shape[0], sub):
        pos = pos_ref[c0:c0 + sub, :]
        onehots = jnp.concatenate([_onehot(pos[:, e:e + 1] == slot) for e in range(N_EXPERTS)], axis=1)
        x = x_ref[c0:c0 + sub, :] + gate * _dot(onehots, y_all)
        if final:
            x = x * lax.rsqrt(jnp.mean(x * x, axis=-1, keepdims=True) + EPS) * nw_ref[...]
        out_ref[c0:c0 + sub, :] = x


def _scatter(x, y, pos, mods, nw_final, r0, n, cap, slot0, tm, out_rows, ctx_mods, final, carry=None):
    bsz, t, d = x.shape
    n_slots = y.shape[2]
    blk0 = r0 // tm
    tok = lambda b, i: (b, blk0 + i, 0)
    in_specs = [
        pl.BlockSpec((None, tm, d), tok),
        pl.BlockSpec((N_EXPERTS, None, n_slots, d), lambda b, i: (0, b, 0, 0)),
        pl.BlockSpec((None, tm, LANES), tok),
        pl.BlockSpec(mods.shape, lambda b, i: (0, 0, 0)),
        pl.BlockSpec((1, d), lambda b, i: (0, 0)),
    ]
    args = [x, y, pos, mods, nw_final]
    aliases = {}
    if carry is not None:
        in_specs.append(pl.BlockSpec(memory_space=pl.ANY))
        args.append(carry)
        aliases = {len(args) - 1: 0}
    return pl.pallas_call(
        functools.partial(_scatter_kernel, cap=cap, slot0=slot0, ctx_mods=ctx_mods, final=final, sub=min(tm, 256)),
        grid=(bsz, n // tm),
        in_specs=in_specs,
        out_specs=pl.BlockSpec((None, tm, d), tok),
        out_shape=jax.ShapeDtypeStruct((bsz, out_rows, d), F32),
        input_output_aliases=aliases,
        compiler_params=_cparams(("parallel", "parallel")),
    )(*args)


def _moe(x, hb, aff, asplit, mods, w_gate, w_up, w_down, layer, nw_final, with_ctx, final):
    bsz, t, d = x.shape
    seq = t - TOKEN_BLOCK
    cap_lat = EC_CAPACITY * seq // N_EXPERTS
    cap_ctx = EC_CAPACITY * TOKEN_BLOCK // N_EXPERTS if with_ctx else 0
    segments = ((0, seq, cap_lat), (seq, TOKEN_BLOCK, cap_ctx))
    pos = _route(aff, segments)
    pos_rows = jnp.swapaxes(pos[:, :, :N_EXPERTS], 1, 2).reshape(bsz * N_EXPERTS, 1, t)
    xe, gate = _gather(hb, asplit, pos_rows, segments)
    rows = xe.shape[1]
    y = _ffn(xe.reshape(N_EXPERTS, bsz * rows, d), gate.reshape(N_EXPERTS, bsz * rows, LANES),
             w_gate, w_up, w_down, layer)
    y = y.reshape(N_EXPERTS, bsz, rows, d)
    out = _scatter(x, y, pos, mods, nw_final, 0, seq, cap_lat, 0, 512, seq if final else t, False, final)
    if with_ctx:
        out = _scatter(x, y, pos, mods, nw_final, seq, TOKEN_BLOCK, cap_ctx, cap_lat, TOKEN_BLOCK, t, True, final,
                       carry=out)
    return out


def _rope_tables(seq, n_ctx):
    half = RET_DK // 2
    t = jnp.arange(seq)
    inv = ROPE_THETA ** (-jnp.arange(0, half, 2, dtype=F32) / half)
    tabs = []
    for pos in ((t // GRID_W).astype(F32), (t % GRID_W).astype(F32)):
        ang = pos[:, None] * inv
        cos, sin = jnp.cos(ang), jnp.sin(ang)
        tabs.append(jnp.concatenate([cos, cos], axis=1))
        tabs.append(jnp.concatenate([-sin, sin], axis=1))
    ident = [jnp.ones, jnp.zeros, jnp.ones, jnp.zeros]
    return [jnp.concatenate([tab, fn((n_ctx, 2 * (half // 2)), F32)], axis=0) for tab, fn in zip(tabs, ident)]


def kernel(x, c, ctx, c_ctx, w_ada, b_ada, norm_mix, norm_ffn, ret_w_in, ret_w_out, hg_w_in, hg_g_norm,
           hg_lower_bounds, hg_w_out, moe_router, moe_w_gate, moe_w_up, moe_w_down, norm_final):
    bsz, seq, d = x.shape
    n_ctx = ctx.shape[1]
    depth = w_ada.shape[0]
    assert n_ctx == TOKEN_BLOCK and seq % TOKEN_BLOCK == 0 and d == RET_HEADS * RET_DK == HG_HEADS * HG_DK

    xs = jnp.concatenate([x, ctx], axis=1)
    cc = jnp.concatenate([c, c_ctx[None, :], jnp.zeros((16 - bsz - 1, d), F32)], axis=0)
    mods_all = _ada(cc, w_ada, b_ada)[:, :bsz + 1].reshape(depth, bsz + 1, 6, d)
    t = seq + n_ctx
    rope = _rope_tables(seq, n_ctx)
    rope_specs = [pl.BlockSpec((t // 2, LANES), lambda b, i, j: (i, 0))] * 4
    w_router = jnp.pad(moe_router, ((0, 0), (0, 0), (0, LANES - N_EXPERTS)))
    ones_dv = jnp.ones((1, RET_DV), F32)

    for layer in range(depth):
        last = layer == depth - 1
        slot = layer // 2
        mods = mods_all[layer]
        h = _prenorm(xs, mods, norm_mix[layer][None, :], seq, 0)
        if layer % 2 == 0:
            assert RET_HEADS * RET_DK == PROJ_COLS
            w = ret_w_in[slot]
            w = jnp.concatenate([w[:, :PROJ_COLS], w[:, PROJ_COLS:2 * PROJ_COLS] * (RET_DK ** -0.5),
                                 w[:, 2 * PROJ_COLS:]], axis=1).astype(BF16)
            q, = _proj(h, w, 0, 1, _ep_rope, [BF16], rope, rope_specs)
            k, = _proj(h, w, 1, 1, _ep_rope, [BF16], rope, rope_specs)
            v, = _proj(h, w, 2, 2, _ep_plain, [BF16])
            gt, = _proj(h, w, 4, 2, _ep_plain, [F32])
            o = _retention(q, k, v)
            w_out, g_norm, n_heads, dv = ret_w_out[slot].astype(BF16), ones_dv, RET_HEADS, RET_DV
        else:
            assert HG_HEADS * HG_DK == PROJ_COLS
            w = hg_w_in[slot].astype(BF16)
            lb_specs = [pl.BlockSpec((depth, PROJ_COLS), lambda b, i, j: (0, 0))]
            ep_gate = functools.partial(_ep_gate, layer=layer)
            q, = _proj(h, w, 0, 1, _ep_silu, [BF16])
            kf, gf = _proj(h, w, 1, 1, ep_gate, [F32, F32], (hg_lower_bounds[:, 0, :],), lb_specs)
            kb, gb = _proj(h, w, 2, 1, ep_gate, [F32, F32], (hg_lower_bounds[:, 1, :],), lb_specs)
            v, = _proj(h, w, 3, 1, _ep_plain, [BF16])
            gt, = _proj(h, w, 4, 1, _ep_plain, [F32])
            o = _hgrn2(q, kf, kb, gf, gb, v)
            w_out, g_norm, n_heads, dv = hg_w_out[slot].astype(BF16), hg_g_norm[slot][None, :], HG_HEADS, HG_DV
        xs, hb, aff, asplit = _readout(o, gt, xs, mods, w_out, g_norm, norm_ffn[layer][None, :], w_router[layer],
                                       n_heads, dv)
        xs = _moe(xs, hb, aff, asplit, mods, moe_w_gate, moe_w_up, moe_w_down, layer, norm_final[None, :],
                  with_ctx=not last, final=last)
    return xs
```

```python
import functools

import numpy as np
import jax
import jax.numpy as jnp
from jax import lax
from jax.experimental import pallas as pl
from jax.experimental.pallas import tpu as pltpu

F32 = jnp.float32
BF16 = jnp.bfloat16
EPS = 1e-6
ROPE_THETA = 10000.0
GRID_W = 64

LANES = 128
TOKEN_BLOCK = 256
RET_HEADS, RET_DK, RET_DV, RET_CHUNK = 4, 256, 512, 256
HG_HEADS, HG_DK, HG_DV, HG_CHUNK, HG_SUB = 8, 128, 128, 128, 8
HG_HEADS_PER_STEP = 2
READOUT_SUBBLOCKS = 3
LOG2_E = 1.4426950408889634
N_EXPERTS = 16
EC_CAPACITY = 2
FF_TILE = 256
PROJ_COLS = 1024
VMEM_LIMIT = 56 * 1024 * 1024

HIGHEST = lax.Precision.HIGHEST


def _cparams(sem):
    return pltpu.CompilerParams(dimension_semantics=sem, vmem_limit_bytes=VMEM_LIMIT)


def _dot(a, b, **kw):
    return jnp.dot(a, b, preferred_element_type=F32, **kw)


def _dot_nt(a, b):
    return lax.dot_general(a, b, (((1,), (1,)), ((), ())), preferred_element_type=F32)


def _silu(x):
    return x * jax.nn.sigmoid(x)


def _rms_mod(x, nw, shift, scale):
    ms = jnp.mean(x * x, axis=-1, keepdims=True)
    h = x * lax.rsqrt(ms + EPS) * nw
    return h * (1.0 + scale) + shift


def _ada_kernel(c_ref, w_ref, b_ref, o_ref):
    o_ref[...] = _dot(_silu(c_ref[...]), w_ref[...], precision=HIGHEST) + b_ref[...]


def _ada(cc, w_ada, b_ada):
    depth, d, d6 = w_ada.shape
    rows = cc.shape[0]
    return pl.pallas_call(
        _ada_kernel,
        grid=(depth, d6 // d),
        in_specs=[
            pl.BlockSpec((rows, d), lambda l, j: (0, 0)),
            pl.BlockSpec((None, d, d), lambda l, j: (l, 0, j)),
            pl.BlockSpec((None, 1, d), lambda l, j: (l, 0, j)),
        ],
        out_specs=pl.BlockSpec((None, rows, d), lambda l, j: (l, 0, j)),
        out_shape=jax.ShapeDtypeStruct((depth, rows, d6), F32),
        compiler_params=_cparams(("parallel", "parallel")),
    )(cc, w_ada, b_ada.reshape(depth, 1, d6))


def _prenorm_kernel(x_ref, ctx_ref, mods_ref, nw_ref, xs_ref, h_ref):
    ctx_row = mods_ref.shape[0] - 1
    nw = nw_ref[...]
    seq = x_ref.shape[0]
    for src_ref, r0, row in ((x_ref, 0, pl.program_id(0)), (ctx_ref, seq, ctx_row)):
        m = mods_ref[row]
        shift, scale = m[0:1, :], m[1:2, :]

        def body(i, carry):
            src = pl.ds(pl.multiple_of(i * TOKEN_BLOCK, TOKEN_BLOCK), TOKEN_BLOCK)
            dst = pl.ds(pl.multiple_of(r0 + i * TOKEN_BLOCK, TOKEN_BLOCK), TOKEN_BLOCK)
            x = src_ref[src, :]
            xs_ref[dst, :] = x
            h_ref[dst, :] = _rms_mod(x, nw, shift, scale).astype(BF16)
            return carry

        lax.fori_loop(0, src_ref.shape[0] // TOKEN_BLOCK, body, 0)


def _prenorm(x, ctx, mods, nw):
    bsz, seq, d = x.shape
    n_ctx = ctx.shape[1]
    t = seq + n_ctx
    return pl.pallas_call(
        _prenorm_kernel,
        grid=(bsz,),
        in_specs=[pl.BlockSpec((None, seq, d), lambda b: (b, 0, 0)),
                  pl.BlockSpec((None, n_ctx, d), lambda b: (b, 0, 0)),
                  pl.BlockSpec(mods.shape, lambda b: (0, 0, 0)),
                  pl.BlockSpec((1, d), lambda b: (0, 0))],
        out_specs=[pl.BlockSpec((None, t, d), lambda b: (b, 0, 0)), pl.BlockSpec((None, t, d), lambda b: (b, 0, 0))],
        out_shape=[jax.ShapeDtypeStruct((bsz, t, d), F32), jax.ShapeDtypeStruct((bsz, t, d), BF16)],
        compiler_params=_cparams(("parallel",)),
    )(x, ctx, mods, nw)


def _proj_kernel(h_ref, w_ref, *rest, epilogue, n_extra):
    extras, outs = rest[:n_extra], rest[n_extra:]
    epilogue(_dot(h_ref[...], w_ref[...]), extras, outs)


def _ep_plain(acc, extras, outs):
    outs[0][...] = acc.astype(outs[0].dtype)


def _ep_silu(acc, extras, outs):
    outs[0][...] = _silu(acc).astype(outs[0].dtype)


def _ep_rope(acc, extras, outs):
    c_row, s_row, c_col, s_col = (r[...] for r in extras)
    for s in range(acc.shape[1] // LANES):
        slab = acc[:, s * LANES:(s + 1) * LANES]
        cos, sin = (c_row, s_row) if s % 2 == 0 else (c_col, s_col)
        rot = slab * cos + pltpu.roll(slab, LANES // 2, 1) * sin
        outs[0][:, s * LANES:(s + 1) * LANES] = rot.astype(outs[0].dtype)


def _ep_gate(acc, extras, outs, *, layer):
    raw = extras[0][...]
    e = jnp.exp(raw - jnp.max(raw, axis=0, keepdims=True))
    sm = e / jnp.sum(e, axis=0, keepdims=True)
    lb = -sm[0:1, :]
    for l in range(layer + 1):
        lb = lb + sm[l:l + 1, :]
    f = lb + (1.0 - lb) * jax.nn.sigmoid(acc)
    outs[0][...] = 1.0 - f
    outs[1][...] = jnp.log(f)


def _proj(h, w, col_block0, ncol, epilogue, out_dtypes, extras=(), extra_specs=()):
    bsz, t, d = h.shape
    tm, tn = t // 2, PROJ_COLS
    in_specs = [
        pl.BlockSpec((None, tm, d), lambda b, i, j: (b, i, 0)),
        pl.BlockSpec((d, tn), lambda b, i, j: (0, col_block0 + j)),
    ] + list(extra_specs)
    out_specs = [pl.BlockSpec((None, tm, tn), lambda b, i, j: (b, i, j)) for _ in out_dtypes]
    out_shape = [jax.ShapeDtypeStruct((bsz, t, tn * ncol), dt) for dt in out_dtypes]
    return pl.pallas_call(
        functools.partial(_proj_kernel, epilogue=epilogue, n_extra=len(extras)),
        grid=(bsz, 2, ncol), in_specs=in_specs, out_specs=out_specs, out_shape=out_shape,
        compiler_params=_cparams(("parallel", "parallel", "parallel")),
    )(h, w, *extras)


def _ret_tables():
    c = RET_CHUNK
    j = np.arange(2 * RET_HEADS, dtype=np.float64)
    lg = np.log1p(-np.exp2(-5.0 - j / 2))
    lg_f, lg_b = lg[0::2], lg[1::2]
    i = np.arange(c, dtype=np.float64)
    diff = i[:, None] - i[None, :]
    mats = np.zeros((RET_HEADS, c, c), np.float64)
    tabs = np.zeros((RET_HEADS, c, LANES), np.float64)
    for h in range(RET_HEADS):
        mats[h] = np.where(diff >= 0, np.exp(diff * lg_f[h]), 0.0) + np.where(diff <= 0, np.exp(-diff * lg_b[h]), 0.0)
        tabs[h, :, 0] = np.exp((i + 1) * lg_f[h])
        tabs[h, :, 1] = np.exp((c - 1 - i) * lg_f[h])
        tabs[h, :, 2] = np.exp((c - i) * lg_b[h])
        tabs[h, :, 3] = np.exp(i * lg_b[h])
        tabs[h, :, 4] = np.exp(c * lg_f[h])
        tabs[h, :, 5] = np.exp(c * lg_b[h])
    return jnp.asarray(mats, F32), jnp.asarray(tabs, F32)


def _ret_kernel(q_ref, k_ref, v_ref, m_ref, tab_ref, out_ref, o_ref, vt_ref, sf_ref, sb_ref, *, n_chunks):
    c = RET_CHUNK
    decay = m_ref[...]
    tab = tab_ref[...]
    qf, kfs, qb, kbs = tab[:, 0:1], tab[:, 1:2], tab[:, 2:3], tab[:, 3:4]
    end_f, end_b = tab[0:1, 4:5], tab[0:1, 5:6]

    def rows(ci):
        return slice(ci * c, (ci + 1) * c)

    for ci in range(n_chunks):
        q, k, v = q_ref[rows(ci), :], k_ref[rows(ci), :], v_ref[rows(ci), :]
        scores = (_dot_nt(q, k) * decay).astype(BF16)
        o_ref[rows(ci), :] = _dot(scores, v)
        vt_ref[ci] = v.astype(F32).T.astype(BF16)

    ctx = n_chunks - 1
    order_f = [ctx] + list(range(ctx))
    order_b = [ctx] + list(range(ctx - 1, -1, -1))
    sf_ref[...] = jnp.zeros_like(sf_ref)
    sb_ref[...] = jnp.zeros_like(sb_ref)
    for step in range(n_chunks):
        for order, s_ref, qd, kd, end in ((order_f, sf_ref, qf, kfs, end_f), (order_b, sb_ref, qb, kbs, end_b)):
            ci = order[step]
            if step > 0:
                o_ref[rows(ci), :] += _dot_nt(q_ref[rows(ci), :], s_ref[...].astype(BF16)) * qd
            if step < n_chunks - 1:
                ks = (k_ref[rows(ci), :].astype(F32) * kd).astype(BF16)
                upd = _dot(vt_ref[ci], ks)
                s_ref[...] = upd if step == 0 else s_ref[...] * end + upd
    for ci in range(n_chunks):
        out_ref[rows(ci), :] = o_ref[rows(ci), :].astype(out_ref.dtype)


def _retention(q, k, v):
    bsz, t, _ = q.shape
    n_chunks = t // RET_CHUNK
    mats, tabs = _ret_tables()
    return pl.pallas_call(
        functools.partial(_ret_kernel, n_chunks=n_chunks),
        grid=(bsz, RET_HEADS),
        in_specs=[
            pl.BlockSpec((None, t, RET_DK), lambda b, h: (b, 0, h)),
            pl.BlockSpec((None, t, RET_DK), lambda b, h: (b, 0, h)),
            pl.BlockSpec((None, t, RET_DV), lambda b, h: (b, 0, h)),
            pl.BlockSpec((None, RET_CHUNK, RET_CHUNK), lambda b, h: (h, 0, 0)),
            pl.BlockSpec((None, RET_CHUNK, LANES), lambda b, h: (h, 0, 0)),
        ],
        out_specs=pl.BlockSpec((None, t, RET_DV), lambda b, h: (b, 0, h)),
        out_shape=jax.ShapeDtypeStruct((bsz, t, RET_HEADS * RET_DV), BF16),
        scratch_shapes=[
            pltpu.VMEM((t, RET_DV), F32),
            pltpu.VMEM((n_chunks, RET_DV, RET_CHUNK), BF16),
            pltpu.VMEM((RET_DV, RET_DK), F32),
            pltpu.VMEM((RET_DV, RET_DK), F32),
        ],
        compiler_params=_cparams(("parallel", "parallel")),
    )(q, k, v, mats, tabs)


def _hg_tables():
    c, sub = HG_CHUNK, HG_SUB
    r = np.arange(c)[:, None]
    col = np.arange(c)[None, :]
    masks = []
    for lower in (True, False):
        s = c // 2
        while s >= sub:
            same = (r // (2 * s)) == (col // (2 * s))
            rh, ch = (r // s) % 2, (col // s) % 2
            masks.append(same & ((rh == 1) & (ch == 0) if lower else (rh == 0) & (ch == 1)))
            s //= 2
    blockdiag = (r // sub) == (col // sub)
    masks.append(blockdiag & (col <= r))
    masks.append(blockdiag & (col >= r))
    tri = np.stack([col <= r, col >= r]).astype(np.float32)
    place = np.stack([np.broadcast_to((np.arange(c) % sub) == j, (c, c)) for j in range(sub)]).astype(np.float32)
    return jnp.asarray(np.stack(masks).astype(np.float32)), jnp.asarray(tri, BF16), jnp.asarray(place, BF16)


def _split3(g):
    hi = g.astype(BF16)
    r1 = g - hi.astype(F32)
    mid = r1.astype(BF16)
    lo = (r1 - mid.astype(F32)).astype(BF16)
    return hi, mid, lo


def _hg_intra(q, k, b, masks, place, lower):
    c, sub = HG_CHUNK, HG_SUB
    n_levels = int(np.log2(c // sub))
    total = jnp.zeros((c, c), F32)
    s = c // 2
    for level in range(n_levels):
        pieces = []
        for m in range(0, c, 2 * s):
            r = m + s - 1 if lower else m + s
            pieces.append(jnp.broadcast_to(b[r:r + 1, :], (2 * s, b.shape[1])))
        ref = pieces[0] if len(pieces) == 1 else jnp.concatenate(pieces, axis=0)
        diff = b - ref
        w = jnp.exp2(jnp.minimum(diff, -diff))
        total = total + _dot_nt((q * w).astype(BF16), (k * w).astype(BF16)) * masks[(0 if lower else n_levels) + level]
        s //= 2
    shape3 = (c // sub, sub, q.shape[1])
    q3, b3 = q.reshape(shape3), b.reshape(shape3)
    a3 = (jnp.log2(jnp.maximum(k, 0.0)) - b).reshape(shape3)
    acc = jnp.zeros((c, c), F32)
    for jj in range(sub):
        aj = jnp.broadcast_to(a3[:, jj:jj + 1, :], shape3)
        p = q3 * jnp.exp2(jnp.minimum(b3 + aj, 0.0))
        acc = acc + _dot(p.reshape(c, q.shape[1]).astype(BF16), place[jj])
    return total + acc * masks[2 * n_levels + (0 if lower else 1)]


def _hg_kernel(q_ref, kf_ref, kb_ref, gf_ref, gb_ref, v_ref, mask_ref, tri_ref, place_ref, out_ref,
               o_ref, bf_ref, bb_ref, vt_ref, s_ref, *, n_chunks):
    c = HG_CHUNK
    masks = [mask_ref[i] for i in range(mask_ref.shape[0])]
    place = [place_ref[i] for i in range(place_ref.shape[0])]
    tri_lo, tri_up = tri_ref[0], tri_ref[1]

    def cumsum(tri, g):
        hi, mid, lo = _split3(g)
        return (_dot(tri, hi) + _dot(tri, mid) + _dot(tri, lo)) * LOG2_E

    def chunk_rows(ci):
        return pl.ds(ci * c if isinstance(ci, int) else pl.multiple_of(ci * c, c), c)

    slabs = [slice(hh * HG_DK, (hh + 1) * HG_DK) for hh in range(HG_HEADS_PER_STEP)]

    def finish(ci, hh, scores, v):
        o_ref[chunk_rows(ci), slabs[hh]] = _dot(scores, v)
        vt_ref[hh, ci] = v.astype(F32).T.astype(BF16)

    first = chunk_rows(0)
    bf_ref[first, :] = jnp.concatenate([cumsum(tri_lo, gf_ref[first, sl]) for sl in slabs], axis=1)
    bb_ref[first, :] = jnp.concatenate([cumsum(tri_up, gb_ref[first, sl]) for sl in slabs], axis=1)
    s_ref[first, :] = jnp.zeros((c, s_ref.shape[1]), BF16)

    def intra(ci, carry):
        prev, nxt = jnp.maximum(ci - 1, 0), jnp.minimum(ci + 1, n_chunks - 1)
        rows, rows_p, rows_n = chunk_rows(ci), chunk_rows(prev), chunk_rows(nxt)
        loaded = []
        for sl in slabs:
            loaded.append((bf_ref[rows, sl], bb_ref[rows, sl], q_ref[rows, sl].astype(F32), kf_ref[rows, sl],
                           kb_ref[rows, sl], s_ref[rows_p, sl], v_ref[rows_p, sl], gf_ref[rows_n, sl],
                           gb_ref[rows_n, sl]))
        for hh, (bf, bb, q, kf, kb, s_prev, v_prev, gf_n, gb_n) in enumerate(loaded):
            finish(prev, hh, s_prev, v_prev)
            bf_ref[rows_n, slabs[hh]] = cumsum(tri_lo, gf_n)
            bb_ref[rows_n, slabs[hh]] = cumsum(tri_up, gb_n)
            scores = _hg_intra(q, kf, bf, masks, place, True) + _hg_intra(q, kb, bb, masks, place, False)
            s_ref[rows, slabs[hh]] = scores.astype(BF16)
        return carry

    lax.fori_loop(0, n_chunks, intra, 0)
    last = chunk_rows(n_chunks - 1)
    for hh, sl in enumerate(slabs):
        finish(n_chunks - 1, hh, s_ref[last, sl], v_ref[last, sl])

    n_ctx = TOKEN_BLOCK // c
    n_lat = n_chunks - n_ctx

    def scan_step(ci, hh, st, k_ref, b_ref, end_row):
        rows, sl = chunk_rows(ci), slabs[hh]
        q = q_ref[rows, sl].astype(F32)
        b = b_ref[rows, sl]
        qs = (q * jnp.exp2(b)).astype(BF16)
        o_ref[rows, sl] += _dot_nt(qs, st.astype(BF16))
        b_end = b[end_row:end_row + 1, :]
        ks = (k_ref[rows, sl] * jnp.exp2(b_end - b)).astype(BF16)
        return st * jnp.exp2(b_end) + _dot(vt_ref[hh, ci], ks)

    def scan(step, carry):
        cf = jnp.where(step < n_ctx, n_lat + step, step - n_ctx)
        cb = n_chunks - 1 - step
        out = []
        for hh in range(HG_HEADS_PER_STEP):
            out.append(scan_step(cf, hh, carry[2 * hh], kf_ref, bf_ref, c - 1))
            out.append(scan_step(cb, hh, carry[2 * hh + 1], kb_ref, bb_ref, 0))
        return tuple(out)

    zero = jnp.zeros((HG_DV, HG_DK), F32)
    lax.fori_loop(0, n_chunks, scan, (zero,) * (2 * HG_HEADS_PER_STEP))

    def emit(ci, carry):
        out_ref[chunk_rows(ci), :] = o_ref[chunk_rows(ci), :].astype(out_ref.dtype)
        return carry

    lax.fori_loop(0, n_chunks, emit, 0)


def _hgrn2(q, kf, kb, gf, gb, v):
    bsz, t, _ = q.shape
    n_chunks = t // HG_CHUNK
    hps = HG_HEADS_PER_STEP
    masks, tri, place = _hg_tables()
    head = lambda b, h: (b, 0, h)
    spec = pl.BlockSpec((None, t, hps * HG_DK), head)
    return pl.pallas_call(
        functools.partial(_hg_kernel, n_chunks=n_chunks),
        grid=(bsz, HG_HEADS // hps),
        in_specs=[spec, spec, spec, spec, spec, spec,
                  pl.BlockSpec(masks.shape, lambda b, h: (0, 0, 0)),
                  pl.BlockSpec(tri.shape, lambda b, h: (0, 0, 0)),
                  pl.BlockSpec(place.shape, lambda b, h: (0, 0, 0))],
        out_specs=pl.BlockSpec((None, t, hps * HG_DV), head),
        out_shape=jax.ShapeDtypeStruct((bsz, t, HG_HEADS * HG_DV), BF16),
        scratch_shapes=[
            pltpu.VMEM((t, hps * HG_DV), F32),
            pltpu.VMEM((t, hps * HG_DK), F32),
            pltpu.VMEM((t, hps * HG_DK), F32),
            pltpu.VMEM((hps, n_chunks, HG_DV, HG_CHUNK), BF16),
            pltpu.VMEM((t, hps * HG_CHUNK), BF16),
        ],
        compiler_params=_cparams(("parallel", "parallel")),
    )(q, kf, kb, gf, gb, v, masks, tri, place)


def _split_lanes(aff):
    hi, mid, lo = _split3(aff)
    packed = hi.astype(F32) + pltpu.roll(mid.astype(F32), N_EXPERTS, 1) + pltpu.roll(lo.astype(F32), 2 * N_EXPERTS, 1)
    return packed.astype(BF16)


def _readout_kernel(o_ref, zg_ref, x_ref, mods_ref, wout_ref, gn_ref, nw_ref, wra_ref, wrb_ref,
                    xo_ref, hb_ref, aff_ref, asplit_ref, *, n_heads, dv, affine):
    gn = gn_ref[...] if affine else None
    n_sub = x_ref.shape[0] // TOKEN_BLOCK
    ctx_row = mods_ref.shape[0] - 1
    on_last_block = pl.program_id(1) == pl.num_programs(1) - 1
    for s in range(n_sub):
        rows = slice(s * TOKEN_BLOCK, (s + 1) * TOKEN_BLOCK)
        mod_row = jnp.where(on_last_block, ctx_row, pl.program_id(0)) if s == n_sub - 1 else pl.program_id(0)
        mod = mods_ref[mod_row]
        o = o_ref[rows, :].astype(F32)
        parts = []
        for h in range(n_heads):
            oh = o[:, h * dv:(h + 1) * dv]
            ms = jnp.mean(oh * oh, axis=-1, keepdims=True)
            on = oh * lax.rsqrt(ms + EPS)
            parts.append(on if gn is None else on * gn)
        z = zg_ref[rows, :].astype(F32) * jnp.concatenate(parts, axis=1)
        x = x_ref[rows, :] + mod[2:3, :] * _dot(z.astype(BF16), wout_ref[...])
        xo_ref[rows, :] = x
        h2 = _rms_mod(x, nw_ref[...], mod[3:4, :], mod[4:5, :])
        h_hi = h2.astype(BF16)
        hb_ref[rows, :] = h_hi
        h_lo = (h2 - h_hi.astype(F32)).astype(BF16)
        both = _dot(h_hi, wra_ref[...])
        logits = both[:, :LANES] + both[:, LANES:] + _dot(h_lo, wrb_ref[...])
        lane = lax.broadcasted_iota(jnp.int32, logits.shape, 1)
        logits = jnp.where(lane < N_EXPERTS, logits, -1e30)
        e = jnp.exp(logits - jnp.max(logits, axis=-1, keepdims=True))
        aff = e / jnp.sum(e, axis=-1, keepdims=True)
        aff_ref[rows, :] = aff
        asplit_ref[rows, :] = _split_lanes(aff)


def _readout(o, zg, x, mods, w_out, g_norm, nw_ffn, w_router, n_heads, dv):
    affine = g_norm is not None
    if not affine:
        g_norm = jnp.ones((1, dv), F32)
    w_router_hi = w_router.astype(BF16)
    w_router_lo = (w_router - w_router_hi.astype(F32)).astype(BF16)
    w_router_hi_lo = jnp.concatenate([w_router_hi, w_router_lo], axis=1)
    bsz, t, d = x.shape
    hv = n_heads * dv
    tm = READOUT_SUBBLOCKS * TOKEN_BLOCK
    assert t % tm == 0
    tok = lambda b, i: (b, i, 0)
    full2 = lambda b, i: (0, 0)
    return pl.pallas_call(
        functools.partial(_readout_kernel, n_heads=n_heads, dv=dv, affine=affine),
        grid=(bsz, t // tm),
        in_specs=[
            pl.BlockSpec((None, tm, hv), tok),
            pl.BlockSpec((None, tm, hv), tok),
            pl.BlockSpec((None, tm, d), tok),
            pl.BlockSpec(mods.shape, lambda b, i: (0, 0, 0)),
            pl.BlockSpec((hv, d), full2),
            pl.BlockSpec((1, dv), full2),
            pl.BlockSpec((1, d), full2),
            pl.BlockSpec((d, 2 * LANES), full2),
            pl.BlockSpec((d, LANES), full2),
        ],
        out_specs=[
            pl.BlockSpec((None, tm, d), tok),
            pl.BlockSpec((None, tm, d), tok),
            pl.BlockSpec((None, tm, LANES), tok),
            pl.BlockSpec((None, tm, LANES), tok),
        ],
        out_shape=[
            jax.ShapeDtypeStruct((bsz, t, d), F32),
            jax.ShapeDtypeStruct((bsz, t, d), BF16),
            jax.ShapeDtypeStruct((bsz, t, LANES), F32),
            jax.ShapeDtypeStruct((bsz, t, LANES), BF16),
        ],
        compiler_params=_cparams(("parallel", "parallel")),
    )(o, zg, x, mods, w_out, g_norm, nw_ffn, w_router_hi_lo, w_router_hi)


def _excl_cumsum_rows(x01, tri_strict):
    blk = tri_strict.shape[0]
    carry = jnp.zeros((1, x01.shape[1]), F32)
    out = []
    for r0 in range(0, x01.shape[0], blk):
        xb = x01[r0:r0 + blk, :]
        out.append(_dot(tri_strict, xb.astype(BF16)) + carry)
        carry = carry + jnp.sum(xb, axis=0, keepdims=True)
    return out[0] if len(out) == 1 else jnp.concatenate(out, axis=0)


def _route_kernel(aff_ref, packed_ref, tri_ref, pos_ref, *, segments):
    tri_strict = tri_ref[...]
    per_row = LANES // N_EXPERTS
    for r0, n, cap in segments:
        if cap == 0:
            pos_ref[r0:r0 + n, :] = jnp.full((n, LANES), -1.0, F32)
            continue
        aff = aff_ref[r0:r0 + n, :]
        packed = packed_ref[r0 // per_row:(r0 + n) // per_row, :]

        def as_float(pattern):
            return lax.bitcast_convert_type(pattern, F32)

        def refine(thr, low_bit, n_bits):
            counts = []
            for digit in range(1, 1 << n_bits):
                cand = thr | jnp.left_shift(jnp.int32(digit), low_bit)
                counts.append(jnp.sum(jnp.where(packed >= as_float(cand), 1.0, 0.0), axis=0, keepdims=True))
            cnt = jnp.concatenate(counts + [jnp.zeros((1, LANES), F32)], axis=0)
            shift = LANES // 2
            while shift >= N_EXPERTS:
                cnt = cnt + pltpu.roll(cnt, shift, 1)
                shift //= 2
            digit = jnp.sum(jnp.where(cnt >= cap, 1.0, 0.0), axis=0, keepdims=True).astype(jnp.int32)
            return thr | jnp.left_shift(digit, low_bit)

        thr = refine(jnp.zeros((1, LANES), jnp.int32), 28, 3)
        thr = lax.fori_loop(0, 7, lambda i, th: refine(th, 24 - 4 * i, 4), thr)
        beyond = as_float(thr + 1)
        above = jnp.where(aff >= beyond, 1.0, 0.0)
        tied = jnp.where((aff >= as_float(thr)) & (aff < beyond), 1.0, 0.0)
        need = cap - jnp.sum(above, axis=0, keepdims=True)
        tie_rank = _excl_cumsum_rows(tied, tri_strict)
        sel = above + tied * jnp.where(tie_rank < need, 1.0, 0.0)
        slot = _excl_cumsum_rows(sel, tri_strict)
        pos_ref[r0:r0 + n, :] = jnp.where(sel > 0.0, slot, -1.0)


def _route(aff, segments):
    bsz, t, _ = aff.shape
    r = np.arange(TOKEN_BLOCK)
    tri_strict = jnp.asarray((r[None, :] < r[:, None]).astype(np.float32), BF16)
    per_row = LANES // N_EXPERTS
    packed = aff[:, :, :N_EXPERTS].reshape(bsz, t // per_row, LANES)
    return pl.pallas_call(
        functools.partial(_route_kernel, segments=segments),
        grid=(bsz,),
        in_specs=[pl.BlockSpec((None, t, LANES), lambda b: (b, 0, 0)),
                  pl.BlockSpec((None, t // per_row, LANES), lambda b: (b, 0, 0)),
                  pl.BlockSpec(tri_strict.shape, lambda b: (0, 0))],
        out_specs=pl.BlockSpec((None, t, LANES), lambda b: (b, 0, 0)),
        out_shape=jax.ShapeDtypeStruct((bsz, t, LANES), F32),
        compiler_params=_cparams(("parallel",)),
    )(aff, packed, tri_strict)


def _onehot(cond):
    return jnp.where(cond, 1.0, 0.0).astype(BF16)


def _gather_kernel(hb_ref, asplit_ref, pos_ref, xe_ref, gate_ref, *, segments):
    e = pl.program_id(1)
    pos = pos_ref[...]
    out0 = 0
    for r0, n, cap in segments:
        if cap == 0:
            continue
        slot = lax.broadcasted_iota(jnp.int32, (cap, n), 0).astype(F32)
        g = _onehot(pos[:, r0:r0 + n] == slot)
        xe_ref[out0:out0 + cap, :] = _dot(g, hb_ref[r0:r0 + n, :]).astype(BF16)
        pieces = _dot(g, asplit_ref[r0:r0 + n, :])
        lane = lax.broadcasted_iota(jnp.int32, pieces.shape, 1)
        mine = (lane < 3 * N_EXPERTS) & ((lane & (N_EXPERTS - 1)) == e)
        gate = jnp.sum(jnp.where(mine, pieces, 0.0), axis=1, keepdims=True)
        gate_ref[out0:out0 + cap, :] = jnp.broadcast_to(gate, pieces.shape)
        out0 += cap


def _gather(hb, asplit, pos_rows, segments):
    bsz, t, d = hb.shape
    rows = sum(cap for _, _, cap in segments)
    slots = lambda b, e: (e * bsz + b, 0, 0)
    return pl.pallas_call(
        functools.partial(_gather_kernel, segments=segments),
        grid=(bsz, N_EXPERTS),
        in_specs=[pl.BlockSpec((None, t, d), lambda b, e: (b, 0, 0)),
                  pl.BlockSpec((None, t, LANES), lambda b, e: (b, 0, 0)),
                  pl.BlockSpec((None, 1, t), lambda b, e: (b * N_EXPERTS + e, 0, 0))],
        out_specs=[pl.BlockSpec((None, rows, d), slots), pl.BlockSpec((None, rows, LANES), slots)],
        out_shape=[jax.ShapeDtypeStruct((N_EXPERTS * bsz, rows, d), BF16),
                   jax.ShapeDtypeStruct((N_EXPERTS * bsz, rows, LANES), F32)],
        compiler_params=_cparams(("parallel", "parallel")),
    )(hb, asplit, pos_rows)


def _ffn_kernel(xe_ref, gate_ref, wg_ref, wu_ref, wd_ref, y_ref, acc_ref, *, row_chunk):
    f = pl.program_id(1)
    last = pl.num_programs(1) - 1

    def step(mode):
        wg = wg_ref[...].astype(BF16)
        wu = wu_ref[...].astype(BF16)
        wd = wd_ref[...].astype(BF16)
        for r0 in range(0, xe_ref.shape[0], row_chunk):
            rows = slice(r0, r0 + row_chunk)
            xe = xe_ref[rows, :]
            hidden = (_silu(_dot(xe, wg)) * _dot(xe, wu)).astype(BF16)
            part = _dot(hidden, wd)
            if mode == "first":
                acc_ref[rows, :] = part
            elif mode == "middle":
                acc_ref[rows, :] += part
            else:
                y_ref[rows, :] = ((acc_ref[rows, :] + part) * gate_ref[rows, 0:1]).astype(y_ref.dtype)

    pl.when(f == 0)(lambda: step("first"))
    pl.when((f > 0) & (f < last))(lambda: step("middle"))
    pl.when(f == last)(lambda: step("last"))


def _ffn(xe, gate, w_gate, w_up, w_down, layer):
    n_rows, d = xe.shape[1], xe.shape[2]
    ff = w_gate.shape[-1]
    row_chunk = n_rows // 2 if n_rows % 2 == 0 and (n_rows // 2) % 16 == 0 else n_rows
    return pl.pallas_call(
        functools.partial(_ffn_kernel, row_chunk=row_chunk),
        grid=(N_EXPERTS, ff // FF_TILE),
        in_specs=[
            pl.BlockSpec((None, n_rows, d), lambda e, f: (e, 0, 0)),
            pl.BlockSpec((None, n_rows, LANES), lambda e, f: (e, 0, 0)),
            pl.BlockSpec((None, None, d, FF_TILE), lambda e, f: (layer, e, 0, f)),
            pl.BlockSpec((None, None, d, FF_TILE), lambda e, f: (layer, e, 0, f)),
            pl.BlockSpec((None, None, FF_TILE, d), lambda e, f: (layer, e, f, 0)),
        ],
        out_specs=pl.BlockSpec((None, n_rows, d), lambda e, f: (e, 0, 0)),
        out_shape=jax.ShapeDtypeStruct((N_EXPERTS, n_rows, d), BF16),
        scratch_shapes=[pltpu.VMEM((n_rows, d), F32)],
        compiler_params=_cparams(("parallel", "arbitrary")),
    )(xe, gate, w_gate, w_up, w_down)


def _scatter_kernel(x_ref, y_ref, pos_ref, mods_ref, nw_ref, *rest, segments, final, with_next):
    if with_next:
        next_mods_ref, next_nw_ref, out_ref, h_ref = rest
    else:
        out_ref, = rest
    d = y_ref.shape[-1]
    ctx_row = mods_ref.shape[0] - 1
    n_lat_blocks = segments[0][1] // TOKEN_BLOCK

    def block(seg_id):
        cap = segments[seg_id][2]
        slot0 = sum(seg[2] for seg in segments[:seg_id])
        mod_row = ctx_row if seg_id == 1 else pl.program_id(0)
        gate = mods_ref[mod_row][5:6, :]
        y_all = y_ref[:, slot0:slot0 + cap, :].reshape(N_EXPERTS * cap, d)
        slot = lax.broadcasted_iota(jnp.int32, (TOKEN_BLOCK, cap), 1).astype(F32)
        pos = pos_ref[...]
        onehots = jnp.concatenate([_onehot(pos[:, e:e + 1] == slot) for e in range(N_EXPERTS)], axis=1)
        x = x_ref[...] + gate * _dot(onehots, y_all)
        if final:
            x = x * lax.rsqrt(jnp.mean(x * x, axis=-1, keepdims=True) + EPS) * nw_ref[...]
        out_ref[...] = x
        if with_next:
            m = next_mods_ref[mod_row]
            h_ref[...] = _rms_mod(x, next_nw_ref[...], m[0:1, :], m[1:2, :]).astype(BF16)

    pl.when(pl.program_id(1) < n_lat_blocks)(lambda: block(0))
    if segments[1][2] > 0:
        pl.when(pl.program_id(1) >= n_lat_blocks)(lambda: block(1))


def _scatter(x, y, pos, mods, nw_final, segments, final, next_norm=None):
    bsz, t, d = x.shape
    n_slots = y.shape[2]
    out_rows = sum(n for _, n, cap in segments if cap > 0)
    tok = lambda b, i: (b, i, 0)
    const2, const3 = (lambda b, i: (0, 0)), (lambda b, i: (0, 0, 0))
    in_specs = [
        pl.BlockSpec((None, TOKEN_BLOCK, d), tok),
        pl.BlockSpec((N_EXPERTS, None, n_slots, d), lambda b, i: (0, b, 0, 0)),
        pl.BlockSpec((None, TOKEN_BLOCK, LANES), tok),
        pl.BlockSpec(mods.shape, const3),
        pl.BlockSpec((1, d), const2),
    ]
    args = [x, y, pos, mods, nw_final]
    out_specs = [pl.BlockSpec((None, TOKEN_BLOCK, d), tok)]
    out_shape = [jax.ShapeDtypeStruct((bsz, out_rows, d), F32)]
    if next_norm is not None:
        in_specs += [pl.BlockSpec(next_norm[0].shape, const3), pl.BlockSpec((1, d), const2)]
        args += list(next_norm)
        out_specs.append(pl.BlockSpec((None, TOKEN_BLOCK, d), tok))
        out_shape.append(jax.ShapeDtypeStruct((bsz, out_rows, d), BF16))
    return pl.pallas_call(
        functools.partial(_scatter_kernel, segments=segments, final=final, with_next=next_norm is not None),
        grid=(bsz, out_rows // TOKEN_BLOCK),
        in_specs=in_specs, out_specs=out_specs, out_shape=out_shape,
        compiler_params=_cparams(("parallel", "parallel")),
    )(*args)


def _moe(x, hb, aff, asplit, mods, w_gate, w_up, w_down, layer, nw_final, with_ctx, final, next_norm):
    bsz, t, d = x.shape
    seq = t - TOKEN_BLOCK
    cap_lat = EC_CAPACITY * seq // N_EXPERTS
    cap_ctx = EC_CAPACITY * TOKEN_BLOCK // N_EXPERTS if with_ctx else 0
    segments = ((0, seq, cap_lat), (seq, TOKEN_BLOCK, cap_ctx))
    pos = _route(aff, segments)
    pos_rows = jnp.swapaxes(pos[:, :, :N_EXPERTS], 1, 2).reshape(bsz * N_EXPERTS, 1, t)
    xe, gate = _gather(hb, asplit, pos_rows, segments)
    rows = xe.shape[1]
    y = _ffn(xe.reshape(N_EXPERTS, bsz * rows, d), gate.reshape(N_EXPERTS, bsz * rows, LANES),
             w_gate, w_up, w_down, layer)
    y = y.reshape(N_EXPERTS, bsz, rows, d)
    return _scatter(x, y, pos, mods, nw_final, segments, final, next_norm)


def _rope_tables(seq, n_ctx):
    half = RET_DK // 2
    t = jnp.arange(seq)
    inv = ROPE_THETA ** (-jnp.arange(0, half, 2, dtype=F32) / half)
    tabs = []
    for pos in ((t // GRID_W).astype(F32), (t % GRID_W).astype(F32)):
        ang = pos[:, None] * inv
        cos, sin = jnp.cos(ang), jnp.sin(ang)
        tabs.append(jnp.concatenate([cos, cos], axis=1))
        tabs.append(jnp.concatenate([-sin, sin], axis=1))
    ident = [jnp.ones, jnp.zeros, jnp.ones, jnp.zeros]
    return [jnp.concatenate([tab, fn((n_ctx, 2 * (half // 2)), F32)], axis=0) for tab, fn in zip(tabs, ident)]


def kernel(x, c, ctx, c_ctx, w_ada, b_ada, norm_mix, norm_ffn, ret_w_in, ret_w_out, hg_w_in, hg_g_norm,
           hg_lower_bounds, hg_w_out, moe_router, moe_w_gate, moe_w_up, moe_w_down, norm_final):
    bsz, seq, d = x.shape
    n_ctx = ctx.shape[1]
    depth = w_ada.shape[0]
    assert n_ctx == TOKEN_BLOCK and seq % TOKEN_BLOCK == 0 and d == RET_HEADS * RET_DK == HG_HEADS * HG_DK

    cc = jnp.concatenate([c, c_ctx[None, :], jnp.zeros((16 - bsz - 1, d), F32)], axis=0)
    mods_all = _ada(cc, w_ada, b_ada)[:, :bsz + 1].reshape(depth, bsz + 1, 6, d)
    t = seq + n_ctx
    rope = _rope_tables(seq, n_ctx)
    rope_specs = [pl.BlockSpec((t // 2, LANES), lambda b, i, j: (i, 0))] * 4
    w_router = jnp.pad(moe_router, ((0, 0), (0, 0), (0, LANES - N_EXPERTS)))

    xs, h = _prenorm(x, ctx, mods_all[0], norm_mix[0][None, :])
    for layer in range(depth):
        last = layer == depth - 1
        slot = layer // 2
        mods = mods_all[layer]
        if layer % 2 == 0:
            assert RET_HEADS * RET_DK == PROJ_COLS
            w = ret_w_in[slot]
            w = jnp.concatenate([w[:, :PROJ_COLS], w[:, PROJ_COLS:2 * PROJ_COLS] * (RET_DK ** -0.5),
                                 w[:, 2 * PROJ_COLS:]], axis=1).astype(BF16)
            q, = _proj(h, w, 0, 1, _ep_rope, [BF16], rope, rope_specs)
            k, = _proj(h, w, 1, 1, _ep_rope, [BF16], rope, rope_specs)
            v, = _proj(h, w, 2, 2, _ep_plain, [BF16])
            zg, = _proj(h, w, 4, 2, _ep_silu, [BF16])
            o = _retention(q, k, v)
            w_out, g_norm, n_heads, dv = ret_w_out[slot].astype(BF16), None, RET_HEADS, RET_DV
        else:
            assert HG_HEADS * HG_DK == PROJ_COLS
            w = hg_w_in[slot].astype(BF16)
            lb_specs = [pl.BlockSpec((depth, PROJ_COLS), lambda b, i, j: (0, 0))]
            ep_gate = functools.partial(_ep_gate, layer=layer)
            q, = _proj(h, w, 0, 1, _ep_silu, [BF16])
            kf, gf = _proj(h, w, 1, 1, ep_gate, [F32, F32], (hg_lower_bounds[:, 0, :],), lb_specs)
            kb, gb = _proj(h, w, 2, 1, ep_gate, [F32, F32], (hg_lower_bounds[:, 1, :],), lb_specs)
            v, = _proj(h, w, 3, 1, _ep_plain, [BF16])
            zg, = _proj(h, w, 4, 1, _ep_silu, [BF16])
            o = _hgrn2(q, kf, kb, gf, gb, v)
            w_out, g_norm, n_heads, dv = hg_w_out[slot].astype(BF16), hg_g_norm[slot][None, :], HG_HEADS, HG_DV
        xs, hb, aff, asplit = _readout(o, zg, xs, mods, w_out, g_norm, norm_ffn[layer][None, :], w_router[layer],
                                       n_heads, dv)
        next_norm = None if last else (mods_all[layer + 1], norm_mix[layer + 1][None, :])
        res = _moe(xs, hb, aff, asplit, mods, moe_w_gate, moe_w_up, moe_w_down, layer, norm_final[None, :],
                   with_ctx=not last, final=last, next_norm=next_norm)
        xs, h = res if next_norm is not None else (res[0], None)
    return xs
```

```python
import functools

import numpy as np
import jax
import jax.numpy as jnp
from jax import lax
from jax.experimental import pallas as pl
from jax.experimental.pallas import tpu as pltpu

F32 = jnp.float32
BF16 = jnp.bfloat16
EPS = 1e-6
ROPE_THETA = 10000.0
GRID_W = 64

LANES = 128
TOKEN_BLOCK = 256
RET_HEADS, RET_DK, RET_DV, RET_CHUNK = 4, 256, 512, 256
HG_HEADS, HG_DK, HG_DV, HG_CHUNK, HG_SUB = 8, 128, 128, 128, 8
HG_HEADS_PER_STEP = 2
READOUT_SUBBLOCKS = 3
LOG2_E = 1.4426950408889634
N_EXPERTS = 16
EC_CAPACITY = 2
FF_TILE = 256
PROJ_COLS = 1024
VMEM_LIMIT = 56 * 1024 * 1024

HIGHEST = lax.Precision.HIGHEST


def _cparams(sem):
    return pltpu.CompilerParams(dimension_semantics=sem, vmem_limit_bytes=VMEM_LIMIT)


def _dot(a, b, **kw):
    return jnp.dot(a, b, preferred_element_type=F32, **kw)


def _dot_nt(a, b):
    return lax.dot_general(a, b, (((1,), (1,)), ((), ())), preferred_element_type=F32)


def _silu(x):
    return x * jax.nn.sigmoid(x)


def _rms_mod(x, nw, shift, scale):
    ms = jnp.mean(x * x, axis=-1, keepdims=True)
    h = x * lax.rsqrt(ms + EPS) * nw
    return h * (1.0 + scale) + shift


def _ada_kernel(c_ref, w_ref, b_ref, o_ref):
    o_ref[...] = _dot(_silu(c_ref[...]), w_ref[...], precision=HIGHEST) + b_ref[...]


def _ada(cc, w_ada, b_ada):
    depth, d, d6 = w_ada.shape
    rows = cc.shape[0]
    return pl.pallas_call(
        _ada_kernel,
        grid=(depth, d6 // d),
        in_specs=[
            pl.BlockSpec((rows, d), lambda l, j: (0, 0)),
            pl.BlockSpec((None, d, d), lambda l, j: (l, 0, j)),
            pl.BlockSpec((None, 1, d), lambda l, j: (l, 0, j)),
        ],
        out_specs=pl.BlockSpec((None, rows, d), lambda l, j: (l, 0, j)),
        out_shape=jax.ShapeDtypeStruct((depth, rows, d6), F32),
        compiler_params=_cparams(("parallel", "parallel")),
    )(cc, w_ada, b_ada.reshape(depth, 1, d6))


def _prenorm_kernel(x_ref, ctx_ref, mods_ref, nw_ref, xs_ref, h_ref):
    ctx_row = mods_ref.shape[0] - 1
    nw = nw_ref[...]
    seq = x_ref.shape[0]
    for src_ref, r0, row in ((x_ref, 0, pl.program_id(0)), (ctx_ref, seq, ctx_row)):
        m = mods_ref[row]
        shift, scale = m[0:1, :], m[1:2, :]

        def body(i, carry):
            src = pl.ds(pl.multiple_of(i * TOKEN_BLOCK, TOKEN_BLOCK), TOKEN_BLOCK)
            dst = pl.ds(pl.multiple_of(r0 + i * TOKEN_BLOCK, TOKEN_BLOCK), TOKEN_BLOCK)
            x = src_ref[src, :]
            xs_ref[dst, :] = x
            h_ref[dst, :] = _rms_mod(x, nw, shift, scale).astype(BF16)
            return carry

        lax.fori_loop(0, src_ref.shape[0] // TOKEN_BLOCK, body, 0)


def _prenorm(x, ctx, mods, nw):
    bsz, seq, d = x.shape
    n_ctx = ctx.shape[1]
    t = seq + n_ctx
    return pl.pallas_call(
        _prenorm_kernel,
        grid=(bsz,),
        in_specs=[pl.BlockSpec((None, seq, d), lambda b: (b, 0, 0)),
                  pl.BlockSpec((None, n_ctx, d), lambda b: (b, 0, 0)),
                  pl.BlockSpec(mods.shape, lambda b: (0, 0, 0)),
                  pl.BlockSpec((1, d), lambda b: (0, 0))],
        out_specs=[pl.BlockSpec((None, t, d), lambda b: (b, 0, 0)), pl.BlockSpec((None, t, d), lambda b: (b, 0, 0))],
        out_shape=[jax.ShapeDtypeStruct((bsz, t, d), F32), jax.ShapeDtypeStruct((bsz, t, d), BF16)],
        compiler_params=_cparams(("parallel",)),
    )(x, ctx, mods, nw)


def _proj_kernel(h_ref, w_ref, *rest, epilogue, n_extra):
    extras, outs = rest[:n_extra], rest[n_extra:]
    epilogue(_dot(h_ref[...], w_ref[...].astype(BF16)), extras, outs)


def _ep_plain(acc, extras, outs):
    outs[0][...] = acc.astype(outs[0].dtype)


def _ep_silu(acc, extras, outs):
    outs[0][...] = _silu(acc).astype(outs[0].dtype)


def _ep_rope(acc, extras, outs, *, scale=None):
    c_row, s_row, c_col, s_col = (r[...] for r in extras)
    for s in range(acc.shape[1] // LANES):
        slab = acc[:, s * LANES:(s + 1) * LANES]
        if scale is not None:
            slab = slab * scale
        cos, sin = (c_row, s_row) if s % 2 == 0 else (c_col, s_col)
        rot = slab * cos + pltpu.roll(slab, LANES // 2, 1) * sin
        outs[0][:, s * LANES:(s + 1) * LANES] = rot.astype(outs[0].dtype)


def _ep_gate(acc, extras, outs, *, layer):
    raw = extras[0][...]
    e = jnp.exp(raw - jnp.max(raw, axis=0, keepdims=True))
    sm = e / jnp.sum(e, axis=0, keepdims=True)
    lb = -sm[0:1, :]
    for l in range(layer + 1):
        lb = lb + sm[l:l + 1, :]
    f = lb + (1.0 - lb) * jax.nn.sigmoid(acc)
    outs[0][...] = 1.0 - f
    outs[1][...] = jnp.log(f)


def _proj(h, w, col_block0, ncol, epilogue, out_dtypes, extras=(), extra_specs=()):
    bsz, t, d = h.shape
    tm, tn = t // 2, PROJ_COLS
    in_specs = [
        pl.BlockSpec((None, tm, d), lambda b, i, j: (b, i, 0)),
        pl.BlockSpec((d, tn), lambda b, i, j: (0, col_block0 + j)),
    ] + list(extra_specs)
    out_specs = [pl.BlockSpec((None, tm, tn), lambda b, i, j: (b, i, j)) for _ in out_dtypes]
    out_shape = [jax.ShapeDtypeStruct((bsz, t, tn * ncol), dt) for dt in out_dtypes]
    return pl.pallas_call(
        functools.partial(_proj_kernel, epilogue=epilogue, n_extra=len(extras)),
        grid=(bsz, 2, ncol), in_specs=in_specs, out_specs=out_specs, out_shape=out_shape,
        compiler_params=_cparams(("parallel", "parallel", "parallel")),
    )(h, w, *extras)


def _ret_tables():
    c = RET_CHUNK
    j = np.arange(2 * RET_HEADS, dtype=np.float64)
    lg = np.log1p(-np.exp2(-5.0 - j / 2))
    lg_f, lg_b = lg[0::2], lg[1::2]
    i = np.arange(c, dtype=np.float64)
    diff = i[:, None] - i[None, :]
    mats = np.zeros((RET_HEADS, c, c), np.float64)
    tabs = np.zeros((RET_HEADS, c, LANES), np.float64)
    for h in range(RET_HEADS):
        mats[h] = np.where(diff >= 0, np.exp(diff * lg_f[h]), 0.0) + np.where(diff <= 0, np.exp(-diff * lg_b[h]), 0.0)
        tabs[h, :, 0] = np.exp((i + 1) * lg_f[h])
        tabs[h, :, 1] = np.exp((c - 1 - i) * lg_f[h])
        tabs[h, :, 2] = np.exp((c - i) * lg_b[h])
        tabs[h, :, 3] = np.exp(i * lg_b[h])
        tabs[h, :, 4] = np.exp(c * lg_f[h])
        tabs[h, :, 5] = np.exp(c * lg_b[h])
    return jnp.asarray(mats, F32), jnp.asarray(tabs, F32)


def _ret_kernel(q_ref, k_ref, v_ref, m_ref, tab_ref, out_ref, o_ref, vt_ref, sf_ref, sb_ref, *, n_chunks):
    c = RET_CHUNK
    decay = m_ref[...]
    tab = tab_ref[...]
    qf, kfs, qb, kbs = tab[:, 0:1], tab[:, 1:2], tab[:, 2:3], tab[:, 3:4]
    end_f, end_b = tab[0:1, 4:5], tab[0:1, 5:6]

    def rows(ci):
        return slice(ci * c, (ci + 1) * c)

    for ci in range(n_chunks):
        q, k, v = q_ref[rows(ci), :], k_ref[rows(ci), :], v_ref[rows(ci), :]
        scores = (_dot_nt(q, k) * decay).astype(BF16)
        o_ref[rows(ci), :] = _dot(scores, v)
        vt_ref[ci] = v.astype(F32).T.astype(BF16)

    ctx = n_chunks - 1
    order_f = [ctx] + list(range(ctx))
    order_b = [ctx] + list(range(ctx - 1, -1, -1))
    sf_ref[...] = jnp.zeros_like(sf_ref)
    sb_ref[...] = jnp.zeros_like(sb_ref)
    for step in range(n_chunks):
        for order, s_ref, qd, kd, end in ((order_f, sf_ref, qf, kfs, end_f), (order_b, sb_ref, qb, kbs, end_b)):
            ci = order[step]
            if step > 0:
                o_ref[rows(ci), :] += _dot_nt(q_ref[rows(ci), :], s_ref[...].astype(BF16)) * qd
            if step < n_chunks - 1:
                ks = (k_ref[rows(ci), :].astype(F32) * kd).astype(BF16)
                upd = _dot(vt_ref[ci], ks)
                s_ref[...] = upd if step == 0 else s_ref[...] * end + upd
    for ci in range(n_chunks):
        out_ref[rows(ci), :] = o_ref[rows(ci), :].astype(out_ref.dtype)


def _retention(q, k, v):
    bsz, t, _ = q.shape
    n_chunks = t // RET_CHUNK
    mats, tabs = _ret_tables()
    return pl.pallas_call(
        functools.partial(_ret_kernel, n_chunks=n_chunks),
        grid=(bsz, RET_HEADS),
        in_specs=[
            pl.BlockSpec((None, t, RET_DK), lambda b, h: (b, 0, h)),
            pl.BlockSpec((None, t, RET_DK), lambda b, h: (b, 0, h)),
            pl.BlockSpec((None, t, RET_DV), lambda b, h: (b, 0, h)),
            pl.BlockSpec((None, RET_CHUNK, RET_CHUNK), lambda b, h: (h, 0, 0)),
            pl.BlockSpec((None, RET_CHUNK, LANES), lambda b, h: (h, 0, 0)),
        ],
        out_specs=pl.BlockSpec((None, t, RET_DV), lambda b, h: (b, 0, h)),
        out_shape=jax.ShapeDtypeStruct((bsz, t, RET_HEADS * RET_DV), BF16),
        scratch_shapes=[
            pltpu.VMEM((t, RET_DV), F32),
            pltpu.VMEM((n_chunks, RET_DV, RET_CHUNK), BF16),
            pltpu.VMEM((RET_DV, RET_DK), F32),
            pltpu.VMEM((RET_DV, RET_DK), F32),
        ],
        compiler_params=_cparams(("parallel", "parallel")),
    )(q, k, v, mats, tabs)


def _hg_tables():
    c, sub = HG_CHUNK, HG_SUB
    r = np.arange(c)[:, None]
    col = np.arange(c)[None, :]
    masks = []
    for lower in (True, False):
        s = c // 2
        while s >= sub:
            same = (r // (2 * s)) == (col // (2 * s))
            rh, ch = (r // s) % 2, (col // s) % 2
            masks.append(same & ((rh == 1) & (ch == 0) if lower else (rh == 0) & (ch == 1)))
            s //= 2
    blockdiag = (r // sub) == (col // sub)
    masks.append(blockdiag & (col <= r))
    masks.append(blockdiag & (col >= r))
    tri = np.stack([col <= r, col >= r]).astype(np.float32)
    place = np.stack([np.broadcast_to((np.arange(c) % sub) == j, (c, c)) for j in range(sub)]).astype(np.float32)
    return jnp.asarray(np.stack(masks).astype(np.float32)), jnp.asarray(tri, BF16), jnp.asarray(place, BF16)


def _split3(g):
    hi = g.astype(BF16)
    r1 = g - hi.astype(F32)
    mid = r1.astype(BF16)
    lo = (r1 - mid.astype(F32)).astype(BF16)
    return hi, mid, lo


def _hg_intra(q, k, b, masks, place, lower):
    c, sub = HG_CHUNK, HG_SUB
    n_levels = int(np.log2(c // sub))
    total = jnp.zeros((c, c), F32)
    s = c // 2
    for level in range(n_levels):
        pieces = []
        for m in range(0, c, 2 * s):
            r = m + s - 1 if lower else m + s
            pieces.append(jnp.broadcast_to(b[r:r + 1, :], (2 * s, b.shape[1])))
        ref = pieces[0] if len(pieces) == 1 else jnp.concatenate(pieces, axis=0)
        diff = b - ref
        w = jnp.exp2(jnp.minimum(diff, -diff))
        total = total + _dot_nt((q * w).astype(BF16), (k * w).astype(BF16)) * masks[(0 if lower else n_levels) + level]
        s //= 2
    shape3 = (c // sub, sub, q.shape[1])
    q3, b3 = q.reshape(shape3), b.reshape(shape3)
    a3 = (jnp.log2(jnp.maximum(k, 0.0)) - b).reshape(shape3)
    acc = jnp.zeros((c, c), F32)
    for jj in range(sub):
        aj = jnp.broadcast_to(a3[:, jj:jj + 1, :], shape3)
        p = q3 * jnp.exp2(jnp.minimum(b3 + aj, 0.0))
        acc = acc + _dot(p.reshape(c, q.shape[1]).astype(BF16), place[jj])
    return total + acc * masks[2 * n_levels + (0 if lower else 1)]


def _hg_kernel(q_ref, kf_ref, kb_ref, gf_ref, gb_ref, v_ref, mask_ref, tri_ref, place_ref, out_ref,
               o_ref, bf_ref, bb_ref, vt_ref, s_ref, *, n_chunks):
    c = HG_CHUNK
    masks = [mask_ref[i] for i in range(mask_ref.shape[0])]
    place = [place_ref[i] for i in range(place_ref.shape[0])]
    tri_lo, tri_up = tri_ref[0], tri_ref[1]

    def cumsum(tri, g):
        hi, mid, lo = _split3(g)
        return (_dot(tri, hi) + _dot(tri, mid) + _dot(tri, lo)) * LOG2_E

    def chunk_rows(ci):
        return pl.ds(ci * c if isinstance(ci, int) else pl.multiple_of(ci * c, c), c)

    slabs = [slice(hh * HG_DK, (hh + 1) * HG_DK) for hh in range(HG_HEADS_PER_STEP)]

    def finish(ci, hh, scores, v):
        o_ref[chunk_rows(ci), slabs[hh]] = _dot(scores, v)
        vt_ref[hh, ci] = v.astype(F32).T.astype(BF16)

    first = chunk_rows(0)
    bf_ref[first, :] = jnp.concatenate([cumsum(tri_lo, gf_ref[first, sl]) for sl in slabs], axis=1)
    bb_ref[first, :] = jnp.concatenate([cumsum(tri_up, gb_ref[first, sl]) for sl in slabs], axis=1)
    s_ref[first, :] = jnp.zeros((c, s_ref.shape[1]), BF16)

    def intra(ci, carry):
        prev, nxt = jnp.maximum(ci - 1, 0), jnp.minimum(ci + 1, n_chunks - 1)
        rows, rows_p, rows_n = chunk_rows(ci), chunk_rows(prev), chunk_rows(nxt)
        loaded = []
        for sl in slabs:
            loaded.append((bf_ref[rows, sl], bb_ref[rows, sl], q_ref[rows, sl].astype(F32), kf_ref[rows, sl],
                           kb_ref[rows, sl], s_ref[rows_p, sl], v_ref[rows_p, sl], gf_ref[rows_n, sl],
                           gb_ref[rows_n, sl]))
        for hh, (bf, bb, q, kf, kb, s_prev, v_prev, gf_n, gb_n) in enumerate(loaded):
            finish(prev, hh, s_prev, v_prev)
            bf_ref[rows_n, slabs[hh]] = cumsum(tri_lo, gf_n)
            bb_ref[rows_n, slabs[hh]] = cumsum(tri_up, gb_n)
            scores = _hg_intra(q, kf, bf, masks, place, True) + _hg_intra(q, kb, bb, masks, place, False)
            s_ref[rows, slabs[hh]] = scores.astype(BF16)
        return carry

    lax.fori_loop(0, n_chunks, intra, 0)
    last = chunk_rows(n_chunks - 1)
    for hh, sl in enumerate(slabs):
        finish(n_chunks - 1, hh, s_ref[last, sl], v_ref[last, sl])

    n_ctx = TOKEN_BLOCK // c
    n_lat = n_chunks - n_ctx

    def scan_step(ci, hh, st, k_ref, b_ref, end_row):
        rows, sl = chunk_rows(ci), slabs[hh]
        q = q_ref[rows, sl].astype(F32)
        b = b_ref[rows, sl]
        qs = (q * jnp.exp2(b)).astype(BF16)
        o_ref[rows, sl] += _dot_nt(qs, st.astype(BF16))
        b_end = b[end_row:end_row + 1, :]
        ks = (k_ref[rows, sl] * jnp.exp2(b_end - b)).astype(BF16)
        return st * jnp.exp2(b_end) + _dot(vt_ref[hh, ci], ks)

    def scan(step, carry):
        cf = jnp.where(step < n_ctx, n_lat + step, step - n_ctx)
        cb = n_chunks - 1 - step
        out = []
        for hh in range(HG_HEADS_PER_STEP):
            out.append(scan_step(cf, hh, carry[2 * hh], kf_ref, bf_ref, c - 1))
            out.append(scan_step(cb, hh, carry[2 * hh + 1], kb_ref, bb_ref, 0))
        return tuple(out)

    zero = jnp.zeros((HG_DV, HG_DK), F32)
    lax.fori_loop(0, n_chunks, scan, (zero,) * (2 * HG_HEADS_PER_STEP))

    def emit(ci, carry):
        out_ref[chunk_rows(ci), :] = o_ref[chunk_rows(ci), :].astype(out_ref.dtype)
        return carry

    lax.fori_loop(0, n_chunks, emit, 0)


def _hgrn2(q, kf, kb, gf, gb, v):
    bsz, t, _ = q.shape
    n_chunks = t // HG_CHUNK
    hps = HG_HEADS_PER_STEP
    masks, tri, place = _hg_tables()
    head = lambda b, h: (b, 0, h)
    spec = pl.BlockSpec((None, t, hps * HG_DK), head)
    return pl.pallas_call(
        functools.partial(_hg_kernel, n_chunks=n_chunks),
        grid=(bsz, HG_HEADS // hps),
        in_specs=[spec, spec, spec, spec, spec, spec,
                  pl.BlockSpec(masks.shape, lambda b, h: (0, 0, 0)),
                  pl.BlockSpec(tri.shape, lambda b, h: (0, 0, 0)),
                  pl.BlockSpec(place.shape, lambda b, h: (0, 0, 0))],
        out_specs=pl.BlockSpec((None, t, hps * HG_DV), head),
        out_shape=jax.ShapeDtypeStruct((bsz, t, HG_HEADS * HG_DV), BF16),
        scratch_shapes=[
            pltpu.VMEM((t, hps * HG_DV), F32),
            pltpu.VMEM((t, hps * HG_DK), F32),
            pltpu.VMEM((t, hps * HG_DK), F32),
            pltpu.VMEM((hps, n_chunks, HG_DV, HG_CHUNK), BF16),
            pltpu.VMEM((t, hps * HG_CHUNK), BF16),
        ],
        compiler_params=_cparams(("parallel", "parallel")),
    )(q, kf, kb, gf, gb, v, masks, tri, place)


def _readout_kernel(o_ref, zg_ref, x_ref, mods_ref, wout_ref, gn_ref, nw_ref, wra_ref, wrb_ref,
                    xo_ref, hb_ref, aff_ref, *, n_heads, dv, affine):
    gn = gn_ref[...] if affine else None
    n_sub = x_ref.shape[0] // TOKEN_BLOCK
    ctx_row = mods_ref.shape[0] - 1
    on_last_block = pl.program_id(1) == pl.num_programs(1) - 1
    for s in range(n_sub):
        rows = slice(s * TOKEN_BLOCK, (s + 1) * TOKEN_BLOCK)
        mod_row = jnp.where(on_last_block, ctx_row, pl.program_id(0)) if s == n_sub - 1 else pl.program_id(0)
        mod = mods_ref[mod_row]
        o = o_ref[rows, :].astype(F32)
        parts = []
        for h in range(n_heads):
            oh = o[:, h * dv:(h + 1) * dv]
            ms = jnp.mean(oh * oh, axis=-1, keepdims=True)
            on = oh * lax.rsqrt(ms + EPS)
            parts.append(on if gn is None else on * gn)
        z = zg_ref[rows, :].astype(F32) * jnp.concatenate(parts, axis=1)
        x = x_ref[rows, :] + mod[2:3, :] * _dot(z.astype(BF16), wout_ref[...])
        xo_ref[rows, :] = x
        h2 = _rms_mod(x, nw_ref[...], mod[3:4, :], mod[4:5, :])
        h_hi = h2.astype(BF16)
        hb_ref[rows, :] = h_hi
        h_lo = (h2 - h_hi.astype(F32)).astype(BF16)
        both = _dot(h_hi, wra_ref[...])
        logits = both[:, :LANES] + both[:, LANES:] + _dot(h_lo, wrb_ref[...])
        lane = lax.broadcasted_iota(jnp.int32, logits.shape, 1)
        logits = jnp.where(lane < N_EXPERTS, logits, -1e30)
        e = jnp.exp(logits - jnp.max(logits, axis=-1, keepdims=True))
        aff_ref[rows, :] = e / jnp.sum(e, axis=-1, keepdims=True)


def _readout(o, zg, x, mods, w_out, g_norm, nw_ffn, w_router, n_heads, dv):
    affine = g_norm is not None
    if not affine:
        g_norm = jnp.ones((1, dv), F32)
    w_router_hi = w_router.astype(BF16)
    w_router_lo = (w_router - w_router_hi.astype(F32)).astype(BF16)
    w_router_hi_lo = jnp.concatenate([w_router_hi, w_router_lo], axis=1)
    bsz, t, d = x.shape
    hv = n_heads * dv
    tm = READOUT_SUBBLOCKS * TOKEN_BLOCK
    assert t % tm == 0
    tok = lambda b, i: (b, i, 0)
    full2 = lambda b, i: (0, 0)
    return pl.pallas_call(
        functools.partial(_readout_kernel, n_heads=n_heads, dv=dv, affine=affine),
        grid=(bsz, t // tm),
        in_specs=[
            pl.BlockSpec((None, tm, hv), tok),
            pl.BlockSpec((None, tm, hv), tok),
            pl.BlockSpec((None, tm, d), tok),
            pl.BlockSpec(mods.shape, lambda b, i: (0, 0, 0)),
            pl.BlockSpec((hv, d), full2),
            pl.BlockSpec((1, dv), full2),
            pl.BlockSpec((1, d), full2),
            pl.BlockSpec((d, 2 * LANES), full2),
            pl.BlockSpec((d, LANES), full2),
        ],
        out_specs=[
            pl.BlockSpec((None, tm, d), tok),
            pl.BlockSpec((None, tm, d), tok),
            pl.BlockSpec((None, tm, LANES), tok),
        ],
        out_shape=[
            jax.ShapeDtypeStruct((bsz, t, d), F32),
            jax.ShapeDtypeStruct((bsz, t, d), BF16),
            jax.ShapeDtypeStruct((bsz, t, LANES), F32),
        ],
        compiler_params=_cparams(("parallel", "parallel")),
    )(o, zg, x, mods, w_out, g_norm, nw_ffn, w_router_hi_lo, w_router_hi)


def _excl_cumsum_rows(x01, tri_strict):
    blk = tri_strict.shape[0]
    carry = jnp.zeros((1, x01.shape[1]), F32)
    out = []
    for r0 in range(0, x01.shape[0], blk):
        xb = x01[r0:r0 + blk, :]
        out.append(_dot(tri_strict, xb.astype(BF16)) + carry)
        carry = carry + jnp.sum(xb, axis=0, keepdims=True)
    return out[0] if len(out) == 1 else jnp.concatenate(out, axis=0)


def _route_kernel(aff_ref, packed_ref, tri_ref, pos_ref, *, segments):
    tri_strict = tri_ref[...]
    per_row = LANES // N_EXPERTS
    for r0, n, cap in segments:
        if cap == 0:
            pos_ref[r0:r0 + n, :] = jnp.full((n, LANES), -1.0, F32)
            continue
        aff = aff_ref[r0:r0 + n, :]
        packed = packed_ref[r0 // per_row:(r0 + n) // per_row, :]

        def as_float(pattern):
            return lax.bitcast_convert_type(pattern, F32)

        def refine(thr, low_bit, n_bits):
            counts = []
            for digit in range(1, 1 << n_bits):
                cand = thr | jnp.left_shift(jnp.int32(digit), low_bit)
                counts.append(jnp.sum(jnp.where(packed >= as_float(cand), 1.0, 0.0), axis=0, keepdims=True))
            cnt = jnp.concatenate(counts + [jnp.zeros((1, LANES), F32)], axis=0)
            shift = LANES // 2
            while shift >= N_EXPERTS:
                cnt = cnt + pltpu.roll(cnt, shift, 1)
                shift //= 2
            digit = jnp.sum(jnp.where(cnt >= cap, 1.0, 0.0), axis=0, keepdims=True).astype(jnp.int32)
            return thr | jnp.left_shift(digit, low_bit)

        thr = refine(jnp.zeros((1, LANES), jnp.int32), 28, 3)
        thr = lax.fori_loop(0, 7, lambda i, th: refine(th, 24 - 4 * i, 4), thr)
        beyond = as_float(thr + 1)
        above = jnp.where(aff >= beyond, 1.0, 0.0)
        tied = jnp.where((aff >= as_float(thr)) & (aff < beyond), 1.0, 0.0)
        need = cap - jnp.sum(above, axis=0, keepdims=True)
        tie_rank = _excl_cumsum_rows(tied, tri_strict)
        sel = above + tied * jnp.where(tie_rank < need, 1.0, 0.0)
        slot = _excl_cumsum_rows(sel, tri_strict)
        pos_ref[r0:r0 + n, :] = jnp.where(sel > 0.0, slot, -1.0)


def _route(aff, segments):
    bsz, t, _ = aff.shape
    r = np.arange(TOKEN_BLOCK)
    tri_strict = jnp.asarray((r[None, :] < r[:, None]).astype(np.float32), BF16)
    per_row = LANES // N_EXPERTS
    packed = aff[:, :, :N_EXPERTS].reshape(bsz, t // per_row, LANES)
    return pl.pallas_call(
        functools.partial(_route_kernel, segments=segments),
        grid=(bsz,),
        in_specs=[pl.BlockSpec((None, t, LANES), lambda b: (b, 0, 0)),
                  pl.BlockSpec((None, t // per_row, LANES), lambda b: (b, 0, 0)),
                  pl.BlockSpec(tri_strict.shape, lambda b: (0, 0))],
        out_specs=pl.BlockSpec((None, t, LANES), lambda b: (b, 0, 0)),
        out_shape=jax.ShapeDtypeStruct((bsz, t, LANES), F32),
        compiler_params=_cparams(("parallel",)),
    )(aff, packed, tri_strict)


def _onehot(cond):
    return jnp.where(cond, 1.0, 0.0).astype(BF16)


def _gather_kernel(hb_ref, aff_ref, pos_ref, xe_ref, gate_ref, *, segments):
    pos = pos_ref[...]
    aff = aff_ref[...]
    out0 = 0
    for r0, n, cap in segments:
        if cap == 0:
            continue
        slot = lax.broadcasted_iota(jnp.int32, (cap, n), 0).astype(F32)
        hit = pos[:, r0:r0 + n] == slot
        xe_ref[out0:out0 + cap, :] = _dot(_onehot(hit), hb_ref[r0:r0 + n, :]).astype(BF16)
        picked = jnp.where(hit, aff[:, r0:r0 + n], 0.0)
        folded = picked[:, 0:LANES]
        for c0 in range(LANES, n, LANES):
            folded = folded + picked[:, c0:c0 + LANES]
        gate = jnp.sum(folded, axis=1, keepdims=True)
        gate_ref[out0:out0 + cap, :] = jnp.broadcast_to(gate, (cap, LANES))
        out0 += cap


def _gather(hb, aff_rows, pos_rows, segments):
    bsz, t, d = hb.shape
    rows = sum(cap for _, _, cap in segments)
    slots = lambda b, e: (e * bsz + b, 0, 0)
    per_expert = pl.BlockSpec((None, 1, t), lambda b, e: (b * N_EXPERTS + e, 0, 0))
    return pl.pallas_call(
        functools.partial(_gather_kernel, segments=segments),
        grid=(bsz, N_EXPERTS),
        in_specs=[pl.BlockSpec((None, t, d), lambda b, e: (b, 0, 0)), per_expert, per_expert],
        out_specs=[pl.BlockSpec((None, rows, d), slots), pl.BlockSpec((None, rows, LANES), slots)],
        out_shape=[jax.ShapeDtypeStruct((N_EXPERTS * bsz, rows, d), BF16),
                   jax.ShapeDtypeStruct((N_EXPERTS * bsz, rows, LANES), F32)],
        compiler_params=_cparams(("parallel", "parallel")),
    )(hb, aff_rows, pos_rows)


def _ffn_kernel(xe_ref, gate_ref, wg_ref, wu_ref, wd_ref, y_ref, acc_ref, *, row_chunk):
    f = pl.program_id(1)
    last = pl.num_programs(1) - 1

    def step(mode):
        wg = wg_ref[...].astype(BF16)
        wu = wu_ref[...].astype(BF16)
        wd = wd_ref[...].astype(BF16)
        for r0 in range(0, xe_ref.shape[0], row_chunk):
            rows = slice(r0, r0 + row_chunk)
            xe = xe_ref[rows, :]
            hidden = (_silu(_dot(xe, wg)) * _dot(xe, wu)).astype(BF16)
            part = _dot(hidden, wd)
            if mode == "first":
                acc_ref[rows, :] = part
            elif mode == "middle":
                acc_ref[rows, :] += part
            else:
                y_ref[rows, :] = ((acc_ref[rows, :] + part) * gate_ref[rows, 0:1]).astype(y_ref.dtype)

    pl.when(f == 0)(lambda: step("first"))
    pl.when((f > 0) & (f < last))(lambda: step("middle"))
    pl.when(f == last)(lambda: step("last"))


def _ffn(xe, gate, w_gate, w_up, w_down, layer):
    n_rows, d = xe.shape[1], xe.shape[2]
    ff = w_gate.shape[-1]
    row_chunk = n_rows // 2 if n_rows % 2 == 0 and (n_rows // 2) % 16 == 0 else n_rows
    return pl.pallas_call(
        functools.partial(_ffn_kernel, row_chunk=row_chunk),
        grid=(N_EXPERTS, ff // FF_TILE),
        in_specs=[
            pl.BlockSpec((None, n_rows, d), lambda e, f: (e, 0, 0)),
            pl.BlockSpec((None, n_rows, LANES), lambda e, f: (e, 0, 0)),
            pl.BlockSpec((None, None, d, FF_TILE), lambda e, f: (layer, e, 0, f)),
            pl.BlockSpec((None, None, d, FF_TILE), lambda e, f: (layer, e, 0, f)),
            pl.BlockSpec((None, None, FF_TILE, d), lambda e, f: (layer, e, f, 0)),
        ],
        out_specs=pl.BlockSpec((None, n_rows, d), lambda e, f: (e, 0, 0)),
        out_shape=jax.ShapeDtypeStruct((N_EXPERTS, n_rows, d), BF16),
        scratch_shapes=[pltpu.VMEM((n_rows, d), F32)],
        compiler_params=_cparams(("parallel", "arbitrary")),
    )(xe, gate, w_gate, w_up, w_down)


def _scatter_kernel(x_ref, y_ref, pos_ref, mods_ref, nw_ref, *rest, segments, final, with_next):
    if with_next:
        next_mods_ref, next_nw_ref, out_ref, h_ref = rest
    else:
        out_ref, = rest
    d = y_ref.shape[-1]
    ctx_row = mods_ref.shape[0] - 1
    n_sub = x_ref.shape[0] // TOKEN_BLOCK
    on_last_block = pl.program_id(1) == pl.num_programs(1) - 1

    def block(seg_id, rows):
        cap = segments[seg_id][2]
        slot0 = sum(seg[2] for seg in segments[:seg_id])
        mod_row = ctx_row if seg_id == 1 else pl.program_id(0)
        gate = mods_ref[mod_row][5:6, :]
        y_all = y_ref[:, slot0:slot0 + cap, :].reshape(N_EXPERTS * cap, d)
        slot = lax.broadcasted_iota(jnp.int32, (TOKEN_BLOCK, cap), 1).astype(F32)
        pos = pos_ref[rows, :]
        onehots = jnp.concatenate([_onehot(pos[:, e:e + 1] == slot) for e in range(N_EXPERTS)], axis=1)
        x = x_ref[rows, :] + gate * _dot(onehots, y_all)
        if final:
            x = x * lax.rsqrt(jnp.mean(x * x, axis=-1, keepdims=True) + EPS) * nw_ref[...]
        out_ref[rows, :] = x
        if with_next:
            m = next_mods_ref[mod_row]
            h_ref[rows, :] = _rms_mod(x, next_nw_ref[...], m[0:1, :], m[1:2, :]).astype(BF16)

    for s in range(n_sub):
        rows = slice(s * TOKEN_BLOCK, (s + 1) * TOKEN_BLOCK)
        if segments[1][2] > 0 and s == n_sub - 1:
            pl.when(jnp.logical_not(on_last_block))(functools.partial(block, 0, rows))
            pl.when(on_last_block)(functools.partial(block, 1, rows))
        else:
            block(0, rows)


def _scatter(x, y, pos, mods, nw_final, segments, final, next_norm=None):
    bsz, t, d = x.shape
    n_slots = y.shape[2]
    out_rows = sum(n for _, n, cap in segments if cap > 0)
    tm = (3 if segments[1][2] > 0 else 2) * TOKEN_BLOCK
    assert out_rows % tm == 0
    tok = lambda b, i: (b, i, 0)
    const2, const3 = (lambda b, i: (0, 0)), (lambda b, i: (0, 0, 0))
    in_specs = [
        pl.BlockSpec((None, tm, d), tok),
        pl.BlockSpec((N_EXPERTS, None, n_slots, d), lambda b, i: (0, b, 0, 0)),
        pl.BlockSpec((None, tm, LANES), tok),
        pl.BlockSpec(mods.shape, const3),
        pl.BlockSpec((1, d), const2),
    ]
    args = [x, y, pos, mods, nw_final]
    out_specs = [pl.BlockSpec((None, tm, d), tok)]
    out_shape = [jax.ShapeDtypeStruct((bsz, out_rows, d), F32)]
    if next_norm is not None:
        in_specs += [pl.BlockSpec(next_norm[0].shape, const3), pl.BlockSpec((1, d), const2)]
        args += list(next_norm)
        out_specs.append(pl.BlockSpec((None, tm, d), tok))
        out_shape.append(jax.ShapeDtypeStruct((bsz, out_rows, d), BF16))
    return pl.pallas_call(
        functools.partial(_scatter_kernel, segments=segments, final=final, with_next=next_norm is not None),
        grid=(bsz, out_rows // tm),
        in_specs=in_specs, out_specs=out_specs, out_shape=out_shape,
        compiler_params=_cparams(("parallel", "parallel")),
    )(*args)


def _moe(x, hb, aff, mods, w_gate, w_up, w_down, layer, nw_final, with_ctx, final, next_norm):
    bsz, t, d = x.shape
    seq = t - TOKEN_BLOCK
    cap_lat = EC_CAPACITY * seq // N_EXPERTS
    cap_ctx = EC_CAPACITY * TOKEN_BLOCK // N_EXPERTS if with_ctx else 0
    segments = ((0, seq, cap_lat), (seq, TOKEN_BLOCK, cap_ctx))
    pos = _route(aff, segments)
    pos_rows = jnp.swapaxes(pos[:, :, :N_EXPERTS], 1, 2).reshape(bsz * N_EXPERTS, 1, t)
    aff_rows = jnp.swapaxes(aff[:, :, :N_EXPERTS], 1, 2).reshape(bsz * N_EXPERTS, 1, t)
    xe, gate = _gather(hb, aff_rows, pos_rows, segments)
    rows = xe.shape[1]
    y = _ffn(xe.reshape(N_EXPERTS, bsz * rows, d), gate.reshape(N_EXPERTS, bsz * rows, LANES),
             w_gate, w_up, w_down, layer)
    y = y.reshape(N_EXPERTS, bsz, rows, d)
    return _scatter(x, y, pos, mods, nw_final, segments, final, next_norm)


def _rope_tables(seq, n_ctx):
    half = RET_DK // 2
    t = jnp.arange(seq)
    inv = ROPE_THETA ** (-jnp.arange(0, half, 2, dtype=F32) / half)
    tabs = []
    for pos in ((t // GRID_W).astype(F32), (t % GRID_W).astype(F32)):
        ang = pos[:, None] * inv
        cos, sin = jnp.cos(ang), jnp.sin(ang)
        tabs.append(jnp.concatenate([cos, cos], axis=1))
        tabs.append(jnp.concatenate([-sin, sin], axis=1))
    ident = [jnp.ones, jnp.zeros, jnp.ones, jnp.zeros]
    return [jnp.concatenate([tab, fn((n_ctx, 2 * (half // 2)), F32)], axis=0) for tab, fn in zip(tabs, ident)]


def kernel(x, c, ctx, c_ctx, w_ada, b_ada, norm_mix, norm_ffn, ret_w_in, ret_w_out, hg_w_in, hg_g_norm,
           hg_lower_bounds, hg_w_out, moe_router, moe_w_gate, moe_w_up, moe_w_down, norm_final):
    bsz, seq, d = x.shape
    n_ctx = ctx.shape[1]
    depth = w_ada.shape[0]
    assert n_ctx == TOKEN_BLOCK and seq % TOKEN_BLOCK == 0 and d == RET_HEADS * RET_DK == HG_HEADS * HG_DK

    cc = jnp.concatenate([c, c_ctx[None, :], jnp.zeros((16 - bsz - 1, d), F32)], axis=0)
    mods_all = _ada(cc, w_ada, b_ada)[:, :bsz + 1].reshape(depth, bsz + 1, 6, d)
    t = seq + n_ctx
    rope = _rope_tables(seq, n_ctx)
    rope_specs = [pl.BlockSpec((t // 2, LANES), lambda b, i, j: (i, 0))] * 4
    w_router = jnp.pad(moe_router, ((0, 0), (0, 0), (0, LANES - N_EXPERTS)))

    xs, h = _prenorm(x, ctx, mods_all[0], norm_mix[0][None, :])
    for layer in range(depth):
        last = layer == depth - 1
        slot = layer // 2
        mods = mods_all[layer]
        if layer % 2 == 0:
            assert RET_HEADS * RET_DK == PROJ_COLS
            w = ret_w_in[slot]
            assert np.log2(RET_DK ** -0.5) % 1 == 0
            q, = _proj(h, w, 0, 1, _ep_rope, [BF16], rope, rope_specs)
            k, = _proj(h, w, 1, 1, functools.partial(_ep_rope, scale=RET_DK ** -0.5), [BF16], rope, rope_specs)
            v, = _proj(h, w, 2, 2, _ep_plain, [BF16])
            zg, = _proj(h, w, 4, 2, _ep_silu, [BF16])
            o = _retention(q, k, v)
            w_out, g_norm, n_heads, dv = ret_w_out[slot].astype(BF16), None, RET_HEADS, RET_DV
        else:
            assert HG_HEADS * HG_DK == PROJ_COLS
            w = hg_w_in[slot]
            lb_specs = [pl.BlockSpec((depth, PROJ_COLS), lambda b, i, j: (0, 0))]
            ep_gate = functools.partial(_ep_gate, layer=layer)
            q, = _proj(h, w, 0, 1, _ep_silu, [BF16])
            kf, gf = _proj(h, w, 1, 1, ep_gate, [F32, F32], (hg_lower_bounds[:, 0, :],), lb_specs)
            kb, gb = _proj(h, w, 2, 1, ep_gate, [F32, F32], (hg_lower_bounds[:, 1, :],), lb_specs)
            v, = _proj(h, w, 3, 1, _ep_plain, [BF16])
            zg, = _proj(h, w, 4, 1, _ep_silu, [BF16])
            o = _hgrn2(q, kf, kb, gf, gb, v)
            w_out, g_norm, n_heads, dv = hg_w_out[slot].astype(BF16), hg_g_norm[slot][None, :], HG_HEADS, HG_DV
        xs, hb, aff = _readout(o, zg, xs, mods, w_out, g_norm, norm_ffn[layer][None, :], w_router[layer],
                                       n_heads, dv)
        next_norm = None if last else (mods_all[layer + 1], norm_mix[layer + 1][None, :])
        res = _moe(xs, hb, aff, mods, moe_w_gate, moe_w_up, moe_w_down, layer, norm_final[None, :],
                   with_ctx=not last, final=last, next_norm=next_norm)
        xs, h = res if next_norm is not None else (res[0], None)
    return xs
```

```python
import functools

import numpy as np
import jax
import jax.numpy as jnp
from jax import lax
from jax.experimental import pallas as pl
from jax.experimental.pallas import tpu as pltpu

F32 = jnp.float32
BF16 = jnp.bfloat16
EPS = 1e-6
ROPE_THETA = 10000.0
GRID_W = 64

LANES = 128
TOKEN_BLOCK = 256
RET_HEADS, RET_DK, RET_DV, RET_CHUNK = 4, 256, 512, 256
HG_HEADS, HG_DK, HG_DV, HG_CHUNK, HG_SUB = 8, 128, 128, 128, 8
HG_HEADS_PER_STEP = 2
HG_FAST_SPAN = 60.0
READOUT_SUBBLOCKS = 3
LOG2_E = 1.4426950408889634
N_EXPERTS = 16
EC_CAPACITY = 2
FF_TILE = 256
PROJ_COLS = 1024
VMEM_LIMIT = 56 * 1024 * 1024

HIGHEST = lax.Precision.HIGHEST


def _cparams(sem):
    return pltpu.CompilerParams(dimension_semantics=sem, vmem_limit_bytes=VMEM_LIMIT)


def _dot(a, b, **kw):
    return jnp.dot(a, b, preferred_element_type=F32, **kw)


def _dot_nt(a, b):
    return lax.dot_general(a, b, (((1,), (1,)), ((), ())), preferred_element_type=F32)


def _silu(x):
    return x * jax.nn.sigmoid(x)


def _rms_mod(x, nw, shift, scale):
    ms = jnp.mean(x * x, axis=-1, keepdims=True)
    h = x * lax.rsqrt(ms + EPS) * nw
    return h * (1.0 + scale) + shift


def _ada_kernel(c_ref, w_ref, b_ref, o_ref):
    o_ref[...] = _dot(_silu(c_ref[...]), w_ref[...], precision=HIGHEST) + b_ref[...]


def _ada(cc, w_ada, b_ada):
    depth, d, d6 = w_ada.shape
    rows = cc.shape[0]
    return pl.pallas_call(
        _ada_kernel,
        grid=(depth, d6 // d),
        in_specs=[
            pl.BlockSpec((rows, d), lambda l, j: (0, 0)),
            pl.BlockSpec((None, d, d), lambda l, j: (l, 0, j)),
            pl.BlockSpec((None, 1, d), lambda l, j: (l, 0, j)),
        ],
        out_specs=pl.BlockSpec((None, rows, d), lambda l, j: (l, 0, j)),
        out_shape=jax.ShapeDtypeStruct((depth, rows, d6), F32),
        compiler_params=_cparams(("parallel", "parallel")),
    )(cc, w_ada, b_ada.reshape(depth, 1, d6))


def _prenorm_kernel(x_ref, ctx_ref, mods_ref, nw_ref, xs_ref, h_ref):
    ctx_row = mods_ref.shape[0] - 1
    nw = nw_ref[...]
    seq = x_ref.shape[0]
    for src_ref, r0, row in ((x_ref, 0, pl.program_id(0)), (ctx_ref, seq, ctx_row)):
        m = mods_ref[row]
        shift, scale = m[0:1, :], m[1:2, :]

        def body(i, carry):
            src = pl.ds(pl.multiple_of(i * TOKEN_BLOCK, TOKEN_BLOCK), TOKEN_BLOCK)
            dst = pl.ds(pl.multiple_of(r0 + i * TOKEN_BLOCK, TOKEN_BLOCK), TOKEN_BLOCK)
            x = src_ref[src, :]
            xs_ref[dst, :] = x
            h_ref[dst, :] = _rms_mod(x, nw, shift, scale).astype(BF16)
            return carry

        lax.fori_loop(0, src_ref.shape[0] // TOKEN_BLOCK, body, 0)


def _prenorm(x, ctx, mods, nw):
    bsz, seq, d = x.shape
    n_ctx = ctx.shape[1]
    t = seq + n_ctx
    return pl.pallas_call(
        _prenorm_kernel,
        grid=(bsz,),
        in_specs=[pl.BlockSpec((None, seq, d), lambda b: (b, 0, 0)),
                  pl.BlockSpec((None, n_ctx, d), lambda b: (b, 0, 0)),
                  pl.BlockSpec(mods.shape, lambda b: (0, 0, 0)),
                  pl.BlockSpec((1, d), lambda b: (0, 0))],
        out_specs=[pl.BlockSpec((None, t, d), lambda b: (b, 0, 0)), pl.BlockSpec((None, t, d), lambda b: (b, 0, 0))],
        out_shape=[jax.ShapeDtypeStruct((bsz, t, d), F32), jax.ShapeDtypeStruct((bsz, t, d), BF16)],
        compiler_params=_cparams(("parallel",)),
    )(x, ctx, mods, nw)


def _proj_kernel(h_ref, w_ref, *rest, epilogue, n_extra):
    extras, outs = rest[:n_extra], rest[n_extra:]
    epilogue(_dot(h_ref[...], w_ref[...].astype(BF16)), extras, outs)


def _ep_plain(acc, extras, outs):
    outs[0][...] = acc.astype(outs[0].dtype)


def _ep_silu(acc, extras, outs):
    outs[0][...] = _silu(acc).astype(outs[0].dtype)


def _ep_rope(acc, extras, outs, *, scale=None):
    c_row, s_row, c_col, s_col = (r[...] for r in extras)
    for s in range(acc.shape[1] // LANES):
        slab = acc[:, s * LANES:(s + 1) * LANES]
        if scale is not None:
            slab = slab * scale
        cos, sin = (c_row, s_row) if s % 2 == 0 else (c_col, s_col)
        rot = slab * cos + pltpu.roll(slab, LANES // 2, 1) * sin
        outs[0][:, s * LANES:(s + 1) * LANES] = rot.astype(outs[0].dtype)


def _ep_gate(acc, extras, outs, *, layer):
    raw = extras[0][...]
    e = jnp.exp(raw - jnp.max(raw, axis=0, keepdims=True))
    sm = e / jnp.sum(e, axis=0, keepdims=True)
    lb = -sm[0:1, :]
    for l in range(layer + 1):
        lb = lb + sm[l:l + 1, :]
    f = lb + (1.0 - lb) * jax.nn.sigmoid(acc)
    outs[0][...] = 1.0 - f
    outs[1][...] = jnp.log(f)


def _proj(h, w, col_block0, ncol, epilogue, out_dtypes, extras=(), extra_specs=()):
    bsz, t, d = h.shape
    tm, tn = t // 2, PROJ_COLS
    in_specs = [
        pl.BlockSpec((None, tm, d), lambda j, b, i: (b, i, 0)),
        pl.BlockSpec((d, tn), lambda j, b, i: (0, col_block0 + j)),
    ] + list(extra_specs)
    out_specs = [pl.BlockSpec((None, tm, tn), lambda j, b, i: (b, i, j)) for _ in out_dtypes]
    out_shape = [jax.ShapeDtypeStruct((bsz, t, tn * ncol), dt) for dt in out_dtypes]
    return pl.pallas_call(
        functools.partial(_proj_kernel, epilogue=epilogue, n_extra=len(extras)),
        grid=(ncol, bsz, 2), in_specs=in_specs, out_specs=out_specs, out_shape=out_shape,
        compiler_params=_cparams(("parallel", "parallel", "parallel")),
    )(h, w, *extras)


def _ret_tables():
    c = RET_CHUNK
    j = np.arange(2 * RET_HEADS, dtype=np.float64)
    lg = np.log1p(-np.exp2(-5.0 - j / 2))
    lg_f, lg_b = lg[0::2], lg[1::2]
    i = np.arange(c, dtype=np.float64)
    diff = i[:, None] - i[None, :]
    mats = np.zeros((RET_HEADS, c, c), np.float64)
    tabs = np.zeros((RET_HEADS, c, LANES), np.float64)
    for h in range(RET_HEADS):
        mats[h] = np.where(diff >= 0, np.exp(diff * lg_f[h]), 0.0) + np.where(diff <= 0, np.exp(-diff * lg_b[h]), 0.0)
        tabs[h, :, 0] = np.exp((i + 1) * lg_f[h])
        tabs[h, :, 1] = np.exp((c - 1 - i) * lg_f[h])
        tabs[h, :, 2] = np.exp((c - i) * lg_b[h])
        tabs[h, :, 3] = np.exp(i * lg_b[h])
        tabs[h, :, 4] = np.exp(c * lg_f[h])
        tabs[h, :, 5] = np.exp(c * lg_b[h])
    return jnp.asarray(mats, F32), jnp.asarray(tabs, F32)


def _ret_kernel(q_ref, k_ref, v_ref, m_ref, tab_ref, out_ref, o_ref, vt_ref, sf_ref, sb_ref, *, n_chunks):
    c = RET_CHUNK
    decay = m_ref[...]
    tab = tab_ref[...]
    qf, kfs, qb, kbs = tab[:, 0:1], tab[:, 1:2], tab[:, 2:3], tab[:, 3:4]
    end_f, end_b = tab[0:1, 4:5], tab[0:1, 5:6]

    def rows(ci):
        return slice(ci * c, (ci + 1) * c)

    for ci in range(n_chunks):
        q, k, v = q_ref[rows(ci), :], k_ref[rows(ci), :], v_ref[rows(ci), :]
        scores = (_dot_nt(q, k) * decay).astype(BF16)
        o_ref[rows(ci), :] = _dot(scores, v)
        vt_ref[ci] = v.astype(F32).T.astype(BF16)

    ctx = n_chunks - 1
    order_f = [ctx] + list(range(ctx))
    order_b = [ctx] + list(range(ctx - 1, -1, -1))
    sf_ref[...] = jnp.zeros_like(sf_ref)
    sb_ref[...] = jnp.zeros_like(sb_ref)
    for step in range(n_chunks):
        for order, s_ref, qd, kd, end in ((order_f, sf_ref, qf, kfs, end_f), (order_b, sb_ref, qb, kbs, end_b)):
            ci = order[step]
            if step > 0:
                o_ref[rows(ci), :] += _dot_nt(q_ref[rows(ci), :], s_ref[...].astype(BF16)) * qd
            if step < n_chunks - 1:
                ks = (k_ref[rows(ci), :].astype(F32) * kd).astype(BF16)
                upd = _dot(vt_ref[ci], ks)
                s_ref[...] = upd if step == 0 else s_ref[...] * end + upd
    for ci in range(n_chunks):
        out_ref[rows(ci), :] = o_ref[rows(ci), :].astype(out_ref.dtype)


def _retention(q, k, v):
    bsz, t, _ = q.shape
    n_chunks = t // RET_CHUNK
    mats, tabs = _ret_tables()
    return pl.pallas_call(
        functools.partial(_ret_kernel, n_chunks=n_chunks),
        grid=(bsz, RET_HEADS),
        in_specs=[
            pl.BlockSpec((None, t, RET_DK), lambda b, h: (b, 0, h)),
            pl.BlockSpec((None, t, RET_DK), lambda b, h: (b, 0, h)),
            pl.BlockSpec((None, t, RET_DV), lambda b, h: (b, 0, h)),
            pl.BlockSpec((None, RET_CHUNK, RET_CHUNK), lambda b, h: (h, 0, 0)),
            pl.BlockSpec((None, RET_CHUNK, LANES), lambda b, h: (h, 0, 0)),
        ],
        out_specs=pl.BlockSpec((None, t, RET_DV), lambda b, h: (b, 0, h)),
        out_shape=jax.ShapeDtypeStruct((bsz, t, RET_HEADS * RET_DV), BF16),
        scratch_shapes=[
            pltpu.VMEM((t, RET_DV), F32),
            pltpu.VMEM((n_chunks, RET_DV, RET_CHUNK), BF16),
            pltpu.VMEM((RET_DV, RET_DK), F32),
            pltpu.VMEM((RET_DV, RET_DK), F32),
        ],
        compiler_params=_cparams(("parallel", "parallel")),
    )(q, k, v, mats, tabs)


def _hg_tables():
    c, sub = HG_CHUNK, HG_SUB
    r = np.arange(c)[:, None]
    col = np.arange(c)[None, :]
    masks = []
    for lower in (True, False):
        s = c // 2
        while s >= sub:
            same = (r // (2 * s)) == (col // (2 * s))
            rh, ch = (r // s) % 2, (col // s) % 2
            masks.append(same & ((rh == 1) & (ch == 0) if lower else (rh == 0) & (ch == 1)))
            s //= 2
    blockdiag = (r // sub) == (col // sub)
    masks.append(blockdiag & (col <= r))
    masks.append(blockdiag & (col >= r))
    for lower in (True, False):
        s = c // 2
        while s >= sub:
            far = ((r // s) % 2 == 1) if lower else ((r // s) % 2 == 0)
            masks.append(np.broadcast_to(np.where(far, 1.0, -1.0), (c, c)))
            s //= 2
    tri = np.stack([col <= r, col >= r]).astype(np.float32)
    place = np.stack([np.broadcast_to((np.arange(c) % sub) == j, (c, c)) for j in range(sub)]).astype(np.float32)
    return jnp.asarray(np.stack(masks).astype(np.float32)), jnp.asarray(tri, BF16), jnp.asarray(place, BF16)


def _split3(g):
    hi = g.astype(BF16)
    r1 = g - hi.astype(F32)
    mid = r1.astype(BF16)
    lo = (r1 - mid.astype(F32)).astype(BF16)
    return hi, mid, lo


def _hg_intra(q, k, b, masks, place, lower, exact_diagonal):
    c, sub = HG_CHUNK, HG_SUB
    n_levels = int(np.log2(c // sub))
    total = jnp.zeros((c, c), F32)
    s = c // 2
    for level in range(n_levels):
        pieces = []
        for m in range(0, c, 2 * s):
            r = m + s - 1 if lower else m + s
            pieces.append(jnp.broadcast_to(b[r:r + 1, :], (2 * s, b.shape[1])))
        ref = pieces[0] if len(pieces) == 1 else jnp.concatenate(pieces, axis=0)
        w = jnp.exp2((b - ref) * masks[2 * n_levels + 2 + (0 if lower else n_levels) + level])
        total = total + _dot_nt((q * w).astype(BF16), (k * w).astype(BF16)) * masks[(0 if lower else n_levels) + level]
        s //= 2
    diag_mask = masks[2 * n_levels + (0 if lower else 1)]
    if not exact_diagonal:
        firsts = [u * sub if lower else u * sub + sub - 1 for u in range(c // sub)]
        ref = jnp.concatenate([jnp.broadcast_to(b[r:r + 1, :], (sub, b.shape[1])) for r in firsts], axis=0)
        d = b - ref
        p = _dot_nt((q * jnp.exp2(d)).astype(BF16), (k * jnp.exp2(-d)).astype(BF16))
        return total + p * diag_mask, jnp.max(-d)
    shape3 = (c // sub, sub, q.shape[1])
    q3, b3 = q.reshape(shape3), b.reshape(shape3)
    a3 = (jnp.log2(jnp.maximum(k, 0.0)) - b).reshape(shape3)
    acc = jnp.zeros((c, c), F32)
    for jj in range(sub):
        aj = jnp.broadcast_to(a3[:, jj:jj + 1, :], shape3)
        p = q3 * jnp.exp2(jnp.minimum(b3 + aj, 0.0))
        acc = acc + _dot(p.reshape(c, q.shape[1]).astype(BF16), place[jj])
    return total + acc * diag_mask, None


def _hg_kernel(q_ref, kf_ref, kb_ref, gf_ref, gb_ref, v_ref, mask_ref, tri_ref, place_ref, out_ref,
               o_ref, bf_ref, bb_ref, vt_ref, s_ref, *, n_chunks):
    c = HG_CHUNK
    masks = [mask_ref[i] for i in range(mask_ref.shape[0])]
    place = [place_ref[i] for i in range(place_ref.shape[0])]
    tri_lo, tri_up = tri_ref[0], tri_ref[1]

    def cumsum(tri, g):
        hi, mid, lo = _split3(g)
        return (_dot(tri, hi) + _dot(tri, mid) + _dot(tri, lo)) * LOG2_E

    def chunk_rows(ci):
        return pl.ds(ci * c if isinstance(ci, int) else pl.multiple_of(ci * c, c), c)

    slabs = [slice(hh * HG_DK, (hh + 1) * HG_DK) for hh in range(HG_HEADS_PER_STEP)]

    def finish(ci, hh, scores, v):
        o_ref[chunk_rows(ci), slabs[hh]] = _dot(scores, v)
        vt_ref[hh, ci] = v.astype(F32).T.astype(BF16)

    first = chunk_rows(0)
    bf_ref[first, :] = jnp.concatenate([cumsum(tri_lo, gf_ref[first, sl]) for sl in slabs], axis=1)
    bb_ref[first, :] = jnp.concatenate([cumsum(tri_up, gb_ref[first, sl]) for sl in slabs], axis=1)
    s_ref[first, :] = jnp.zeros((c, s_ref.shape[1]), BF16)

    def intra(ci, carry):
        prev, nxt = jnp.maximum(ci - 1, 0), jnp.minimum(ci + 1, n_chunks - 1)
        rows, rows_p, rows_n = chunk_rows(ci), chunk_rows(prev), chunk_rows(nxt)
        loaded = []
        for sl in slabs:
            loaded.append((bf_ref[rows, sl], bb_ref[rows, sl], q_ref[rows, sl].astype(F32), kf_ref[rows, sl],
                           kb_ref[rows, sl], s_ref[rows_p, sl], v_ref[rows_p, sl], gf_ref[rows_n, sl],
                           gb_ref[rows_n, sl]))
        span = 0.0
        for hh, (bf, bb, q, kf, kb, s_prev, v_prev, gf_n, gb_n) in enumerate(loaded):
            finish(prev, hh, s_prev, v_prev)
            bf_ref[rows_n, slabs[hh]] = cumsum(tri_lo, gf_n)
            bb_ref[rows_n, slabs[hh]] = cumsum(tri_up, gb_n)
            lo, span_lo = _hg_intra(q, kf, bf, masks, place, True, False)
            up, span_up = _hg_intra(q, kb, bb, masks, place, False, False)
            s_ref[rows, slabs[hh]] = (lo + up).astype(BF16)
            span = jnp.maximum(span, jnp.maximum(span_lo, span_up))

        @pl.when(span > HG_FAST_SPAN)
        def _():
            for sl in slabs:
                q = q_ref[rows, sl].astype(F32)
                lo, _ = _hg_intra(q, kf_ref[rows, sl], bf_ref[rows, sl], masks, place, True, True)
                up, _ = _hg_intra(q, kb_ref[rows, sl], bb_ref[rows, sl], masks, place, False, True)
                s_ref[rows, sl] = (lo + up).astype(BF16)

        return carry

    lax.fori_loop(0, n_chunks, intra, 0)
    last = chunk_rows(n_chunks - 1)
    for hh, sl in enumerate(slabs):
        finish(n_chunks - 1, hh, s_ref[last, sl], v_ref[last, sl])

    n_ctx = TOKEN_BLOCK // c
    n_lat = n_chunks - n_ctx

    def scan_step(ci, hh, st, k_ref, b_ref, end_row):
        rows, sl = chunk_rows(ci), slabs[hh]
        q = q_ref[rows, sl].astype(F32)
        b = b_ref[rows, sl]
        qs = (q * jnp.exp2(b)).astype(BF16)
        o_ref[rows, sl] += _dot_nt(qs, st.astype(BF16))
        b_end = b[end_row:end_row + 1, :]
        ks = (k_ref[rows, sl] * jnp.exp2(b_end - b)).astype(BF16)
        return st * jnp.exp2(b_end) + _dot(vt_ref[hh, ci], ks)

    def scan(step, carry):
        cf = jnp.where(step < n_ctx, n_lat + step, step - n_ctx)
        cb = n_chunks - 1 - step
        out = []
        for hh in range(HG_HEADS_PER_STEP):
            out.append(scan_step(cf, hh, carry[2 * hh], kf_ref, bf_ref, c - 1))
            out.append(scan_step(cb, hh, carry[2 * hh + 1], kb_ref, bb_ref, 0))
        return tuple(out)

    zero = jnp.zeros((HG_DV, HG_DK), F32)
    lax.fori_loop(0, n_chunks, scan, (zero,) * (2 * HG_HEADS_PER_STEP))

    def emit(ci, carry):
        out_ref[chunk_rows(ci), :] = o_ref[chunk_rows(ci), :].astype(out_ref.dtype)
        return carry

    lax.fori_loop(0, n_chunks, emit, 0)


def _hgrn2(q, kf, kb, gf, gb, v):
    bsz, t, _ = q.shape
    n_chunks = t // HG_CHUNK
    hps = HG_HEADS_PER_STEP
    masks, tri, place = _hg_tables()
    head = lambda b, h: (b, 0, h)
    spec = pl.BlockSpec((None, t, hps * HG_DK), head)
    return pl.pallas_call(
        functools.partial(_hg_kernel, n_chunks=n_chunks),
        grid=(bsz, HG_HEADS // hps),
        in_specs=[spec, spec, spec, spec, spec, spec,
                  pl.BlockSpec(masks.shape, lambda b, h: (0, 0, 0)),
                  pl.BlockSpec(tri.shape, lambda b, h: (0, 0, 0)),
                  pl.BlockSpec(place.shape, lambda b, h: (0, 0, 0))],
        out_specs=pl.BlockSpec((None, t, hps * HG_DV), head),
        out_shape=jax.ShapeDtypeStruct((bsz, t, HG_HEADS * HG_DV), BF16),
        scratch_shapes=[
            pltpu.VMEM((t, hps * HG_DV), F32),
            pltpu.VMEM((t, hps * HG_DK), F32),
            pltpu.VMEM((t, hps * HG_DK), F32),
            pltpu.VMEM((hps, n_chunks, HG_DV, HG_CHUNK), BF16),
            pltpu.VMEM((t, hps * HG_CHUNK), BF16),
        ],
        compiler_params=_cparams(("parallel", "parallel")),
    )(q, kf, kb, gf, gb, v, masks, tri, place)


def _readout_kernel(o_ref, zg_ref, x_ref, mods_ref, wout_ref, gn_ref, nw_ref, wra_ref, wrb_ref,
                    xo_ref, hb_ref, aff_ref, *, n_heads, dv, affine):
    gn = gn_ref[...] if affine else None
    n_sub = x_ref.shape[0] // TOKEN_BLOCK
    ctx_row = mods_ref.shape[0] - 1
    on_last_block = pl.program_id(1) == pl.num_programs(1) - 1
    for s in range(n_sub):
        rows = slice(s * TOKEN_BLOCK, (s + 1) * TOKEN_BLOCK)
        mod_row = jnp.where(on_last_block, ctx_row, pl.program_id(0)) if s == n_sub - 1 else pl.program_id(0)
        mod = mods_ref[mod_row]
        o = o_ref[rows, :].astype(F32)
        parts = []
        for h in range(n_heads):
            oh = o[:, h * dv:(h + 1) * dv]
            ms = jnp.mean(oh * oh, axis=-1, keepdims=True)
            on = oh * lax.rsqrt(ms + EPS)
            parts.append(on if gn is None else on * gn)
        z = zg_ref[rows, :].astype(F32) * jnp.concatenate(parts, axis=1)
        x = x_ref[rows, :] + mod[2:3, :] * _dot(z.astype(BF16), wout_ref[...])
        xo_ref[rows, :] = x
        h2 = _rms_mod(x, nw_ref[...], mod[3:4, :], mod[4:5, :])
        h_hi = h2.astype(BF16)
        hb_ref[rows, :] = h_hi
        h_lo = (h2 - h_hi.astype(F32)).astype(BF16)
        both = _dot(h_hi, wra_ref[...])
        logits = both[:, :LANES] + both[:, LANES:] + _dot(h_lo, wrb_ref[...])
        lane = lax.broadcasted_iota(jnp.int32, logits.shape, 1)
        logits = jnp.where(lane < N_EXPERTS, logits, -1e30)
        e = jnp.exp(logits - jnp.max(logits, axis=-1, keepdims=True))
        aff_ref[rows, :] = e / jnp.sum(e, axis=-1, keepdims=True)


def _readout(o, zg, x, mods, w_out, g_norm, nw_ffn, w_router, n_heads, dv):
    affine = g_norm is not None
    if not affine:
        g_norm = jnp.ones((1, dv), F32)
    w_router_hi = w_router.astype(BF16)
    w_router_lo = (w_router - w_router_hi.astype(F32)).astype(BF16)
    w_router_hi_lo = jnp.concatenate([w_router_hi, w_router_lo], axis=1)
    bsz, t, d = x.shape
    hv = n_heads * dv
    tm = READOUT_SUBBLOCKS * TOKEN_BLOCK
    assert t % tm == 0
    tok = lambda b, i: (b, i, 0)
    full2 = lambda b, i: (0, 0)
    return pl.pallas_call(
        functools.partial(_readout_kernel, n_heads=n_heads, dv=dv, affine=affine),
        grid=(bsz, t // tm),
        in_specs=[
            pl.BlockSpec((None, tm, hv), tok),
            pl.BlockSpec((None, tm, hv), tok),
            pl.BlockSpec((None, tm, d), tok),
            pl.BlockSpec(mods.shape, lambda b, i: (0, 0, 0)),
            pl.BlockSpec((hv, d), full2),
            pl.BlockSpec((1, dv), full2),
            pl.BlockSpec((1, d), full2),
            pl.BlockSpec((d, 2 * LANES), full2),
            pl.BlockSpec((d, LANES), full2),
        ],
        out_specs=[
            pl.BlockSpec((None, tm, d), tok),
            pl.BlockSpec((None, tm, d), tok),
            pl.BlockSpec((None, tm, LANES), tok),
        ],
        out_shape=[
            jax.ShapeDtypeStruct((bsz, t, d), F32),
            jax.ShapeDtypeStruct((bsz, t, d), BF16),
            jax.ShapeDtypeStruct((bsz, t, LANES), F32),
        ],
        compiler_params=_cparams(("parallel", "parallel")),
    )(o, zg, x, mods, w_out, g_norm, nw_ffn, w_router_hi_lo, w_router_hi)


def _excl_cumsum_rows(x01, tri_strict):
    blk = tri_strict.shape[0]
    carry = jnp.zeros((1, x01.shape[1]), F32)
    out = []
    for r0 in range(0, x01.shape[0], blk):
        xb = x01[r0:r0 + blk, :]
        out.append(_dot(tri_strict, xb.astype(BF16)) + carry)
        carry = carry + jnp.sum(xb, axis=0, keepdims=True)
    return out[0] if len(out) == 1 else jnp.concatenate(out, axis=0)


def _route_kernel(aff_ref, packed_ref, tri_ref, pos_ref, pos_rows_ref, aff_rows_ref, *, segments):
    tri_strict = tri_ref[...]
    per_row = LANES // N_EXPERTS

    def expert_major(a):
        return a.T[0:N_EXPERTS, :]

    for r0, n, cap in segments:
        aff = aff_ref[r0:r0 + n, :]
        aff_rows_ref[:, r0:r0 + n] = expert_major(aff)
        if cap == 0:
            pos_ref[r0:r0 + n, :] = jnp.full((n, LANES), -1.0, F32)
            pos_rows_ref[:, r0:r0 + n] = jnp.full((N_EXPERTS, n), -1.0, F32)
            continue
        packed = packed_ref[r0 // per_row:(r0 + n) // per_row, :]

        def as_float(pattern):
            return lax.bitcast_convert_type(pattern, F32)

        def refine(thr, low_bit, n_bits):
            counts = []
            for digit in range(1, 1 << n_bits):
                cand = thr | jnp.left_shift(jnp.int32(digit), low_bit)
                counts.append(jnp.sum(jnp.where(packed >= as_float(cand), 1.0, 0.0), axis=0, keepdims=True))
            cnt = jnp.concatenate(counts + [jnp.zeros((1, LANES), F32)], axis=0)
            shift = LANES // 2
            while shift >= N_EXPERTS:
                cnt = cnt + pltpu.roll(cnt, shift, 1)
                shift //= 2
            digit = jnp.sum(jnp.where(cnt >= cap, 1.0, 0.0), axis=0, keepdims=True).astype(jnp.int32)
            return thr | jnp.left_shift(digit, low_bit)

        thr = refine(jnp.zeros((1, LANES), jnp.int32), 28, 3)
        thr = lax.fori_loop(0, 7, lambda i, th: refine(th, 24 - 4 * i, 4), thr)
        beyond = as_float(thr + 1)
        above = jnp.where(aff >= beyond, 1.0, 0.0)
        tied = jnp.where((aff >= as_float(thr)) & (aff < beyond), 1.0, 0.0)
        need = cap - jnp.sum(above, axis=0, keepdims=True)
        tie_rank = _excl_cumsum_rows(tied, tri_strict)
        sel = above + tied * jnp.where(tie_rank < need, 1.0, 0.0)
        pos = jnp.where(sel > 0.0, _excl_cumsum_rows(sel, tri_strict), -1.0)
        pos_ref[r0:r0 + n, :] = pos
        pos_rows_ref[:, r0:r0 + n] = expert_major(pos)


def _route(aff, segments):
    bsz, t, _ = aff.shape
    r = np.arange(TOKEN_BLOCK)
    tri_strict = jnp.asarray((r[None, :] < r[:, None]).astype(np.float32), BF16)
    per_row = LANES // N_EXPERTS
    packed = aff[:, :, :N_EXPERTS].reshape(bsz, t // per_row, LANES)
    return pl.pallas_call(
        functools.partial(_route_kernel, segments=segments),
        grid=(bsz,),
        in_specs=[pl.BlockSpec((None, t, LANES), lambda b: (b, 0, 0)),
                  pl.BlockSpec((None, t // per_row, LANES), lambda b: (b, 0, 0)),
                  pl.BlockSpec(tri_strict.shape, lambda b: (0, 0))],
        out_specs=[pl.BlockSpec((None, t, LANES), lambda b: (b, 0, 0)),
                   pl.BlockSpec((None, N_EXPERTS, t), lambda b: (b, 0, 0)),
                   pl.BlockSpec((None, N_EXPERTS, t), lambda b: (b, 0, 0))],
        out_shape=[jax.ShapeDtypeStruct((bsz, t, LANES), F32),
                   jax.ShapeDtypeStruct((bsz, N_EXPERTS, t), F32),
                   jax.ShapeDtypeStruct((bsz, N_EXPERTS, t), F32)],
        compiler_params=_cparams(("parallel",)),
    )(aff, packed, tri_strict)


def _onehot(cond):
    return jnp.where(cond, 1.0, 0.0).astype(BF16)


def _gather_kernel(hb_ref, aff_ref, pos_ref, xe_ref, gate_ref, *, segments):
    e = pl.program_id(1)
    pos = pos_ref[pl.ds(e, 1), :]
    aff = aff_ref[pl.ds(e, 1), :]
    out0 = 0
    for r0, n, cap in segments:
        if cap == 0:
            continue
        slot = lax.broadcasted_iota(jnp.int32, (cap, n), 0).astype(F32)
        hit = pos[:, r0:r0 + n] == slot
        xe_ref[out0:out0 + cap, :] = _dot(_onehot(hit), hb_ref[r0:r0 + n, :]).astype(BF16)
        picked = jnp.where(hit, aff[:, r0:r0 + n], 0.0)
        folded = picked[:, 0:LANES]
        for c0 in range(LANES, n, LANES):
            folded = folded + picked[:, c0:c0 + LANES]
        gate = jnp.sum(folded, axis=1, keepdims=True)
        gate_ref[out0:out0 + cap, :] = jnp.broadcast_to(gate, (cap, LANES))
        out0 += cap


def _gather(hb, aff_rows, pos_rows, segments):
    bsz, t, d = hb.shape
    rows = sum(cap for _, _, cap in segments)
    slots = lambda b, e: (e, b, 0)
    per_expert = pl.BlockSpec((None, N_EXPERTS, t), lambda b, e: (b, 0, 0))
    return pl.pallas_call(
        functools.partial(_gather_kernel, segments=segments),
        grid=(bsz, N_EXPERTS),
        in_specs=[pl.BlockSpec((None, t, d), lambda b, e: (b, 0, 0)), per_expert, per_expert],
        out_specs=[pl.BlockSpec((None, rows, d), slots), pl.BlockSpec((None, rows, LANES), slots)],
        out_shape=[jax.ShapeDtypeStruct((N_EXPERTS, bsz * rows, d), BF16),
                   jax.ShapeDtypeStruct((N_EXPERTS, bsz * rows, LANES), F32)],
        compiler_params=_cparams(("parallel", "parallel")),
    )(hb, aff_rows, pos_rows)


def _ffn_kernel(xe_ref, gate_ref, wg_ref, wu_ref, wd_ref, y_ref, acc_ref, *, row_chunk):
    f = pl.program_id(1)
    last = pl.num_programs(1) - 1

    def step(mode):
        wg = wg_ref[...].astype(BF16)
        wu = wu_ref[...].astype(BF16)
        wd = wd_ref[...].astype(BF16)
        for r0 in range(0, xe_ref.shape[0], row_chunk):
            rows = slice(r0, r0 + row_chunk)
            xe = xe_ref[rows, :]
            hidden = (_silu(_dot(xe, wg)) * _dot(xe, wu)).astype(BF16)
            part = _dot(hidden, wd)
            if mode == "first":
                acc_ref[rows, :] = part
            elif mode == "middle":
                acc_ref[rows, :] += part
            else:
                y_ref[rows, :] = ((acc_ref[rows, :] + part) * gate_ref[rows, 0:1]).astype(y_ref.dtype)

    pl.when(f == 0)(lambda: step("first"))
    pl.when((f > 0) & (f < last))(lambda: step("middle"))
    pl.when(f == last)(lambda: step("last"))


def _ffn(xe, gate, w_gate, w_up, w_down, layer):
    n_rows, d = xe.shape[1], xe.shape[2]
    ff = w_gate.shape[-1]
    row_chunk = n_rows // 2 if n_rows % 2 == 0 and (n_rows // 2) % 16 == 0 else n_rows
    return pl.pallas_call(
        functools.partial(_ffn_kernel, row_chunk=row_chunk),
        grid=(N_EXPERTS, ff // FF_TILE),
        in_specs=[
            pl.BlockSpec((None, n_rows, d), lambda e, f: (e, 0, 0)),
            pl.BlockSpec((None, n_rows, LANES), lambda e, f: (e, 0, 0)),
            pl.BlockSpec((None, None, d, FF_TILE), lambda e, f: (layer, e, 0, f)),
            pl.BlockSpec((None, None, d, FF_TILE), lambda e, f: (layer, e, 0, f)),
            pl.BlockSpec((None, None, FF_TILE, d), lambda e, f: (layer, e, f, 0)),
        ],
        out_specs=pl.BlockSpec((None, n_rows, d), lambda e, f: (e, 0, 0)),
        out_shape=jax.ShapeDtypeStruct((N_EXPERTS, n_rows, d), BF16),
        scratch_shapes=[pltpu.VMEM((n_rows, d), F32)],
        compiler_params=_cparams(("parallel", "arbitrary")),
    )(xe, gate, w_gate, w_up, w_down)


def _scatter_kernel(x_ref, y_ref, pos_ref, mods_ref, nw_ref, *rest, segments, final, with_next):
    if with_next:
        next_mods_ref, next_nw_ref, out_ref, h_ref = rest
    else:
        out_ref, = rest
    d = y_ref.shape[-1]
    ctx_row = mods_ref.shape[0] - 1
    n_sub = x_ref.shape[0] // TOKEN_BLOCK
    on_last_block = pl.program_id(1) == pl.num_programs(1) - 1

    def block(seg_id, rows):
        cap = segments[seg_id][2]
        slot0 = sum(seg[2] for seg in segments[:seg_id])
        mod_row = ctx_row if seg_id == 1 else pl.program_id(0)
        gate = mods_ref[mod_row][5:6, :]
        y_all = y_ref[:, slot0:slot0 + cap, :].reshape(N_EXPERTS * cap, d)
        slot = lax.broadcasted_iota(jnp.int32, (TOKEN_BLOCK, cap), 1).astype(F32)
        pos = pos_ref[rows, :]
        onehots = jnp.concatenate([_onehot(pos[:, e:e + 1] == slot) for e in range(N_EXPERTS)], axis=1)
        x = x_ref[rows, :] + gate * _dot(onehots, y_all)
        if final:
            x = x * lax.rsqrt(jnp.mean(x * x, axis=-1, keepdims=True) + EPS) * nw_ref[...]
        out_ref[rows, :] = x
        if with_next:
            m = next_mods_ref[mod_row]
            h_ref[rows, :] = _rms_mod(x, next_nw_ref[...], m[0:1, :], m[1:2, :]).astype(BF16)

    for s in range(n_sub):
        rows = slice(s * TOKEN_BLOCK, (s + 1) * TOKEN_BLOCK)
        if segments[1][2] > 0 and s == n_sub - 1:
            pl.when(jnp.logical_not(on_last_block))(functools.partial(block, 0, rows))
            pl.when(on_last_block)(functools.partial(block, 1, rows))
        else:
            block(0, rows)


def _scatter(x, y, pos, mods, nw_final, segments, final, next_norm=None):
    bsz, t, d = x.shape
    n_slots = y.shape[1] // bsz
    out_rows = sum(n for _, n, cap in segments if cap > 0)
    tm = (3 if segments[1][2] > 0 else 2) * TOKEN_BLOCK
    assert out_rows % tm == 0
    tok = lambda b, i: (b, i, 0)
    const2, const3 = (lambda b, i: (0, 0)), (lambda b, i: (0, 0, 0))
    in_specs = [
        pl.BlockSpec((None, tm, d), tok),
        pl.BlockSpec((N_EXPERTS, n_slots, d), lambda b, i: (0, b, 0)),
        pl.BlockSpec((None, tm, LANES), tok),
        pl.BlockSpec(mods.shape, const3),
        pl.BlockSpec((1, d), const2),
    ]
    args = [x, y, pos, mods, nw_final]
    out_specs = [pl.BlockSpec((None, tm, d), tok)]
    out_shape = [jax.ShapeDtypeStruct((bsz, out_rows, d), F32)]
    if next_norm is not None:
        in_specs += [pl.BlockSpec(next_norm[0].shape, const3), pl.BlockSpec((1, d), const2)]
        args += list(next_norm)
        out_specs.append(pl.BlockSpec((None, tm, d), tok))
        out_shape.append(jax.ShapeDtypeStruct((bsz, out_rows, d), BF16))
    return pl.pallas_call(
        functools.partial(_scatter_kernel, segments=segments, final=final, with_next=next_norm is not None),
        grid=(bsz, out_rows // tm),
        in_specs=in_specs, out_specs=out_specs, out_shape=out_shape,
        compiler_params=_cparams(("parallel", "parallel")),
    )(*args)


def _moe(x, hb, aff, mods, w_gate, w_up, w_down, layer, nw_final, with_ctx, final, next_norm):
    bsz, t, d = x.shape
    seq = t - TOKEN_BLOCK
    cap_lat = EC_CAPACITY * seq // N_EXPERTS
    cap_ctx = EC_CAPACITY * TOKEN_BLOCK // N_EXPERTS if with_ctx else 0
    segments = ((0, seq, cap_lat), (seq, TOKEN_BLOCK, cap_ctx))
    pos, pos_rows, aff_rows = _route(aff, segments)
    xe, gate = _gather(hb, aff_rows, pos_rows, segments)
    y = _ffn(xe, gate, w_gate, w_up, w_down, layer)
    return _scatter(x, y, pos, mods, nw_final, segments, final, next_norm)


def _rope_tables(seq, n_ctx):
    half = RET_DK // 2
    t = jnp.arange(seq)
    inv = ROPE_THETA ** (-jnp.arange(0, half, 2, dtype=F32) / half)
    tabs = []
    for pos in ((t // GRID_W).astype(F32), (t % GRID_W).astype(F32)):
        ang = pos[:, None] * inv
        cos, sin = jnp.cos(ang), jnp.sin(ang)
        tabs.append(jnp.concatenate([cos, cos], axis=1))
        tabs.append(jnp.concatenate([-sin, sin], axis=1))
    ident = [jnp.ones, jnp.zeros, jnp.ones, jnp.zeros]
    return [jnp.concatenate([tab, fn((n_ctx, 2 * (half // 2)), F32)], axis=0) for tab, fn in zip(tabs, ident)]


def kernel(x, c, ctx, c_ctx, w_ada, b_ada, norm_mix, norm_ffn, ret_w_in, ret_w_out, hg_w_in, hg_g_norm,
           hg_lower_bounds, hg_w_out, moe_router, moe_w_gate, moe_w_up, moe_w_down, norm_final):
    bsz, seq, d = x.shape
    n_ctx = ctx.shape[1]
    depth = w_ada.shape[0]
    assert n_ctx == TOKEN_BLOCK and seq % TOKEN_BLOCK == 0 and d == RET_HEADS * RET_DK == HG_HEADS * HG_DK

    cc = jnp.concatenate([c, c_ctx[None, :], jnp.zeros((16 - bsz - 1, d), F32)], axis=0)
    mods_all = _ada(cc, w_ada, b_ada)[:, :bsz + 1].reshape(depth, bsz + 1, 6, d)
    t = seq + n_ctx
    rope = _rope_tables(seq, n_ctx)
    rope_specs = [pl.BlockSpec((t // 2, LANES), lambda j, b, i: (i, 0))] * 4
    w_router = jnp.pad(moe_router, ((0, 0), (0, 0), (0, LANES - N_EXPERTS)))

    xs, h = _prenorm(x, ctx, mods_all[0], norm_mix[0][None, :])
    for layer in range(depth):
        last = layer == depth - 1
        slot = layer // 2
        mods = mods_all[layer]
        if layer % 2 == 0:
            assert RET_HEADS * RET_DK == PROJ_COLS
            w = ret_w_in[slot]
            assert np.log2(RET_DK ** -0.5) % 1 == 0
            q, = _proj(h, w, 0, 1, _ep_rope, [BF16], rope, rope_specs)
            k, = _proj(h, w, 1, 1, functools.partial(_ep_rope, scale=RET_DK ** -0.5), [BF16], rope, rope_specs)
            v, = _proj(h, w, 2, 2, _ep_plain, [BF16])
            zg, = _proj(h, w, 4, 2, _ep_silu, [BF16])
            o = _retention(q, k, v)
            w_out, g_norm, n_heads, dv = ret_w_out[slot].astype(BF16), None, RET_HEADS, RET_DV
        else:
            assert HG_HEADS * HG_DK == PROJ_COLS
            w = hg_w_in[slot]
            lb_specs = [pl.BlockSpec((depth, PROJ_COLS), lambda j, b, i: (0, 0))]
            ep_gate = functools.partial(_ep_gate, layer=layer)
            q, = _proj(h, w, 0, 1, _ep_silu, [BF16])
            kf, gf = _proj(h, w, 1, 1, ep_gate, [F32, F32], (hg_lower_bounds[:, 0, :],), lb_specs)
            kb, gb = _proj(h, w, 2, 1, ep_gate, [F32, F32], (hg_lower_bounds[:, 1, :],), lb_specs)
            v, = _proj(h, w, 3, 1, _ep_plain, [BF16])
            zg, = _proj(h, w, 4, 1, _ep_silu, [BF16])
            o = _hgrn2(q, kf, kb, gf, gb, v)
            w_out, g_norm, n_heads, dv = hg_w_out[slot].astype(BF16), hg_g_norm[slot][None, :], HG_HEADS, HG_DV
        xs, hb, aff = _readout(o, zg, xs, mods, w_out, g_norm, norm_ffn[layer][None, :], w_router[layer],
                                       n_heads, dv)
        next_norm = None if last else (mods_all[layer + 1], norm_mix[layer + 1][None, :])
        res = _moe(xs, hb, aff, mods, moe_w_gate, moe_w_up, moe_w_down, layer, norm_final[None, :],
                   with_ctx=not last, final=last, next_norm=next_norm)
        xs, h = res if next_norm is not None else (res[0], None)
    return xs
```

```python
import functools

import numpy as np
import jax
import jax.numpy as jnp
from jax import lax
from jax.experimental import pallas as pl
from jax.experimental.pallas import tpu as pltpu

F32 = jnp.float32
BF16 = jnp.bfloat16
EPS = 1e-6
ROPE_THETA = 10000.0
GRID_W = 64

LANES = 128
TOKEN_BLOCK = 256
RET_HEADS, RET_DK, RET_DV, RET_CHUNK = 4, 256, 512, 256
HG_HEADS, HG_DK, HG_DV, HG_CHUNK, HG_SUB = 8, 128, 128, 128, 8
HG_WIDE = HG_CHUNK
HG_HEADS_PER_STEP = 2
HG_FAST_SPAN = 96.0
READOUT_SUBBLOCKS = 3
LOG2_E = 1.4426950408889634
N_EXPERTS = 16
EC_CAPACITY = 2
FF_TILE = 256
PROJ_COLS = 1024
VMEM_LIMIT = 56 * 1024 * 1024

HIGHEST = lax.Precision.HIGHEST


def _cparams(sem):
    return pltpu.CompilerParams(dimension_semantics=sem, vmem_limit_bytes=VMEM_LIMIT)


def _dot(a, b, **kw):
    return jnp.dot(a, b, preferred_element_type=F32, **kw)


def _dot_nt(a, b):
    return lax.dot_general(a, b, (((1,), (1,)), ((), ())), preferred_element_type=F32)


def _silu(x):
    return x * jax.nn.sigmoid(x)


def _rms_mod(x, nw, shift, scale):
    ms = jnp.mean(x * x, axis=-1, keepdims=True)
    h = x * lax.rsqrt(ms + EPS) * nw
    return h * (1.0 + scale) + shift


def _ada_kernel(c_ref, w_ref, b_ref, o_ref):
    o_ref[...] = _dot(_silu(c_ref[...]), w_ref[...], precision=HIGHEST) + b_ref[...]


def _ada(cc, w_ada, b_ada):
    depth, d, d6 = w_ada.shape
    rows = cc.shape[0]
    return pl.pallas_call(
        _ada_kernel,
        grid=(depth, d6 // d),
        in_specs=[
            pl.BlockSpec((rows, d), lambda l, j: (0, 0)),
            pl.BlockSpec((None, d, d), lambda l, j: (l, 0, j)),
            pl.BlockSpec((None, 1, d), lambda l, j: (l, 0, j)),
        ],
        out_specs=pl.BlockSpec((None, rows, d), lambda l, j: (l, 0, j)),
        out_shape=jax.ShapeDtypeStruct((depth, rows, d6), F32),
        compiler_params=_cparams(("parallel", "parallel")),
    )(cc, w_ada, b_ada.reshape(depth, 1, d6))


def _prenorm_kernel(x_ref, ctx_ref, mods_ref, nw_ref, xs_ref, h_ref):
    ctx_row = mods_ref.shape[0] - 1
    nw = nw_ref[...]
    seq = x_ref.shape[0]
    for src_ref, r0, row in ((x_ref, 0, pl.program_id(0)), (ctx_ref, seq, ctx_row)):
        m = mods_ref[row]
        shift, scale = m[0:1, :], m[1:2, :]

        def body(i, carry):
            src = pl.ds(pl.multiple_of(i * TOKEN_BLOCK, TOKEN_BLOCK), TOKEN_BLOCK)
            dst = pl.ds(pl.multiple_of(r0 + i * TOKEN_BLOCK, TOKEN_BLOCK), TOKEN_BLOCK)
            x = src_ref[src, :]
            xs_ref[dst, :] = x
            h_ref[dst, :] = _rms_mod(x, nw, shift, scale).astype(BF16)
            return carry

        lax.fori_loop(0, src_ref.shape[0] // TOKEN_BLOCK, body, 0)


def _prenorm(x, ctx, mods, nw):
    bsz, seq, d = x.shape
    n_ctx = ctx.shape[1]
    t = seq + n_ctx
    return pl.pallas_call(
        _prenorm_kernel,
        grid=(bsz,),
        in_specs=[pl.BlockSpec((None, seq, d), lambda b: (b, 0, 0)),
                  pl.BlockSpec((None, n_ctx, d), lambda b: (b, 0, 0)),
                  pl.BlockSpec(mods.shape, lambda b: (0, 0, 0)),
                  pl.BlockSpec((1, d), lambda b: (0, 0))],
        out_specs=[pl.BlockSpec((None, t, d), lambda b: (b, 0, 0)), pl.BlockSpec((None, t, d), lambda b: (b, 0, 0))],
        out_shape=[jax.ShapeDtypeStruct((bsz, t, d), F32), jax.ShapeDtypeStruct((bsz, t, d), BF16)],
        compiler_params=_cparams(("parallel",)),
    )(x, ctx, mods, nw)


def _proj_kernel(h_ref, w_ref, *rest, epilogue, n_extra):
    extras, outs = rest[:n_extra], rest[n_extra:]
    epilogue(_dot(h_ref[...], w_ref[...].astype(BF16)), extras, outs)


def _ep_plain(acc, extras, outs):
    outs[0][...] = acc.astype(outs[0].dtype)


def _ep_silu(acc, extras, outs):
    outs[0][...] = _silu(acc).astype(outs[0].dtype)


def _ep_rope(acc, extras, outs, *, scale=None):
    c_row, s_row, c_col, s_col = (r[...] for r in extras)
    for s in range(acc.shape[1] // LANES):
        slab = acc[:, s * LANES:(s + 1) * LANES]
        if scale is not None:
            slab = slab * scale
        cos, sin = (c_row, s_row) if s % 2 == 0 else (c_col, s_col)
        rot = slab * cos + pltpu.roll(slab, LANES // 2, 1) * sin
        outs[0][:, s * LANES:(s + 1) * LANES] = rot.astype(outs[0].dtype)


def _ep_gate(acc, extras, outs, *, layer):
    raw = extras[0][...]
    e = jnp.exp(raw - jnp.max(raw, axis=0, keepdims=True))
    sm = e / jnp.sum(e, axis=0, keepdims=True)
    lb = -sm[0:1, :]
    for l in range(layer + 1):
        lb = lb + sm[l:l + 1, :]
    f = lb + (1.0 - lb) * jax.nn.sigmoid(acc)
    outs[0][...] = 1.0 - f
    outs[1][...] = jnp.log(f)


def _proj(h, w, col_block0, ncol, epilogue, out_dtypes, extras=(), extra_specs=()):
    bsz, t, d = h.shape
    tm, tn = t // 2, PROJ_COLS
    in_specs = [
        pl.BlockSpec((None, tm, d), lambda j, b, i: (b, i, 0)),
        pl.BlockSpec((d, tn), lambda j, b, i: (0, col_block0 + j)),
    ] + list(extra_specs)
    out_specs = [pl.BlockSpec((None, tm, tn), lambda j, b, i: (b, i, j)) for _ in out_dtypes]
    out_shape = [jax.ShapeDtypeStruct((bsz, t, tn * ncol), dt) for dt in out_dtypes]
    return pl.pallas_call(
        functools.partial(_proj_kernel, epilogue=epilogue, n_extra=len(extras)),
        grid=(ncol, bsz, 2), in_specs=in_specs, out_specs=out_specs, out_shape=out_shape,
        compiler_params=_cparams(("parallel", "parallel", "parallel")),
    )(h, w, *extras)


def _ret_tables():
    c = RET_CHUNK
    j = np.arange(2 * RET_HEADS, dtype=np.float64)
    lg = np.log1p(-np.exp2(-5.0 - j / 2))
    lg_f, lg_b = lg[0::2], lg[1::2]
    i = np.arange(c, dtype=np.float64)
    diff = i[:, None] - i[None, :]
    mats = np.zeros((RET_HEADS, c, c), np.float64)
    tabs = np.zeros((RET_HEADS, c, LANES), np.float64)
    for h in range(RET_HEADS):
        mats[h] = np.where(diff >= 0, np.exp(diff * lg_f[h]), 0.0) + np.where(diff <= 0, np.exp(-diff * lg_b[h]), 0.0)
        tabs[h, :, 0] = np.exp((i + 1) * lg_f[h])
        tabs[h, :, 1] = np.exp((c - 1 - i) * lg_f[h])
        tabs[h, :, 2] = np.exp((c - i) * lg_b[h])
        tabs[h, :, 3] = np.exp(i * lg_b[h])
        tabs[h, :, 4] = np.exp(c * lg_f[h])
        tabs[h, :, 5] = np.exp(c * lg_b[h])
    return jnp.asarray(mats, F32), jnp.asarray(tabs, F32)


def _ret_kernel(q_ref, k_ref, v_ref, m_ref, tab_ref, out_ref, o_ref, vt_ref, sf_ref, sb_ref, *, n_chunks):
    c = RET_CHUNK
    decay = m_ref[...]
    tab = tab_ref[...]
    qf, kfs, qb, kbs = tab[:, 0:1], tab[:, 1:2], tab[:, 2:3], tab[:, 3:4]
    end_f, end_b = tab[0:1, 4:5], tab[0:1, 5:6]

    def rows(ci):
        return slice(ci * c, (ci + 1) * c)

    for ci in range(n_chunks):
        q, k, v = q_ref[rows(ci), :], k_ref[rows(ci), :], v_ref[rows(ci), :]
        scores = (_dot_nt(q, k) * decay).astype(BF16)
        o_ref[rows(ci), :] = _dot(scores, v)
        vt_ref[ci] = v.astype(F32).T.astype(BF16)

    ctx = n_chunks - 1
    order_f = [ctx] + list(range(ctx))
    order_b = [ctx] + list(range(ctx - 1, -1, -1))
    sf_ref[...] = jnp.zeros_like(sf_ref)
    sb_ref[...] = jnp.zeros_like(sb_ref)
    for step in range(n_chunks):
        for order, s_ref, qd, kd, end in ((order_f, sf_ref, qf, kfs, end_f), (order_b, sb_ref, qb, kbs, end_b)):
            ci = order[step]
            if step > 0:
                o_ref[rows(ci), :] += _dot_nt(q_ref[rows(ci), :], s_ref[...].astype(BF16)) * qd
            if step < n_chunks - 1:
                ks = (k_ref[rows(ci), :].astype(F32) * kd).astype(BF16)
                upd = _dot(vt_ref[ci], ks)
                s_ref[...] = upd if step == 0 else s_ref[...] * end + upd
    for ci in range(n_chunks):
        out_ref[rows(ci), :] = o_ref[rows(ci), :].astype(out_ref.dtype)


def _retention(q, k, v):
    bsz, t, _ = q.shape
    n_chunks = t // RET_CHUNK
    mats, tabs = _ret_tables()
    return pl.pallas_call(
        functools.partial(_ret_kernel, n_chunks=n_chunks),
        grid=(bsz, RET_HEADS),
        in_specs=[
            pl.BlockSpec((None, t, RET_DK), lambda b, h: (b, 0, h)),
            pl.BlockSpec((None, t, RET_DK), lambda b, h: (b, 0, h)),
            pl.BlockSpec((None, t, RET_DV), lambda b, h: (b, 0, h)),
            pl.BlockSpec((None, RET_CHUNK, RET_CHUNK), lambda b, h: (h, 0, 0)),
            pl.BlockSpec((None, RET_CHUNK, LANES), lambda b, h: (h, 0, 0)),
        ],
        out_specs=pl.BlockSpec((None, t, RET_DV), lambda b, h: (b, 0, h)),
        out_shape=jax.ShapeDtypeStruct((bsz, t, RET_HEADS * RET_DV), BF16),
        scratch_shapes=[
            pltpu.VMEM((t, RET_DV), F32),
            pltpu.VMEM((n_chunks, RET_DV, RET_CHUNK), BF16),
            pltpu.VMEM((RET_DV, RET_DK), F32),
            pltpu.VMEM((RET_DV, RET_DK), F32),
        ],
        compiler_params=_cparams(("parallel", "parallel")),
    )(q, k, v, mats, tabs)


def _hg_levels(block):
    return [HG_CHUNK >> (i + 1) for i in range(int(np.log2(HG_CHUNK // block)))]


def _hg_masks():
    c = HG_CHUNK
    r = np.arange(c)[:, None]
    col = np.arange(c)[None, :]
    tables = {}
    for lower in (True, False):
        for s in _hg_levels(HG_SUB):
            same = (r // (2 * s)) == (col // (2 * s))
            rh, ch = (r // s) % 2, (col // s) % 2
            tables["level", lower, s] = same & ((rh == 1) & (ch == 0) if lower else (rh == 0) & (ch == 1))
            far = (rh == 1) if lower else (rh == 0)
            tables["sign", lower, s] = np.broadcast_to(np.where(far, 1.0, -1.0), (c, c))
        for block in (HG_SUB, HG_WIDE):
            tables["diag", lower, block] = ((r // block) == (col // block)) & ((col <= r) if lower else (col >= r))
    index = {name: i for i, name in enumerate(tables)}
    return index, np.stack([np.asarray(t, np.float32) for t in tables.values()])


def _hg_tables():
    c, sub = HG_CHUNK, HG_SUB
    r = np.arange(c)[:, None]
    col = np.arange(c)[None, :]
    tri = np.stack([col <= r, col >= r]).astype(np.float32)
    place = np.stack([np.broadcast_to((np.arange(c) % sub) == j, (c, c)) for j in range(sub)]).astype(np.float32)
    return jnp.asarray(_hg_masks()[1]), jnp.asarray(tri, BF16), jnp.asarray(place, BF16)


def _split3(g):
    hi = g.astype(BF16)
    r1 = g - hi.astype(F32)
    mid = r1.astype(BF16)
    lo = (r1 - mid.astype(F32)).astype(BF16)
    return hi, mid, lo


def _hg_intra(q, k, b, masks, place, lower, block, pairwise=False):
    c, sub = HG_CHUNK, HG_SUB
    index = _hg_masks()[0]
    total = jnp.zeros((c, c), F32)
    for s in _hg_levels(block):
        pieces = []
        for m in range(0, c, 2 * s):
            r = m + s - 1 if lower else m + s
            pieces.append(jnp.broadcast_to(b[r:r + 1, :], (2 * s, b.shape[1])))
        ref = pieces[0] if len(pieces) == 1 else jnp.concatenate(pieces, axis=0)
        w = jnp.exp2((b - ref) * masks[index["sign", lower, s]])
        total = total + _dot_nt((q * w).astype(BF16), (k * w).astype(BF16)) * masks[index["level", lower, s]]
    diag_mask = masks[index["diag", lower, block]]
    if not pairwise:
        firsts = [u * block if lower else u * block + block - 1 for u in range(c // block)]
        ref = jnp.concatenate([jnp.broadcast_to(b[r:r + 1, :], (block, b.shape[1])) for r in firsts], axis=0)
        d = b - ref
        p = _dot_nt((q * jnp.exp2(d)).astype(BF16), (k * jnp.exp2(-d)).astype(BF16))
        return total + p * diag_mask, jnp.max(-d)
    assert block == sub
    shape3 = (c // sub, sub, q.shape[1])
    q3, b3 = q.reshape(shape3), b.reshape(shape3)
    a3 = (jnp.log2(jnp.maximum(k, 0.0)) - b).reshape(shape3)
    acc = jnp.zeros((c, c), F32)
    for jj in range(sub):
        aj = jnp.broadcast_to(a3[:, jj:jj + 1, :], shape3)
        p = q3 * jnp.exp2(jnp.minimum(b3 + aj, 0.0))
        acc = acc + _dot(p.reshape(c, q.shape[1]).astype(BF16), place[jj])
    return total + acc * diag_mask, None


def _hg_kernel(q_ref, kf_ref, kb_ref, gf_ref, gb_ref, v_ref, mask_ref, tri_ref, place_ref, out_ref,
               o_ref, bf_ref, bb_ref, vt_ref, s_ref, *, n_chunks):
    c = HG_CHUNK
    masks = [mask_ref[i] for i in range(mask_ref.shape[0])]
    place = [place_ref[i] for i in range(place_ref.shape[0])]
    tri_lo, tri_up = tri_ref[0], tri_ref[1]

    def cumsum(tri, g):
        hi, mid, lo = _split3(g)
        return (_dot(tri, hi) + _dot(tri, mid) + _dot(tri, lo)) * LOG2_E

    def chunk_rows(ci):
        return pl.ds(ci * c if isinstance(ci, int) else pl.multiple_of(ci * c, c), c)

    slabs = [slice(hh * HG_DK, (hh + 1) * HG_DK) for hh in range(HG_HEADS_PER_STEP)]

    def finish(ci, hh, scores, v):
        o_ref[chunk_rows(ci), slabs[hh]] = _dot(scores, v)
        vt_ref[hh, ci] = v.astype(F32).T.astype(BF16)

    first = chunk_rows(0)
    bf_ref[first, :] = jnp.concatenate([cumsum(tri_lo, gf_ref[first, sl]) for sl in slabs], axis=1)
    bb_ref[first, :] = jnp.concatenate([cumsum(tri_up, gb_ref[first, sl]) for sl in slabs], axis=1)
    s_ref[first, :] = jnp.zeros((c, s_ref.shape[1]), BF16)

    def intra(ci, carry):
        prev, nxt = jnp.maximum(ci - 1, 0), jnp.minimum(ci + 1, n_chunks - 1)
        rows, rows_p, rows_n = chunk_rows(ci), chunk_rows(prev), chunk_rows(nxt)
        loaded = []
        for sl in slabs:
            loaded.append((bf_ref[rows, sl], bb_ref[rows, sl], q_ref[rows, sl].astype(F32), kf_ref[rows, sl],
                           kb_ref[rows, sl], s_ref[rows_p, sl], v_ref[rows_p, sl], gf_ref[rows_n, sl],
                           gb_ref[rows_n, sl]))
        span = 0.0
        for hh, (bf, bb, q, kf, kb, s_prev, v_prev, gf_n, gb_n) in enumerate(loaded):
            finish(prev, hh, s_prev, v_prev)
            bf_ref[rows_n, slabs[hh]] = cumsum(tri_lo, gf_n)
            bb_ref[rows_n, slabs[hh]] = cumsum(tri_up, gb_n)
            lo, span_lo = _hg_intra(q, kf, bf, masks, place, True, HG_WIDE)
            up, span_up = _hg_intra(q, kb, bb, masks, place, False, HG_WIDE)
            s_ref[rows, slabs[hh]] = (lo + up).astype(BF16)
            span = jnp.maximum(span, jnp.maximum(span_lo, span_up))

        def redo(block, pairwise):
            worst = 0.0
            for sl in slabs:
                q = q_ref[rows, sl].astype(F32)
                lo, span_lo = _hg_intra(q, kf_ref[rows, sl], bf_ref[rows, sl], masks, place, True, block, pairwise)
                up, span_up = _hg_intra(q, kb_ref[rows, sl], bb_ref[rows, sl], masks, place, False, block, pairwise)
                s_ref[rows, sl] = (lo + up).astype(BF16)
                if not pairwise:
                    worst = jnp.maximum(worst, jnp.maximum(span_lo, span_up))
            return worst

        @pl.when(span > HG_FAST_SPAN)
        def _():
            @pl.when(redo(HG_SUB, False) > HG_FAST_SPAN)
            def _():
                redo(HG_SUB, True)

        return carry

    lax.fori_loop(0, n_chunks, intra, 0)
    last = chunk_rows(n_chunks - 1)
    for hh, sl in enumerate(slabs):
        finish(n_chunks - 1, hh, s_ref[last, sl], v_ref[last, sl])

    n_ctx = TOKEN_BLOCK // c
    n_lat = n_chunks - n_ctx

    assert all((n_lat + s if s < n_ctx else s - n_ctx) != n_chunks - 1 - s for s in range(n_chunks))

    def scan(step, carry):
        cf = jnp.where(step < n_ctx, n_lat + step, step - n_ctx)
        cb = n_chunks - 1 - step
        jobs = []
        for hh, sl in enumerate(slabs):
            for ci, k_ref, b_ref, end_row, st in ((cf, kf_ref, bf_ref, c - 1, carry[2 * hh]),
                                                  (cb, kb_ref, bb_ref, 0, carry[2 * hh + 1])):
                rows = chunk_rows(ci)
                jobs.append((rows, sl, q_ref[rows, sl].astype(F32), b_ref[rows, sl], k_ref[rows, sl], vt_ref[hh, ci],
                             o_ref[rows, sl], st, end_row))
        states, updates = [], []
        for rows, sl, q, b, k, vt, o_old, st, end_row in jobs:
            qs = (q * jnp.exp2(b)).astype(BF16)
            updates.append((rows, sl, o_old + _dot_nt(qs, st.astype(BF16))))
            b_end = b[end_row:end_row + 1, :]
            ks = (k * jnp.exp2(b_end - b)).astype(BF16)
            states.append(st * jnp.exp2(b_end) + _dot(vt, ks))
        for rows, sl, o_new in updates:
            o_ref[rows, sl] = o_new
        return tuple(states)

    zero = jnp.zeros((HG_DV, HG_DK), F32)
    lax.fori_loop(0, n_chunks, scan, (zero,) * (2 * HG_HEADS_PER_STEP))

    def emit(ci, carry):
        out_ref[chunk_rows(ci), :] = o_ref[chunk_rows(ci), :].astype(out_ref.dtype)
        return carry

    lax.fori_loop(0, n_chunks, emit, 0)


def _hgrn2(q, kf, kb, gf, gb, v):
    bsz, t, _ = q.shape
    n_chunks = t // HG_CHUNK
    hps = HG_HEADS_PER_STEP
    masks, tri, place = _hg_tables()
    head = lambda b, h: (b, 0, h)
    spec = pl.BlockSpec((None, t, hps * HG_DK), head)
    return pl.pallas_call(
        functools.partial(_hg_kernel, n_chunks=n_chunks),
        grid=(bsz, HG_HEADS // hps),
        in_specs=[spec, spec, spec, spec, spec, spec,
                  pl.BlockSpec(masks.shape, lambda b, h: (0, 0, 0)),
                  pl.BlockSpec(tri.shape, lambda b, h: (0, 0, 0)),
                  pl.BlockSpec(place.shape, lambda b, h: (0, 0, 0))],
        out_specs=pl.BlockSpec((None, t, hps * HG_DV), head),
        out_shape=jax.ShapeDtypeStruct((bsz, t, HG_HEADS * HG_DV), BF16),
        scratch_shapes=[
            pltpu.VMEM((t, hps * HG_DV), F32),
            pltpu.VMEM((t, hps * HG_DK), F32),
            pltpu.VMEM((t, hps * HG_DK), F32),
            pltpu.VMEM((hps, n_chunks, HG_DV, HG_CHUNK), BF16),
            pltpu.VMEM((t, hps * HG_CHUNK), BF16),
        ],
        compiler_params=_cparams(("parallel", "parallel")),
    )(q, kf, kb, gf, gb, v, masks, tri, place)


def _readout_kernel(o_ref, zg_ref, x_ref, mods_ref, wout_ref, gn_ref, nw_ref, wra_ref, wrb_ref,
                    xo_ref, hb_ref, aff_ref, *, n_heads, dv, affine):
    gn = gn_ref[...] if affine else None
    n_sub = x_ref.shape[0] // TOKEN_BLOCK
    ctx_row = mods_ref.shape[0] - 1
    on_last_block = pl.program_id(1) == pl.num_programs(1) - 1
    for s in range(n_sub):
        rows = slice(s * TOKEN_BLOCK, (s + 1) * TOKEN_BLOCK)
        mod_row = jnp.where(on_last_block, ctx_row, pl.program_id(0)) if s == n_sub - 1 else pl.program_id(0)
        mod = mods_ref[mod_row]
        o = o_ref[rows, :].astype(F32)
        parts = []
        for h in range(n_heads):
            oh = o[:, h * dv:(h + 1) * dv]
            ms = jnp.mean(oh * oh, axis=-1, keepdims=True)
            on = oh * lax.rsqrt(ms + EPS)
            parts.append(on if gn is None else on * gn)
        z = zg_ref[rows, :].astype(F32) * jnp.concatenate(parts, axis=1)
        x = x_ref[rows, :] + mod[2:3, :] * _dot(z.astype(BF16), wout_ref[...])
        xo_ref[rows, :] = x
        h2 = _rms_mod(x, nw_ref[...], mod[3:4, :], mod[4:5, :])
        h_hi = h2.astype(BF16)
        hb_ref[rows, :] = h_hi
        h_lo = (h2 - h_hi.astype(F32)).astype(BF16)
        both = _dot(h_hi, wra_ref[...])
        logits = both[:, :LANES] + both[:, LANES:] + _dot(h_lo, wrb_ref[...])
        lane = lax.broadcasted_iota(jnp.int32, logits.shape, 1)
        logits = jnp.where(lane < N_EXPERTS, logits, -1e30)
        e = jnp.exp(logits - jnp.max(logits, axis=-1, keepdims=True))
        aff_ref[rows, :] = e / jnp.sum(e, axis=-1, keepdims=True)


def _readout(o, zg, x, mods, w_out, g_norm, nw_ffn, w_router, n_heads, dv):
    affine = g_norm is not None
    if not affine:
        g_norm = jnp.ones((1, dv), F32)
    w_router_hi = w_router.astype(BF16)
    w_router_lo = (w_router - w_router_hi.astype(F32)).astype(BF16)
    w_router_hi_lo = jnp.concatenate([w_router_hi, w_router_lo], axis=1)
    bsz, t, d = x.shape
    hv = n_heads * dv
    tm = READOUT_SUBBLOCKS * TOKEN_BLOCK
    assert t % tm == 0
    tok = lambda b, i: (b, i, 0)
    full2 = lambda b, i: (0, 0)
    return pl.pallas_call(
        functools.partial(_readout_kernel, n_heads=n_heads, dv=dv, affine=affine),
        grid=(bsz, t // tm),
        in_specs=[
            pl.BlockSpec((None, tm, hv), tok),
            pl.BlockSpec((None, tm, hv), tok),
            pl.BlockSpec((None, tm, d), tok),
            pl.BlockSpec(mods.shape, lambda b, i: (0, 0, 0)),
            pl.BlockSpec((hv, d), full2),
            pl.BlockSpec((1, dv), full2),
            pl.BlockSpec((1, d), full2),
            pl.BlockSpec((d, 2 * LANES), full2),
            pl.BlockSpec((d, LANES), full2),
        ],
        out_specs=[
            pl.BlockSpec((None, tm, d), tok),
            pl.BlockSpec((None, tm, d), tok),
            pl.BlockSpec((None, tm, LANES), tok),
        ],
        out_shape=[
            jax.ShapeDtypeStruct((bsz, t, d), F32),
            jax.ShapeDtypeStruct((bsz, t, d), BF16),
            jax.ShapeDtypeStruct((bsz, t, LANES), F32),
        ],
        compiler_params=_cparams(("parallel", "parallel")),
    )(o, zg, x, mods, w_out, g_norm, nw_ffn, w_router_hi_lo, w_router_hi)


def _excl_cumsum_rows(x01, tri_strict):
    blk = tri_strict.shape[0]
    carry = jnp.zeros((1, x01.shape[1]), F32)
    out = []
    for r0 in range(0, x01.shape[0], blk):
        xb = x01[r0:r0 + blk, :]
        out.append(_dot(tri_strict, xb.astype(BF16)) + carry)
        carry = carry + jnp.sum(xb, axis=0, keepdims=True)
    return out[0] if len(out) == 1 else jnp.concatenate(out, axis=0)


def _route_kernel(aff_ref, packed_ref, tri_ref, pos_ref, pos_rows_ref, aff_rows_ref, *, segments):
    tri_strict = tri_ref[...]
    per_row = LANES // N_EXPERTS

    def expert_major(a):
        return a.T[0:N_EXPERTS, :]

    for r0, n, cap in segments:
        aff = aff_ref[r0:r0 + n, :]
        aff_rows_ref[:, r0:r0 + n] = expert_major(aff)
        if cap == 0:
            pos_ref[r0:r0 + n, :] = jnp.full((n, LANES), -1.0, F32)
            pos_rows_ref[:, r0:r0 + n] = jnp.full((N_EXPERTS, n), -1.0, F32)
            continue
        packed = packed_ref[r0 // per_row:(r0 + n) // per_row, :]

        def as_float(pattern):
            return lax.bitcast_convert_type(pattern, F32)

        def refine(thr, low_bit, n_bits):
            counts = []
            for digit in range(1, 1 << n_bits):
                cand = thr | jnp.left_shift(jnp.int32(digit), low_bit)
                counts.append(jnp.sum(jnp.where(packed >= as_float(cand), 1.0, 0.0), axis=0, keepdims=True))
            cnt = jnp.concatenate(counts + [jnp.zeros((1, LANES), F32)], axis=0)
            shift = LANES // 2
            while shift >= N_EXPERTS:
                cnt = cnt + pltpu.roll(cnt, shift, 1)
                shift //= 2
            digit = jnp.sum(jnp.where(cnt >= cap, 1.0, 0.0), axis=0, keepdims=True).astype(jnp.int32)
            return thr | jnp.left_shift(digit, low_bit)

        thr = refine(jnp.zeros((1, LANES), jnp.int32), 28, 3)
        thr = lax.fori_loop(0, 7, lambda i, th: refine(th, 24 - 4 * i, 4), thr)
        beyond = as_float(thr + 1)
        above = jnp.where(aff >= beyond, 1.0, 0.0)
        tied = jnp.where((aff >= as_float(thr)) & (aff < beyond), 1.0, 0.0)
        need = cap - jnp.sum(above, axis=0, keepdims=True)
        tie_rank = _excl_cumsum_rows(tied, tri_strict)
        sel = above + tied * jnp.where(tie_rank < need, 1.0, 0.0)
        pos = jnp.where(sel > 0.0, _excl_cumsum_rows(sel, tri_strict), -1.0)
        pos_ref[r0:r0 + n, :] = pos
        pos_rows_ref[:, r0:r0 + n] = expert_major(pos)


def _route(aff, segments):
    bsz, t, _ = aff.shape
    r = np.arange(TOKEN_BLOCK)
    tri_strict = jnp.asarray((r[None, :] < r[:, None]).astype(np.float32), BF16)
    per_row = LANES // N_EXPERTS
    packed = aff[:, :, :N_EXPERTS].reshape(bsz, t // per_row, LANES)
    return pl.pallas_call(
        functools.partial(_route_kernel, segments=segments),
        grid=(bsz,),
        in_specs=[pl.BlockSpec((None, t, LANES), lambda b: (b, 0, 0)),
                  pl.BlockSpec((None, t // per_row, LANES), lambda b: (b, 0, 0)),
                  pl.BlockSpec(tri_strict.shape, lambda b: (0, 0))],
        out_specs=[pl.BlockSpec((None, t, LANES), lambda b: (b, 0, 0)),
                   pl.BlockSpec((None, N_EXPERTS, t), lambda b: (b, 0, 0)),
                   pl.BlockSpec((None, N_EXPERTS, t), lambda b: (b, 0, 0))],
        out_shape=[jax.ShapeDtypeStruct((bsz, t, LANES), F32),
                   jax.ShapeDtypeStruct((bsz, N_EXPERTS, t), F32),
                   jax.ShapeDtypeStruct((bsz, N_EXPERTS, t), F32)],
        compiler_params=_cparams(("parallel",)),
    )(aff, packed, tri_strict)


def _onehot(cond):
    return jnp.where(cond, 1.0, 0.0).astype(BF16)


def _gather_kernel(hb_ref, aff_ref, pos_ref, xe_ref, gate_ref, *, segments):
    e = pl.program_id(1)
    pos = pos_ref[pl.ds(e, 1), :]
    aff = aff_ref[pl.ds(e, 1), :]
    out0 = 0
    for r0, n, cap in segments:
        if cap == 0:
            continue
        slot = lax.broadcasted_iota(jnp.int32, (cap, n), 0).astype(F32)
        hit = pos[:, r0:r0 + n] == slot
        xe_ref[out0:out0 + cap, :] = _dot(_onehot(hit), hb_ref[r0:r0 + n, :]).astype(BF16)
        picked = jnp.where(hit, aff[:, r0:r0 + n], 0.0)
        folded = picked[:, 0:LANES]
        for c0 in range(LANES, n, LANES):
            folded = folded + picked[:, c0:c0 + LANES]
        gate = jnp.sum(folded, axis=1, keepdims=True)
        gate_ref[out0:out0 + cap, :] = jnp.broadcast_to(gate, (cap, LANES))
        out0 += cap


def _gather(hb, aff_rows, pos_rows, segments):
    bsz, t, d = hb.shape
    rows = sum(cap for _, _, cap in segments)
    slots = lambda b, e: (e, b, 0)
    per_expert = pl.BlockSpec((None, N_EXPERTS, t), lambda b, e: (b, 0, 0))
    return pl.pallas_call(
        functools.partial(_gather_kernel, segments=segments),
        grid=(bsz, N_EXPERTS),
        in_specs=[pl.BlockSpec((None, t, d), lambda b, e: (b, 0, 0)), per_expert, per_expert],
        out_specs=[pl.BlockSpec((None, rows, d), slots), pl.BlockSpec((None, rows, LANES), slots)],
        out_shape=[jax.ShapeDtypeStruct((N_EXPERTS, bsz * rows, d), BF16),
                   jax.ShapeDtypeStruct((N_EXPERTS, bsz * rows, LANES), F32)],
        compiler_params=_cparams(("parallel", "parallel")),
    )(hb, aff_rows, pos_rows)


def _ffn_kernel(xe_ref, gate_ref, wg_ref, wu_ref, wd_ref, y_ref, acc_ref, *, row_chunk):
    f = pl.program_id(1)
    last = pl.num_programs(1) - 1

    def step(mode):
        wg = wg_ref[...].astype(BF16)
        wu = wu_ref[...].astype(BF16)
        wd = wd_ref[...].astype(BF16)
        for r0 in range(0, xe_ref.shape[0], row_chunk):
            rows = slice(r0, r0 + row_chunk)
            xe = xe_ref[rows, :]
            hidden = (_silu(_dot(xe, wg)) * _dot(xe, wu)).astype(BF16)
            part = _dot(hidden, wd)
            if mode == "first":
                acc_ref[rows, :] = part
            elif mode == "middle":
                acc_ref[rows, :] += part
            else:
                y_ref[rows, :] = ((acc_ref[rows, :] + part) * gate_ref[rows, 0:1]).astype(y_ref.dtype)

    pl.when(f == 0)(lambda: step("first"))
    pl.when((f > 0) & (f < last))(lambda: step("middle"))
    pl.when(f == last)(lambda: step("last"))


def _ffn(xe, gate, w_gate, w_up, w_down, layer):
    n_rows, d = xe.shape[1], xe.shape[2]
    ff = w_gate.shape[-1]
    row_chunk = n_rows // 2 if n_rows % 2 == 0 and (n_rows // 2) % 16 == 0 else n_rows
    return pl.pallas_call(
        functools.partial(_ffn_kernel, row_chunk=row_chunk),
        grid=(N_EXPERTS, ff // FF_TILE),
        in_specs=[
            pl.BlockSpec((None, n_rows, d), lambda e, f: (e, 0, 0)),
            pl.BlockSpec((None, n_rows, LANES), lambda e, f: (e, 0, 0)),
            pl.BlockSpec((None, None, d, FF_TILE), lambda e, f: (layer, e, 0, f)),
            pl.BlockSpec((None, None, d, FF_TILE), lambda e, f: (layer, e, 0, f)),
            pl.BlockSpec((None, None, FF_TILE, d), lambda e, f: (layer, e, f, 0)),
        ],
        out_specs=pl.BlockSpec((None, n_rows, d), lambda e, f: (e, 0, 0)),
        out_shape=jax.ShapeDtypeStruct((N_EXPERTS, n_rows, d), BF16),
        scratch_shapes=[pltpu.VMEM((n_rows, d), F32)],
        compiler_params=_cparams(("parallel", "arbitrary")),
    )(xe, gate, w_gate, w_up, w_down)


def _scatter_kernel(x_ref, y_ref, pos_ref, mods_ref, nw_ref, *rest, segments, final, with_next):
    if with_next:
        next_mods_ref, next_nw_ref, out_ref, h_ref = rest
    else:
        out_ref, = rest
    d = y_ref.shape[-1]
    ctx_row = mods_ref.shape[0] - 1
    n_sub = x_ref.shape[0] // TOKEN_BLOCK
    on_last_block = pl.program_id(1) == pl.num_programs(1) - 1

    def block(seg_id, rows):
        cap = segments[seg_id][2]
        slot0 = sum(seg[2] for seg in segments[:seg_id])
        mod_row = ctx_row if seg_id == 1 else pl.program_id(0)
        gate = mods_ref[mod_row][5:6, :]
        y_all = y_ref[:, slot0:slot0 + cap, :].reshape(N_EXPERTS * cap, d)
        slot = lax.broadcasted_iota(jnp.int32, (TOKEN_BLOCK, cap), 1).astype(F32)
        pos = pos_ref[rows, :]
        onehots = jnp.concatenate([_onehot(pos[:, e:e + 1] == slot) for e in range(N_EXPERTS)], axis=1)
        x = x_ref[rows, :] + gate * _dot(onehots, y_all)
        if final:
            x = x * lax.rsqrt(jnp.mean(x * x, axis=-1, keepdims=True) + EPS) * nw_ref[...]
        out_ref[rows, :] = x
        if with_next:
            m = next_mods_ref[mod_row]
            h_ref[rows, :] = _rms_mod(x, next_nw_ref[...], m[0:1, :], m[1:2, :]).astype(BF16)

    for s in range(n_sub):
        rows = slice(s * TOKEN_BLOCK, (s + 1) * TOKEN_BLOCK)
        if segments[1][2] > 0 and s == n_sub - 1:
            pl.when(jnp.logical_not(on_last_block))(functools.partial(block, 0, rows))
            pl.when(on_last_block)(functools.partial(block, 1, rows))
        else:
            block(0, rows)


def _scatter(x, y, pos, mods, nw_final, segments, final, next_norm=None):
    bsz, t, d = x.shape
    n_slots = y.shape[1] // bsz
    out_rows = sum(n for _, n, cap in segments if cap > 0)
    tm = (3 if segments[1][2] > 0 else 2) * TOKEN_BLOCK
    assert out_rows % tm == 0
    tok = lambda b, i: (b, i, 0)
    const2, const3 = (lambda b, i: (0, 0)), (lambda b, i: (0, 0, 0))
    in_specs = [
        pl.BlockSpec((None, tm, d), tok),
        pl.BlockSpec((N_EXPERTS, n_slots, d), lambda b, i: (0, b, 0)),
        pl.BlockSpec((None, tm, LANES), tok),
        pl.BlockSpec(mods.shape, const3),
        pl.BlockSpec((1, d), const2),
    ]
    args = [x, y, pos, mods, nw_final]
    out_specs = [pl.BlockSpec((None, tm, d), tok)]
    out_shape = [jax.ShapeDtypeStruct((bsz, out_rows, d), F32)]
    if next_norm is not None:
        in_specs += [pl.BlockSpec(next_norm[0].shape, const3), pl.BlockSpec((1, d), const2)]
        args += list(next_norm)
        out_specs.append(pl.BlockSpec((None, tm, d), tok))
        out_shape.append(jax.ShapeDtypeStruct((bsz, out_rows, d), BF16))
    return pl.pallas_call(
        functools.partial(_scatter_kernel, segments=segments, final=final, with_next=next_norm is not None),
        grid=(bsz, out_rows // tm),
        in_specs=in_specs, out_specs=out_specs, out_shape=out_shape,
        compiler_params=_cparams(("parallel", "parallel")),
    )(*args)


def _moe(x, hb, aff, mods, w_gate, w_up, w_down, layer, nw_final, with_ctx, final, next_norm):
    bsz, t, d = x.shape
    seq = t - TOKEN_BLOCK
    cap_lat = EC_CAPACITY * seq // N_EXPERTS
    cap_ctx = EC_CAPACITY * TOKEN_BLOCK // N_EXPERTS if with_ctx else 0
    segments = ((0, seq, cap_lat), (seq, TOKEN_BLOCK, cap_ctx))
    pos, pos_rows, aff_rows = _route(aff, segments)
    xe, gate = _gather(hb, aff_rows, pos_rows, segments)
    y = _ffn(xe, gate, w_gate, w_up, w_down, layer)
    return _scatter(x, y, pos, mods, nw_final, segments, final, next_norm)


def _rope_tables(seq, n_ctx):
    half = RET_DK // 2
    t = jnp.arange(seq)
    inv = ROPE_THETA ** (-jnp.arange(0, half, 2, dtype=F32) / half)
    tabs = []
    for pos in ((t // GRID_W).astype(F32), (t % GRID_W).astype(F32)):
        ang = pos[:, None] * inv
        cos, sin = jnp.cos(ang), jnp.sin(ang)
        tabs.append(jnp.concatenate([cos, cos], axis=1))
        tabs.append(jnp.concatenate([-sin, sin], axis=1))
    ident = [jnp.ones, jnp.zeros, jnp.ones, jnp.zeros]
    return [jnp.concatenate([tab, fn((n_ctx, 2 * (half // 2)), F32)], axis=0) for tab, fn in zip(tabs, ident)]


def kernel(x, c, ctx, c_ctx, w_ada, b_ada, norm_mix, norm_ffn, ret_w_in, ret_w_out, hg_w_in, hg_g_norm,
           hg_lower_bounds, hg_w_out, moe_router, moe_w_gate, moe_w_up, moe_w_down, norm_final):
    bsz, seq, d = x.shape
    n_ctx = ctx.shape[1]
    depth = w_ada.shape[0]
    assert n_ctx == TOKEN_BLOCK and seq % TOKEN_BLOCK == 0 and d == RET_HEADS * RET_DK == HG_HEADS * HG_DK

    cc = jnp.concatenate([c, c_ctx[None, :], jnp.zeros((16 - bsz - 1, d), F32)], axis=0)
    mods_all = _ada(cc, w_ada, b_ada)[:, :bsz + 1].reshape(depth, bsz + 1, 6, d)
    t = seq + n_ctx
    rope = _rope_tables(seq, n_ctx)
    rope_specs = [pl.BlockSpec((t // 2, LANES), lambda j, b, i: (i, 0))] * 4
    w_router = jnp.pad(moe_router, ((0, 0), (0, 0), (0, LANES - N_EXPERTS)))

    xs, h = _prenorm(x, ctx, mods_all[0], norm_mix[0][None, :])
    for layer in range(depth):
        last = layer == depth - 1
        slot = layer // 2
        mods = mods_all[layer]
        if layer % 2 == 0:
            assert RET_HEADS * RET_DK == PROJ_COLS
            w = ret_w_in[slot]
            assert np.log2(RET_DK ** -0.5) % 1 == 0
            q, = _proj(h, w, 0, 1, _ep_rope, [BF16], rope, rope_specs)
            k, = _proj(h, w, 1, 1, functools.partial(_ep_rope, scale=RET_DK ** -0.5), [BF16], rope, rope_specs)
            v, = _proj(h, w, 2, 2, _ep_plain, [BF16])
            zg, = _proj(h, w, 4, 2, _ep_silu, [BF16])
            o = _retention(q, k, v)
            w_out, g_norm, n_heads, dv = ret_w_out[slot].astype(BF16), None, RET_HEADS, RET_DV
        else:
            assert HG_HEADS * HG_DK == PROJ_COLS
            w = hg_w_in[slot]
            lb_specs = [pl.BlockSpec((depth, PROJ_COLS), lambda j, b, i: (0, 0))]
            ep_gate = functools.partial(_ep_gate, layer=layer)
            q, = _proj(h, w, 0, 1, _ep_silu, [BF16])
            kf, gf = _proj(h, w, 1, 1, ep_gate, [F32, F32], (hg_lower_bounds[:, 0, :],), lb_specs)
            kb, gb = _proj(h, w, 2, 1, ep_gate, [F32, F32], (hg_lower_bounds[:, 1, :],), lb_specs)
            v, = _proj(h, w, 3, 1, _ep_plain, [BF16])
            zg, = _proj(h, w, 4, 1, _ep_silu, [BF16])
            o = _hgrn2(q, kf, kb, gf, gb, v)
            w_out, g_norm, n_heads, dv = hg_w_out[slot].astype(BF16), hg_g_norm[slot][None, :], HG_HEADS, HG_DV
        xs, hb, aff = _readout(o, zg, xs, mods, w_out, g_norm, norm_ffn[layer][None, :], w_router[layer],
                                       n_heads, dv)
        next_norm = None if last else (mods_all[layer + 1], norm_mix[layer + 1][None, :])
        res = _moe(xs, hb, aff, mods, moe_w_gate, moe_w_up, moe_w_down, layer, norm_final[None, :],
                   with_ctx=not last, final=last, next_norm=next_norm)
        xs, h = res if next_norm is not None else (res[0], None)
    return xs
```

```python
import functools

import numpy as np
import jax
import jax.numpy as jnp
from jax import lax
from jax.experimental import pallas as pl
from jax.experimental.pallas import tpu as pltpu

F32 = jnp.float32
BF16 = jnp.bfloat16
EPS = 1e-6
ROPE_THETA = 10000.0
GRID_W = 64

LANES = 128
TOKEN_BLOCK = 256
RET_HEADS, RET_DK, RET_DV, RET_CHUNK = 4, 256, 512, 256
HG_HEADS, HG_DK, HG_DV, HG_CHUNK, HG_SUB = 8, 128, 128, 128, 8
HG_WIDE = HG_CHUNK
HG_HEADS_PER_STEP = 2
HG_FAST_SPAN = 96.0
READOUT_SUBBLOCKS = 3
LOG2_E = 1.4426950408889634
N_EXPERTS = 16
EC_CAPACITY = 2
FF_TILE = 256
PROJ_COLS = 1024
VMEM_LIMIT = 56 * 1024 * 1024

HIGHEST = lax.Precision.HIGHEST


def _cparams(sem):
    return pltpu.CompilerParams(dimension_semantics=sem, vmem_limit_bytes=VMEM_LIMIT)


def _dot(a, b, **kw):
    return jnp.dot(a, b, preferred_element_type=F32, **kw)


def _dot_nt(a, b):
    return lax.dot_general(a, b, (((1,), (1,)), ((), ())), preferred_element_type=F32)


def _silu(x):
    return x * jax.nn.sigmoid(x)


def _rms_mod(x, nw, shift, scale):
    ms = jnp.mean(x * x, axis=-1, keepdims=True)
    h = x * lax.rsqrt(ms + EPS) * nw
    return h * (1.0 + scale) + shift


def _ada_kernel(c_ref, w_ref, b_ref, o_ref):
    o_ref[...] = _dot(_silu(c_ref[...]), w_ref[...], precision=HIGHEST) + b_ref[...]


def _ada(cc, w_ada, b_ada):
    depth, d, d6 = w_ada.shape
    rows = cc.shape[0]
    return pl.pallas_call(
        _ada_kernel,
        grid=(depth, d6 // d),
        in_specs=[
            pl.BlockSpec((rows, d), lambda l, j: (0, 0)),
            pl.BlockSpec((None, d, d), lambda l, j: (l, 0, j)),
            pl.BlockSpec((None, 1, d), lambda l, j: (l, 0, j)),
        ],
        out_specs=pl.BlockSpec((None, rows, d), lambda l, j: (l, 0, j)),
        out_shape=jax.ShapeDtypeStruct((depth, rows, d6), F32),
        compiler_params=_cparams(("parallel", "parallel")),
    )(cc, w_ada, b_ada.reshape(depth, 1, d6))


def _prenorm_kernel(x_ref, ctx_ref, mods_ref, nw_ref, xs_ref, h_ref):
    ctx_row = mods_ref.shape[0] - 1
    nw = nw_ref[...]
    seq = x_ref.shape[0]
    for src_ref, r0, row in ((x_ref, 0, pl.program_id(0)), (ctx_ref, seq, ctx_row)):
        m = mods_ref[row]
        shift, scale = m[0:1, :], m[1:2, :]

        def body(i, carry):
            src = pl.ds(pl.multiple_of(i * TOKEN_BLOCK, TOKEN_BLOCK), TOKEN_BLOCK)
            dst = pl.ds(pl.multiple_of(r0 + i * TOKEN_BLOCK, TOKEN_BLOCK), TOKEN_BLOCK)
            x = src_ref[src, :]
            xs_ref[dst, :] = x
            h_ref[dst, :] = _rms_mod(x, nw, shift, scale).astype(BF16)
            return carry

        lax.fori_loop(0, src_ref.shape[0] // TOKEN_BLOCK, body, 0)


def _prenorm(x, ctx, mods, nw):
    bsz, seq, d = x.shape
    n_ctx = ctx.shape[1]
    t = seq + n_ctx
    return pl.pallas_call(
        _prenorm_kernel,
        grid=(bsz,),
        in_specs=[pl.BlockSpec((None, seq, d), lambda b: (b, 0, 0)),
                  pl.BlockSpec((None, n_ctx, d), lambda b: (b, 0, 0)),
                  pl.BlockSpec(mods.shape, lambda b: (0, 0, 0)),
                  pl.BlockSpec((1, d), lambda b: (0, 0))],
        out_specs=[pl.BlockSpec((None, t, d), lambda b: (b, 0, 0)), pl.BlockSpec((None, t, d), lambda b: (b, 0, 0))],
        out_shape=[jax.ShapeDtypeStruct((bsz, t, d), F32), jax.ShapeDtypeStruct((bsz, t, d), BF16)],
        compiler_params=_cparams(("parallel",)),
    )(x, ctx, mods, nw)


def _proj_kernel(h_ref, w_ref, *rest, epilogue, n_extra):
    extras, outs = rest[:n_extra], rest[n_extra:]
    epilogue(_dot(h_ref[...], w_ref[...].astype(BF16)), extras, outs)


def _ep_plain(acc, extras, outs):
    outs[0][...] = acc.astype(outs[0].dtype)


def _ep_silu(acc, extras, outs):
    outs[0][...] = _silu(acc).astype(outs[0].dtype)


def _ep_rope(acc, extras, outs, *, scale=None):
    c_row, s_row, c_col, s_col = (r[...] for r in extras)
    for s in range(acc.shape[1] // LANES):
        slab = acc[:, s * LANES:(s + 1) * LANES]
        if scale is not None:
            slab = slab * scale
        cos, sin = (c_row, s_row) if s % 2 == 0 else (c_col, s_col)
        rot = slab * cos + pltpu.roll(slab, LANES // 2, 1) * sin
        outs[0][:, s * LANES:(s + 1) * LANES] = rot.astype(outs[0].dtype)


def _ep_gate(acc, extras, outs, *, layer):
    raw = extras[0][...]
    e = jnp.exp(raw - jnp.max(raw, axis=0, keepdims=True))
    sm = e / jnp.sum(e, axis=0, keepdims=True)
    lb = -sm[0:1, :]
    for l in range(layer + 1):
        lb = lb + sm[l:l + 1, :]
    f = lb + (1.0 - lb) * jax.nn.sigmoid(acc)
    outs[0][...] = 1.0 - f
    outs[1][...] = jnp.log(f)


def _proj(h, w, col_block0, ncol, epilogue, out_dtypes, extras=(), extra_specs=()):
    bsz, t, d = h.shape
    tm, tn = t // 2, PROJ_COLS
    in_specs = [
        pl.BlockSpec((None, tm, d), lambda j, b, i: (b, i, 0)),
        pl.BlockSpec((d, tn), lambda j, b, i: (0, col_block0 + j)),
    ] + list(extra_specs)
    out_specs = [pl.BlockSpec((None, tm, tn), lambda j, b, i: (b, i, j)) for _ in out_dtypes]
    out_shape = [jax.ShapeDtypeStruct((bsz, t, tn * ncol), dt) for dt in out_dtypes]
    return pl.pallas_call(
        functools.partial(_proj_kernel, epilogue=epilogue, n_extra=len(extras)),
        grid=(ncol, bsz, 2), in_specs=in_specs, out_specs=out_specs, out_shape=out_shape,
        compiler_params=_cparams(("parallel", "parallel", "parallel")),
    )(h, w, *extras)


def _ret_tables():
    c = RET_CHUNK
    j = np.arange(2 * RET_HEADS, dtype=np.float64)
    lg = np.log1p(-np.exp2(-5.0 - j / 2))
    lg_f, lg_b = lg[0::2], lg[1::2]
    i = np.arange(c, dtype=np.float64)
    diff = i[:, None] - i[None, :]
    mats = np.zeros((RET_HEADS, c, c), np.float64)
    tabs = np.zeros((RET_HEADS, c, LANES), np.float64)
    for h in range(RET_HEADS):
        mats[h] = np.where(diff >= 0, np.exp(diff * lg_f[h]), 0.0) + np.where(diff <= 0, np.exp(-diff * lg_b[h]), 0.0)
        tabs[h, :, 0] = np.exp((i + 1) * lg_f[h])
        tabs[h, :, 1] = np.exp((c - 1 - i) * lg_f[h])
        tabs[h, :, 2] = np.exp((c - i) * lg_b[h])
        tabs[h, :, 3] = np.exp(i * lg_b[h])
        tabs[h, :, 4] = np.exp(c * lg_f[h])
        tabs[h, :, 5] = np.exp(c * lg_b[h])
    return jnp.asarray(mats, F32), jnp.asarray(tabs, F32)


def _ret_kernel(q_ref, k_ref, v_ref, m_ref, tab_ref, out_ref, o_ref, vt_ref, sf_ref, sb_ref, *, n_chunks):
    c = RET_CHUNK
    decay = m_ref[...]
    tab = tab_ref[...]
    qf, kfs, qb, kbs = tab[:, 0:1], tab[:, 1:2], tab[:, 2:3], tab[:, 3:4]
    end_f, end_b = tab[0:1, 4:5], tab[0:1, 5:6]

    def rows(ci):
        return slice(ci * c, (ci + 1) * c)

    for ci in range(n_chunks):
        q, k, v = q_ref[rows(ci), :], k_ref[rows(ci), :], v_ref[rows(ci), :]
        scores = (_dot_nt(q, k) * decay).astype(BF16)
        o_ref[rows(ci), :] = _dot(scores, v)
        vt_ref[ci] = v.astype(F32).T.astype(BF16)

    ctx = n_chunks - 1
    order_f = [ctx] + list(range(ctx))
    order_b = [ctx] + list(range(ctx - 1, -1, -1))
    sf_ref[...] = jnp.zeros_like(sf_ref)
    sb_ref[...] = jnp.zeros_like(sb_ref)
    for step in range(n_chunks):
        for order, s_ref, qd, kd, end in ((order_f, sf_ref, qf, kfs, end_f), (order_b, sb_ref, qb, kbs, end_b)):
            ci = order[step]
            if step > 0:
                o_ref[rows(ci), :] += _dot_nt(q_ref[rows(ci), :], s_ref[...].astype(BF16)) * qd
            if step < n_chunks - 1:
                ks = (k_ref[rows(ci), :].astype(F32) * kd).astype(BF16)
                upd = _dot(vt_ref[ci], ks)
                s_ref[...] = upd if step == 0 else s_ref[...] * end + upd
    for ci in range(n_chunks):
        out_ref[rows(ci), :] = o_ref[rows(ci), :].astype(out_ref.dtype)


def _retention(q, k, v):
    bsz, t, _ = q.shape
    n_chunks = t // RET_CHUNK
    mats, tabs = _ret_tables()
    return pl.pallas_call(
        functools.partial(_ret_kernel, n_chunks=n_chunks),
        grid=(bsz, RET_HEADS),
        in_specs=[
            pl.BlockSpec((None, t, RET_DK), lambda b, h: (b, 0, h)),
            pl.BlockSpec((None, t, RET_DK), lambda b, h: (b, 0, h)),
            pl.BlockSpec((None, t, RET_DV), lambda b, h: (b, 0, h)),
            pl.BlockSpec((None, RET_CHUNK, RET_CHUNK), lambda b, h: (h, 0, 0)),
            pl.BlockSpec((None, RET_CHUNK, LANES), lambda b, h: (h, 0, 0)),
        ],
        out_specs=pl.BlockSpec((None, t, RET_DV), lambda b, h: (b, 0, h)),
        out_shape=jax.ShapeDtypeStruct((bsz, t, RET_HEADS * RET_DV), BF16),
        scratch_shapes=[
            pltpu.VMEM((t, RET_DV), F32),
            pltpu.VMEM((n_chunks, RET_DV, RET_CHUNK), BF16),
            pltpu.VMEM((RET_DV, RET_DK), F32),
            pltpu.VMEM((RET_DV, RET_DK), F32),
        ],
        compiler_params=_cparams(("parallel", "parallel")),
    )(q, k, v, mats, tabs)


def _hg_levels(block):
    return [HG_CHUNK >> (i + 1) for i in range(int(np.log2(HG_CHUNK // block)))]


def _hg_masks():
    c = HG_CHUNK
    r = np.arange(c)[:, None]
    col = np.arange(c)[None, :]
    tables = {}
    for lower in (True, False):
        for s in _hg_levels(HG_SUB):
            same = (r // (2 * s)) == (col // (2 * s))
            rh, ch = (r // s) % 2, (col // s) % 2
            tables["level", lower, s] = same & ((rh == 1) & (ch == 0) if lower else (rh == 0) & (ch == 1))
            far = (rh == 1) if lower else (rh == 0)
            tables["sign", lower, s] = np.broadcast_to(np.where(far, 1.0, -1.0), (c, c))
        for block in (HG_SUB, HG_WIDE):
            tables["diag", lower, block] = ((r // block) == (col // block)) & ((col <= r) if lower else (col >= r))
    index = {name: i for i, name in enumerate(tables)}
    return index, np.stack([np.asarray(t, np.float32) for t in tables.values()])


def _hg_tables():
    c, sub = HG_CHUNK, HG_SUB
    r = np.arange(c)[:, None]
    col = np.arange(c)[None, :]
    tri = np.stack([col <= r, col >= r]).astype(np.float32)
    place = np.stack([np.broadcast_to((np.arange(c) % sub) == j, (c, c)) for j in range(sub)]).astype(np.float32)
    return jnp.asarray(_hg_masks()[1]), jnp.asarray(tri, BF16), jnp.asarray(place, BF16)


def _split3(g):
    hi = g.astype(BF16)
    r1 = g - hi.astype(F32)
    mid = r1.astype(BF16)
    lo = (r1 - mid.astype(F32)).astype(BF16)
    return hi, mid, lo


def _hg_intra(q, k, b, masks, place, lower, block, pairwise=False):
    c, sub = HG_CHUNK, HG_SUB
    index = _hg_masks()[0]
    total = jnp.zeros((c, c), F32)
    for s in _hg_levels(block):
        pieces = []
        for m in range(0, c, 2 * s):
            r = m + s - 1 if lower else m + s
            pieces.append(jnp.broadcast_to(b[r:r + 1, :], (2 * s, b.shape[1])))
        ref = pieces[0] if len(pieces) == 1 else jnp.concatenate(pieces, axis=0)
        w = jnp.exp2((b - ref) * masks[index["sign", lower, s]])
        total = total + _dot_nt((q * w).astype(BF16), (k * w).astype(BF16)) * masks[index["level", lower, s]]
    diag_mask = masks[index["diag", lower, block]]
    if not pairwise:
        firsts = [u * block if lower else u * block + block - 1 for u in range(c // block)]
        ref = jnp.concatenate([jnp.broadcast_to(b[r:r + 1, :], (block, b.shape[1])) for r in firsts], axis=0)
        d = b - ref
        p = _dot_nt((q * jnp.exp2(d)).astype(BF16), (k * jnp.exp2(-d)).astype(BF16))
        return total + p * diag_mask, jnp.max(-d)
    assert block == sub
    shape3 = (c // sub, sub, q.shape[1])
    q3, b3 = q.reshape(shape3), b.reshape(shape3)
    a3 = (jnp.log2(jnp.maximum(k, 0.0)) - b).reshape(shape3)
    acc = jnp.zeros((c, c), F32)
    for jj in range(sub):
        aj = jnp.broadcast_to(a3[:, jj:jj + 1, :], shape3)
        p = q3 * jnp.exp2(jnp.minimum(b3 + aj, 0.0))
        acc = acc + _dot(p.reshape(c, q.shape[1]).astype(BF16), place[jj])
    return total + acc * diag_mask, None


def _hg_kernel(q_ref, kf_ref, kb_ref, gf_ref, gb_ref, v_ref, mask_ref, tri_ref, place_ref, out_ref,
               o_ref, bf_ref, bb_ref, vt_ref, s_ref, *, n_chunks):
    c = HG_CHUNK
    masks = [mask_ref[i] for i in range(mask_ref.shape[0])]
    place = [place_ref[i] for i in range(place_ref.shape[0])]
    tri_lo, tri_up = tri_ref[0], tri_ref[1]

    def cumsum(tri, g):
        hi, mid, lo = _split3(g)
        return (_dot(tri, hi) + _dot(tri, mid) + _dot(tri, lo)) * LOG2_E

    def chunk_rows(ci):
        return pl.ds(ci * c if isinstance(ci, int) else pl.multiple_of(ci * c, c), c)

    slabs = [slice(hh * HG_DK, (hh + 1) * HG_DK) for hh in range(HG_HEADS_PER_STEP)]

    def finish(ci, hh, scores, v):
        o_ref[chunk_rows(ci), slabs[hh]] = _dot(scores, v)
        vt_ref[hh, ci] = v.astype(F32).T.astype(BF16)

    first = chunk_rows(0)
    bf_ref[first, :] = jnp.concatenate([cumsum(tri_lo, gf_ref[first, sl]) for sl in slabs], axis=1)
    bb_ref[first, :] = jnp.concatenate([cumsum(tri_up, gb_ref[first, sl]) for sl in slabs], axis=1)
    s_ref[first, :] = jnp.zeros((c, s_ref.shape[1]), BF16)

    def intra(ci, carry):
        prev, nxt = jnp.maximum(ci - 1, 0), jnp.minimum(ci + 1, n_chunks - 1)
        rows, rows_p, rows_n = chunk_rows(ci), chunk_rows(prev), chunk_rows(nxt)
        loaded = []
        for sl in slabs:
            loaded.append((bf_ref[rows, sl], bb_ref[rows, sl], q_ref[rows, sl].astype(F32), kf_ref[rows, sl],
                           kb_ref[rows, sl], s_ref[rows_p, sl], v_ref[rows_p, sl], gf_ref[rows_n, sl],
                           gb_ref[rows_n, sl]))
        span = 0.0
        for hh, (bf, bb, q, kf, kb, s_prev, v_prev, gf_n, gb_n) in enumerate(loaded):
            finish(prev, hh, s_prev, v_prev)
            bf_ref[rows_n, slabs[hh]] = cumsum(tri_lo, gf_n)
            bb_ref[rows_n, slabs[hh]] = cumsum(tri_up, gb_n)
            lo, span_lo = _hg_intra(q, kf, bf, masks, place, True, HG_WIDE)
            up, span_up = _hg_intra(q, kb, bb, masks, place, False, HG_WIDE)
            s_ref[rows, slabs[hh]] = (lo + up).astype(BF16)
            span = jnp.maximum(span, jnp.maximum(span_lo, span_up))

        def redo(block, pairwise):
            worst = 0.0
            for sl in slabs:
                q = q_ref[rows, sl].astype(F32)
                lo, span_lo = _hg_intra(q, kf_ref[rows, sl], bf_ref[rows, sl], masks, place, True, block, pairwise)
                up, span_up = _hg_intra(q, kb_ref[rows, sl], bb_ref[rows, sl], masks, place, False, block, pairwise)
                s_ref[rows, sl] = (lo + up).astype(BF16)
                if not pairwise:
                    worst = jnp.maximum(worst, jnp.maximum(span_lo, span_up))
            return worst

        @pl.when(span > HG_FAST_SPAN)
        def _():
            @pl.when(redo(HG_SUB, False) > HG_FAST_SPAN)
            def _():
                redo(HG_SUB, True)

        return carry

    lax.fori_loop(0, n_chunks, intra, 0)
    last = chunk_rows(n_chunks - 1)
    for hh, sl in enumerate(slabs):
        finish(n_chunks - 1, hh, s_ref[last, sl], v_ref[last, sl])

    n_ctx = TOKEN_BLOCK // c
    n_lat = n_chunks - n_ctx

    assert all((n_lat + s if s < n_ctx else s - n_ctx) != n_chunks - 1 - s for s in range(n_chunks))

    def scan(step, carry):
        cf = jnp.where(step < n_ctx, n_lat + step, step - n_ctx)
        cb = n_chunks - 1 - step
        jobs = []
        for hh, sl in enumerate(slabs):
            for ci, k_ref, b_ref, end_row, st in ((cf, kf_ref, bf_ref, c - 1, carry[2 * hh]),
                                                  (cb, kb_ref, bb_ref, 0, carry[2 * hh + 1])):
                rows = chunk_rows(ci)
                jobs.append((rows, sl, q_ref[rows, sl].astype(F32), b_ref[rows, sl], k_ref[rows, sl], vt_ref[hh, ci],
                             o_ref[rows, sl], st, end_row))
        states, updates = [], []
        for rows, sl, q, b, k, vt, o_old, st, end_row in jobs:
            qs = (q * jnp.exp2(b)).astype(BF16)
            updates.append((rows, sl, o_old + _dot_nt(qs, st.astype(BF16))))
            b_end = b[end_row:end_row + 1, :]
            ks = (k * jnp.exp2(b_end - b)).astype(BF16)
            states.append(st * jnp.exp2(b_end) + _dot(vt, ks))
        for rows, sl, o_new in updates:
            o_ref[rows, sl] = o_new
        return tuple(states)

    zero = jnp.zeros((HG_DV, HG_DK), F32)
    lax.fori_loop(0, n_chunks, scan, (zero,) * (2 * HG_HEADS_PER_STEP))

    def emit(ci, carry):
        out_ref[chunk_rows(ci), :] = o_ref[chunk_rows(ci), :].astype(out_ref.dtype)
        return carry

    lax.fori_loop(0, n_chunks, emit, 0)


def _hgrn2(q, kf, kb, gf, gb, v):
    bsz, t, _ = q.shape
    n_chunks = t // HG_CHUNK
    hps = HG_HEADS_PER_STEP
    masks, tri, place = _hg_tables()
    head = lambda b, h: (b, 0, h)
    spec = pl.BlockSpec((None, t, hps * HG_DK), head)
    return pl.pallas_call(
        functools.partial(_hg_kernel, n_chunks=n_chunks),
        grid=(bsz, HG_HEADS // hps),
        in_specs=[spec, spec, spec, spec, spec, spec,
                  pl.BlockSpec(masks.shape, lambda b, h: (0, 0, 0)),
                  pl.BlockSpec(tri.shape, lambda b, h: (0, 0, 0)),
                  pl.BlockSpec(place.shape, lambda b, h: (0, 0, 0))],
        out_specs=pl.BlockSpec((None, t, hps * HG_DV), head),
        out_shape=jax.ShapeDtypeStruct((bsz, t, HG_HEADS * HG_DV), BF16),
        scratch_shapes=[
            pltpu.VMEM((t, hps * HG_DV), F32),
            pltpu.VMEM((t, hps * HG_DK), F32),
            pltpu.VMEM((t, hps * HG_DK), F32),
            pltpu.VMEM((hps, n_chunks, HG_DV, HG_CHUNK), BF16),
            pltpu.VMEM((t, hps * HG_CHUNK), BF16),
        ],
        compiler_params=_cparams(("parallel", "parallel")),
    )(q, kf, kb, gf, gb, v, masks, tri, place)


def _readout_kernel(o_ref, zg_ref, x_ref, mods_ref, wout_ref, gn_ref, nw_ref, wra_ref, wrb_ref,
                    xo_ref, hb_ref, aff_ref, *, n_heads, dv, affine):
    gn = gn_ref[...] if affine else None
    n_sub = x_ref.shape[0] // TOKEN_BLOCK
    ctx_row = mods_ref.shape[0] - 1
    on_last_block = pl.program_id(1) == pl.num_programs(1) - 1
    for s in range(n_sub):
        rows = slice(s * TOKEN_BLOCK, (s + 1) * TOKEN_BLOCK)
        mod_row = jnp.where(on_last_block, ctx_row, pl.program_id(0)) if s == n_sub - 1 else pl.program_id(0)
        mod = mods_ref[mod_row]
        o = o_ref[rows, :].astype(F32)
        parts = []
        for h in range(n_heads):
            oh = o[:, h * dv:(h + 1) * dv]
            ms = jnp.mean(oh * oh, axis=-1, keepdims=True)
            on = oh * lax.rsqrt(ms + EPS)
            parts.append(on if gn is None else on * gn)
        z = zg_ref[rows, :].astype(F32) * jnp.concatenate(parts, axis=1)
        x = x_ref[rows, :] + mod[2:3, :] * _dot(z.astype(BF16), wout_ref[...])
        xo_ref[rows, :] = x
        h2 = _rms_mod(x, nw_ref[...], mod[3:4, :], mod[4:5, :])
        h_hi = h2.astype(BF16)
        hb_ref[rows, :] = h_hi
        h_lo = (h2 - h_hi.astype(F32)).astype(BF16)
        both = _dot(h_hi, wra_ref[...])
        logits = both[:, :LANES] + both[:, LANES:] + _dot(h_lo, wrb_ref[...])
        lane = lax.broadcasted_iota(jnp.int32, logits.shape, 1)
        logits = jnp.where(lane < N_EXPERTS, logits, -1e30)
        e = jnp.exp(logits - jnp.max(logits, axis=-1, keepdims=True))
        aff_ref[rows, :] = e / jnp.sum(e, axis=-1, keepdims=True)


def _readout(o, zg, x, mods, w_out, g_norm, nw_ffn, w_router, n_heads, dv):
    affine = g_norm is not None
    if not affine:
        g_norm = jnp.ones((1, dv), F32)
    w_router_hi = w_router.astype(BF16)
    w_router_lo = (w_router - w_router_hi.astype(F32)).astype(BF16)
    w_router_hi_lo = jnp.concatenate([w_router_hi, w_router_lo], axis=1)
    bsz, t, d = x.shape
    hv = n_heads * dv
    tm = READOUT_SUBBLOCKS * TOKEN_BLOCK
    assert t % tm == 0
    tok = lambda b, i: (b, i, 0)
    full2 = lambda b, i: (0, 0)
    return pl.pallas_call(
        functools.partial(_readout_kernel, n_heads=n_heads, dv=dv, affine=affine),
        grid=(bsz, t // tm),
        in_specs=[
            pl.BlockSpec((None, tm, hv), tok),
            pl.BlockSpec((None, tm, hv), tok),
            pl.BlockSpec((None, tm, d), tok),
            pl.BlockSpec(mods.shape, lambda b, i: (0, 0, 0)),
            pl.BlockSpec((hv, d), full2),
            pl.BlockSpec((1, dv), full2),
            pl.BlockSpec((1, d), full2),
            pl.BlockSpec((d, 2 * LANES), full2),
            pl.BlockSpec((d, LANES), full2),
        ],
        out_specs=[
            pl.BlockSpec((None, tm, d), tok),
            pl.BlockSpec((None, tm, d), tok),
            pl.BlockSpec((None, tm, LANES), tok),
        ],
        out_shape=[
            jax.ShapeDtypeStruct((bsz, t, d), F32),
            jax.ShapeDtypeStruct((bsz, t, d), BF16),
            jax.ShapeDtypeStruct((bsz, t, LANES), F32),
        ],
        compiler_params=_cparams(("parallel", "parallel")),
    )(o, zg, x, mods, w_out, g_norm, nw_ffn, w_router_hi_lo, w_router_hi)


def _excl_cumsum_rows(x01, tri_strict):
    blk = tri_strict.shape[0]
    carry = jnp.zeros((1, x01.shape[1]), F32)
    out = []
    for r0 in range(0, x01.shape[0], blk):
        xb = x01[r0:r0 + blk, :]
        out.append(_dot(tri_strict, xb.astype(BF16)) + carry)
        carry = carry + jnp.sum(xb, axis=0, keepdims=True)
    return out[0] if len(out) == 1 else jnp.concatenate(out, axis=0)


def _route_kernel(aff_ref, packed_ref, tri_ref, pos_ref, pos_rows_ref, aff_rows_ref, starts_ref, *, segments):
    tri_strict = tri_ref[...]
    per_row = LANES // N_EXPERTS

    def expert_major(a):
        return a.T[0:N_EXPERTS, :]

    for r0, n, cap in segments:
        aff = aff_ref[r0:r0 + n, :]
        aff_rows_ref[:, r0:r0 + n] = expert_major(aff)
        if cap == 0:
            pos_ref[r0:r0 + n, :] = jnp.full((n, LANES), -1.0, F32)
            pos_rows_ref[:, r0:r0 + n] = jnp.full((N_EXPERTS, n), -1.0, F32)
            continue
        packed = packed_ref[r0 // per_row:(r0 + n) // per_row, :]

        def as_float(pattern):
            return lax.bitcast_convert_type(pattern, F32)

        def refine(thr, low_bit, n_bits):
            counts = []
            for digit in range(1, 1 << n_bits):
                cand = thr | jnp.left_shift(jnp.int32(digit), low_bit)
                counts.append(jnp.sum(jnp.where(packed >= as_float(cand), 1.0, 0.0), axis=0, keepdims=True))
            cnt = jnp.concatenate(counts + [jnp.zeros((1, LANES), F32)], axis=0)
            shift = LANES // 2
            while shift >= N_EXPERTS:
                cnt = cnt + pltpu.roll(cnt, shift, 1)
                shift //= 2
            digit = jnp.sum(jnp.where(cnt >= cap, 1.0, 0.0), axis=0, keepdims=True).astype(jnp.int32)
            return thr | jnp.left_shift(digit, low_bit)

        thr = refine(jnp.zeros((1, LANES), jnp.int32), 28, 3)
        thr = lax.fori_loop(0, 7, lambda i, th: refine(th, 24 - 4 * i, 4), thr)
        beyond = as_float(thr + 1)
        above = jnp.where(aff >= beyond, 1.0, 0.0)
        tied = jnp.where((aff >= as_float(thr)) & (aff < beyond), 1.0, 0.0)
        need = cap - jnp.sum(above, axis=0, keepdims=True)
        tie_rank = _excl_cumsum_rows(tied, tri_strict)
        sel = above + tied * jnp.where(tie_rank < need, 1.0, 0.0)
        before = _excl_cumsum_rows(sel, tri_strict)
        pos = jnp.where(sel > 0.0, before, -1.0)
        pos_ref[r0:r0 + n, :] = pos
        pos_rows_ref[:, r0:r0 + n] = expert_major(pos)
        if r0 == 0:
            marks = [before[i:i + 1, :] for i in range(0, n, TOKEN_BLOCK)] + [jnp.sum(sel, axis=0, keepdims=True)]
            marks += [jnp.zeros((1, LANES), F32)] * (starts_ref.shape[0] - len(marks))
            starts_ref[...] = jnp.concatenate(marks, axis=0)


def _route(aff, segments):
    bsz, t, _ = aff.shape
    r = np.arange(TOKEN_BLOCK)
    tri_strict = jnp.asarray((r[None, :] < r[:, None]).astype(np.float32), BF16)
    per_row = LANES // N_EXPERTS
    packed = aff[:, :, :N_EXPERTS].reshape(bsz, t // per_row, LANES)
    return pl.pallas_call(
        functools.partial(_route_kernel, segments=segments),
        grid=(bsz,),
        in_specs=[pl.BlockSpec((None, t, LANES), lambda b: (b, 0, 0)),
                  pl.BlockSpec((None, t // per_row, LANES), lambda b: (b, 0, 0)),
                  pl.BlockSpec(tri_strict.shape, lambda b: (0, 0))],
        out_specs=[pl.BlockSpec((None, t, LANES), lambda b: (b, 0, 0)),
                   pl.BlockSpec((None, N_EXPERTS, t), lambda b: (b, 0, 0)),
                   pl.BlockSpec((None, N_EXPERTS, t), lambda b: (b, 0, 0)),
                   pl.BlockSpec((None, 16, LANES), lambda b: (b, 0, 0))],
        out_shape=[jax.ShapeDtypeStruct((bsz, t, LANES), F32),
                   jax.ShapeDtypeStruct((bsz, N_EXPERTS, t), F32),
                   jax.ShapeDtypeStruct((bsz, N_EXPERTS, t), F32),
                   jax.ShapeDtypeStruct((bsz, 16, LANES), F32)],
        compiler_params=_cparams(("parallel",)),
    )(aff, packed, tri_strict)


def _onehot(cond):
    return jnp.where(cond, 1.0, 0.0).astype(BF16)


def _gather_kernel(hb_ref, aff_ref, pos_ref, xe_ref, gate_ref, *, segments):
    e = pl.program_id(1)
    pos = pos_ref[pl.ds(e, 1), :]
    aff = aff_ref[pl.ds(e, 1), :]
    out0 = 0
    for r0, n, cap in segments:
        if cap == 0:
            continue
        slot = lax.broadcasted_iota(jnp.int32, (cap, n), 0).astype(F32)
        hit = pos[:, r0:r0 + n] == slot
        xe_ref[out0:out0 + cap, :] = _dot(_onehot(hit), hb_ref[r0:r0 + n, :]).astype(BF16)
        picked = jnp.where(hit, aff[:, r0:r0 + n], 0.0)
        folded = picked[:, 0:LANES]
        for c0 in range(LANES, n, LANES):
            folded = folded + picked[:, c0:c0 + LANES]
        gate = jnp.sum(folded, axis=1, keepdims=True)
        gate_ref[out0:out0 + cap, :] = jnp.broadcast_to(gate, (cap, LANES))
        out0 += cap


def _gather(hb, aff_rows, pos_rows, segments):
    bsz, t, d = hb.shape
    rows = sum(cap for _, _, cap in segments)
    slots = lambda b, e: (e, b, 0)
    per_expert = pl.BlockSpec((None, N_EXPERTS, t), lambda b, e: (b, 0, 0))
    return pl.pallas_call(
        functools.partial(_gather_kernel, segments=segments),
        grid=(bsz, N_EXPERTS),
        in_specs=[pl.BlockSpec((None, t, d), lambda b, e: (b, 0, 0)), per_expert, per_expert],
        out_specs=[pl.BlockSpec((None, rows, d), slots), pl.BlockSpec((None, rows, LANES), slots)],
        out_shape=[jax.ShapeDtypeStruct((N_EXPERTS, bsz * rows, d), BF16),
                   jax.ShapeDtypeStruct((N_EXPERTS, bsz * rows, LANES), F32)],
        compiler_params=_cparams(("parallel", "parallel")),
    )(hb, aff_rows, pos_rows)


def _ffn_kernel(xe_ref, gate_ref, wg_ref, wu_ref, wd_ref, y_ref, acc_ref, *, row_chunk):
    f = pl.program_id(1)
    last = pl.num_programs(1) - 1

    def step(mode):
        wg = wg_ref[...].astype(BF16)
        wu = wu_ref[...].astype(BF16)
        wd = wd_ref[...].astype(BF16)
        for r0 in range(0, xe_ref.shape[0], row_chunk):
            rows = slice(r0, r0 + row_chunk)
            xe = xe_ref[rows, :]
            hidden = (_silu(_dot(xe, wg)) * _dot(xe, wu)).astype(BF16)
            part = _dot(hidden, wd)
            if mode == "first":
                acc_ref[rows, :] = part
            elif mode == "middle":
                acc_ref[rows, :] += part
            else:
                y_ref[rows, :] = ((acc_ref[rows, :] + part) * gate_ref[rows, 0:1]).astype(y_ref.dtype)

    pl.when(f == 0)(lambda: step("first"))
    pl.when((f > 0) & (f < last))(lambda: step("middle"))
    pl.when(f == last)(lambda: step("last"))


def _ffn(xe, gate, w_gate, w_up, w_down, layer):
    n_rows, d = xe.shape[1], xe.shape[2]
    ff = w_gate.shape[-1]
    row_chunk = n_rows // 2 if n_rows % 2 == 0 and (n_rows // 2) % 16 == 0 else n_rows
    return pl.pallas_call(
        functools.partial(_ffn_kernel, row_chunk=row_chunk),
        grid=(N_EXPERTS, ff // FF_TILE),
        in_specs=[
            pl.BlockSpec((None, n_rows, d), lambda e, f: (e, 0, 0)),
            pl.BlockSpec((None, n_rows, LANES), lambda e, f: (e, 0, 0)),
            pl.BlockSpec((None, None, d, FF_TILE), lambda e, f: (layer, e, 0, f)),
            pl.BlockSpec((None, None, d, FF_TILE), lambda e, f: (layer, e, 0, f)),
            pl.BlockSpec((None, None, FF_TILE, d), lambda e, f: (layer, e, f, 0)),
        ],
        out_specs=pl.BlockSpec((None, n_rows, d), lambda e, f: (e, 0, 0)),
        out_shape=jax.ShapeDtypeStruct((N_EXPERTS, n_rows, d), BF16),
        scratch_shapes=[pltpu.VMEM((n_rows, d), F32)],
        compiler_params=_cparams(("parallel", "arbitrary")),
    )(xe, gate, w_gate, w_up, w_down)


SLOT_WINDOW = 128


def _scatter_kernel(starts_ref, x_ref, y_ref, pos_ref, mods_ref, nw_ref, *rest, segments, final, with_next):
    if with_next:
        next_mods_ref, next_nw_ref, out_ref, h_ref = rest
    else:
        out_ref, = rest
    d = y_ref.shape[-1]
    ctx_row = mods_ref.shape[0] - 1
    n_sub = x_ref.shape[0] // TOKEN_BLOCK
    on_last_block = pl.program_id(1) == pl.num_programs(1) - 1
    cap_lat = segments[0][2]

    def window(sub_block, e):
        blk = pl.program_id(1) * n_sub + sub_block
        first = jnp.minimum((starts_ref[blk, e] >> 4) << 4, cap_lat - SLOT_WINDOW)
        return first, starts_ref[blk + 1, e] <= first + SLOT_WINDOW

    def block(seg_id, rows, windowed=None):
        cap = segments[seg_id][2]
        slot0 = sum(seg[2] for seg in segments[:seg_id])
        mod_row = ctx_row if seg_id == 1 else pl.program_id(0)
        gate = mods_ref[mod_row][5:6, :]
        pos = pos_ref[rows, :]
        if windowed is None:
            y_all = y_ref[:, slot0:slot0 + cap, :].reshape(N_EXPERTS * cap, d)
            slot = lax.broadcasted_iota(jnp.int32, (TOKEN_BLOCK, cap), 1).astype(F32)
            onehots = jnp.concatenate([_onehot(pos[:, e:e + 1] == slot) for e in range(N_EXPERTS)], axis=1)
        else:
            slot = lax.broadcasted_iota(jnp.int32, (TOKEN_BLOCK, SLOT_WINDOW), 1).astype(F32)
            firsts = [window(windowed, e)[0] for e in range(N_EXPERTS)]
            y_all = jnp.concatenate(
                [y_ref[e, pl.ds(pl.multiple_of(slot0 + firsts[e], 16), SLOT_WINDOW), :] for e in range(N_EXPERTS)], axis=0)
            onehots = jnp.concatenate(
                [_onehot(pos[:, e:e + 1] - firsts[e].astype(F32) == slot) for e in range(N_EXPERTS)], axis=1)
        x = x_ref[rows, :] + gate * _dot(onehots, y_all)
        if final:
            x = x * lax.rsqrt(jnp.mean(x * x, axis=-1, keepdims=True) + EPS) * nw_ref[...]
        out_ref[rows, :] = x
        if with_next:
            m = next_mods_ref[mod_row]
            h_ref[rows, :] = _rms_mod(x, next_nw_ref[...], m[0:1, :], m[1:2, :]).astype(BF16)

    has_ctx = segments[1][2] > 0
    n_lat_blocks = segments[0][1] // TOKEN_BLOCK

    def step(use_windows):
        for s in range(n_sub):
            rows = slice(s * TOKEN_BLOCK, (s + 1) * TOKEN_BLOCK)
            lat = functools.partial(block, 0, rows, s if use_windows else None)
            if has_ctx and s == n_sub - 1:
                pl.when(jnp.logical_not(on_last_block))(lat)
                pl.when(on_last_block)(functools.partial(block, 1, rows))
            else:
                lat()

    fits = jnp.bool_(True)
    for s in range(n_sub):
        is_lat = pl.program_id(1) * n_sub + s < n_lat_blocks
        for e in range(N_EXPERTS):
            safe = window(jnp.where(is_lat, s, 0) if has_ctx and s == n_sub - 1 else s, e)[1]
            fits = jnp.logical_and(fits, jnp.logical_or(safe, jnp.logical_not(is_lat)))
    pl.when(fits)(lambda: step(True))
    pl.when(jnp.logical_not(fits))(lambda: step(False))


def _scatter(x, y, pos, starts, mods, nw_final, segments, final, next_norm=None):
    bsz, t, d = x.shape
    assert segments[0][2] >= SLOT_WINDOW and starts.shape[1] == segments[0][1] // TOKEN_BLOCK + 1
    n_slots = y.shape[1] // bsz
    out_rows = sum(n for _, n, cap in segments if cap > 0)
    tm = (3 if segments[1][2] > 0 else 2) * TOKEN_BLOCK
    assert out_rows % tm == 0
    tok = lambda b, i: (b, i, 0)
    const2, const3 = (lambda b, i: (0, 0)), (lambda b, i: (0, 0, 0))
    in_specs = [
        pl.BlockSpec((None,) + starts.shape[1:], lambda b, i: (b, 0, 0), memory_space=pltpu.SMEM),
        pl.BlockSpec((None, tm, d), tok),
        pl.BlockSpec((N_EXPERTS, n_slots, d), lambda b, i: (0, b, 0)),
        pl.BlockSpec((None, tm, LANES), tok),
        pl.BlockSpec(mods.shape, const3),
        pl.BlockSpec((1, d), const2),
    ]
    args = [starts, x, y, pos, mods, nw_final]
    out_specs = [pl.BlockSpec((None, tm, d), tok)]
    out_shape = [jax.ShapeDtypeStruct((bsz, out_rows, d), F32)]
    if next_norm is not None:
        in_specs += [pl.BlockSpec(next_norm[0].shape, const3), pl.BlockSpec((1, d), const2)]
        args += list(next_norm)
        out_specs.append(pl.BlockSpec((None, tm, d), tok))
        out_shape.append(jax.ShapeDtypeStruct((bsz, out_rows, d), BF16))
    return pl.pallas_call(
        functools.partial(_scatter_kernel, segments=segments, final=final, with_next=next_norm is not None),
        grid=(bsz, out_rows // tm),
        in_specs=in_specs, out_specs=out_specs, out_shape=out_shape,
        compiler_params=_cparams(("parallel", "parallel")),
    )(*args)


def _moe(x, hb, aff, mods, w_gate, w_up, w_down, layer, nw_final, with_ctx, final, next_norm):
    bsz, t, d = x.shape
    seq = t - TOKEN_BLOCK
    cap_lat = EC_CAPACITY * seq // N_EXPERTS
    cap_ctx = EC_CAPACITY * TOKEN_BLOCK // N_EXPERTS if with_ctx else 0
    segments = ((0, seq, cap_lat), (seq, TOKEN_BLOCK, cap_ctx))
    pos, pos_rows, aff_rows, starts = _route(aff, segments)
    starts = starts[:, :seq // TOKEN_BLOCK + 1, :N_EXPERTS].astype(jnp.int32)
    xe, gate = _gather(hb, aff_rows, pos_rows, segments)
    y = _ffn(xe, gate, w_gate, w_up, w_down, layer)
    return _scatter(x, y, pos, starts, mods, nw_final, segments, final, next_norm)


def _rope_tables(seq, n_ctx):
    half = RET_DK // 2
    t = jnp.arange(seq)
    inv = ROPE_THETA ** (-jnp.arange(0, half, 2, dtype=F32) / half)
    tabs = []
    for pos in ((t // GRID_W).astype(F32), (t % GRID_W).astype(F32)):
        ang = pos[:, None] * inv
        cos, sin = jnp.cos(ang), jnp.sin(ang)
        tabs.append(jnp.concatenate([cos, cos], axis=1))
        tabs.append(jnp.concatenate([-sin, sin], axis=1))
    ident = [jnp.ones, jnp.zeros, jnp.ones, jnp.zeros]
    return [jnp.concatenate([tab, fn((n_ctx, 2 * (half // 2)), F32)], axis=0) for tab, fn in zip(tabs, ident)]


def kernel(x, c, ctx, c_ctx, w_ada, b_ada, norm_mix, norm_ffn, ret_w_in, ret_w_out, hg_w_in, hg_g_norm,
           hg_lower_bounds, hg_w_out, moe_router, moe_w_gate, moe_w_up, moe_w_down, norm_final):
    bsz, seq, d = x.shape
    n_ctx = ctx.shape[1]
    depth = w_ada.shape[0]
    assert n_ctx == TOKEN_BLOCK and seq % TOKEN_BLOCK == 0 and d == RET_HEADS * RET_DK == HG_HEADS * HG_DK

    cc = jnp.concatenate([c, c_ctx[None, :], jnp.zeros((16 - bsz - 1, d), F32)], axis=0)
    mods_all = _ada(cc, w_ada, b_ada)[:, :bsz + 1].reshape(depth, bsz + 1, 6, d)
    t = seq + n_ctx
    rope = _rope_tables(seq, n_ctx)
    rope_specs = [pl.BlockSpec((t // 2, LANES), lambda j, b, i: (i, 0))] * 4
    w_router = jnp.pad(moe_router, ((0, 0), (0, 0), (0, LANES - N_EXPERTS)))

    xs, h = _prenorm(x, ctx, mods_all[0], norm_mix[0][None, :])
    for layer in range(depth):
        last = layer == depth - 1
        slot = layer // 2
        mods = mods_all[layer]
        if layer % 2 == 0:
            assert RET_HEADS * RET_DK == PROJ_COLS
            w = ret_w_in[slot]
            assert np.log2(RET_DK ** -0.5) % 1 == 0
            q, = _proj(h, w, 0, 1, _ep_rope, [BF16], rope, rope_specs)
            k, = _proj(h, w, 1, 1, functools.partial(_ep_rope, scale=RET_DK ** -0.5), [BF16], rope, rope_specs)
            v, = _proj(h, w, 2, 2, _ep_plain, [BF16])
            zg, = _proj(h, w, 4, 2, _ep_silu, [BF16])
            o = _retention(q, k, v)
            w_out, g_norm, n_heads, dv = ret_w_out[slot].astype(BF16), None, RET_HEADS, RET_DV
        else:
            assert HG_HEADS * HG_DK == PROJ_COLS
            w = hg_w_in[slot]
            lb_specs = [pl.BlockSpec((depth, PROJ_COLS), lambda j, b, i: (0, 0))]
            ep_gate = functools.partial(_ep_gate, layer=layer)
            q, = _proj(h, w, 0, 1, _ep_silu, [BF16])
            kf, gf = _proj(h, w, 1, 1, ep_gate, [F32, F32], (hg_lower_bounds[:, 0, :],), lb_specs)
            kb, gb = _proj(h, w, 2, 1, ep_gate, [F32, F32], (hg_lower_bounds[:, 1, :],), lb_specs)
            v, = _proj(h, w, 3, 1, _ep_plain, [BF16])
            zg, = _proj(h, w, 4, 1, _ep_silu, [BF16])
            o = _hgrn2(q, kf, kb, gf, gb, v)
            w_out, g_norm, n_heads, dv = hg_w_out[slot].astype(BF16), hg_g_norm[slot][None, :], HG_HEADS, HG_DV
        xs, hb, aff = _readout(o, zg, xs, mods, w_out, g_norm, norm_ffn[layer][None, :], w_router[layer],
                                       n_heads, dv)
        next_norm = None if last else (mods_all[layer + 1], norm_mix[layer + 1][None, :])
        res = _moe(xs, hb, aff, mods, moe_w_gate, moe_w_up, moe_w_down, layer, norm_final[None, :],
                   with_ctx=not last, final=last, next_norm=next_norm)
        xs, h = res if next_norm is not None else (res[0], None)
    return xs
```

```python
import functools

import numpy as np
import jax
import jax.numpy as jnp
from jax import lax
from jax.experimental import pallas as pl
from jax.experimental.pallas import tpu as pltpu

F32 = jnp.float32
BF16 = jnp.bfloat16
EPS = 1e-6
ROPE_THETA = 10000.0
GRID_W = 64

LANES = 128
TOKEN_BLOCK = 256
RET_HEADS, RET_DK, RET_DV, RET_CHUNK = 4, 256, 512, 256
HG_HEADS, HG_DK, HG_DV, HG_CHUNK, HG_SUB = 8, 128, 128, 128, 8
HG_WIDE = HG_CHUNK
HG_HEADS_PER_STEP = 2
HG_FAST_SPAN = 96.0
READOUT_SUBBLOCKS = 3
LOG2_E = 1.4426950408889634
N_EXPERTS = 16
EC_CAPACITY = 2
FF_TILE = 256
PROJ_COLS = 1024
VMEM_LIMIT = 56 * 1024 * 1024

HIGHEST = lax.Precision.HIGHEST


def _cparams(sem):
    return pltpu.CompilerParams(dimension_semantics=sem, vmem_limit_bytes=VMEM_LIMIT)


def _dot(a, b, **kw):
    return jnp.dot(a, b, preferred_element_type=F32, **kw)


def _dot_nt(a, b):
    return lax.dot_general(a, b, (((1,), (1,)), ((), ())), preferred_element_type=F32)


def _silu(x):
    return x * jax.nn.sigmoid(x)


def _rms_mod(x, nw, shift, scale):
    ms = jnp.mean(x * x, axis=-1, keepdims=True)
    h = x * lax.rsqrt(ms + EPS) * nw
    return h * (1.0 + scale) + shift


def _ada_kernel(c_ref, w_ref, b_ref, o_ref):
    o_ref[...] = _dot(_silu(c_ref[...]), w_ref[...], precision=HIGHEST) + b_ref[...]


def _ada(cc, w_ada, b_ada):
    depth, d, d6 = w_ada.shape
    rows = cc.shape[0]
    return pl.pallas_call(
        _ada_kernel,
        grid=(depth, d6 // d),
        in_specs=[
            pl.BlockSpec((rows, d), lambda l, j: (0, 0)),
            pl.BlockSpec((None, d, d), lambda l, j: (l, 0, j)),
            pl.BlockSpec((None, 1, d), lambda l, j: (l, 0, j)),
        ],
        out_specs=pl.BlockSpec((None, rows, d), lambda l, j: (l, 0, j)),
        out_shape=jax.ShapeDtypeStruct((depth, rows, d6), F32),
        compiler_params=_cparams(("parallel", "parallel")),
    )(cc, w_ada, b_ada.reshape(depth, 1, d6))


def _prenorm_kernel(x_ref, ctx_ref, mods_ref, nw_ref, xs_ref, h_ref):
    ctx_row = mods_ref.shape[0] - 1
    nw = nw_ref[...]
    seq = x_ref.shape[0]
    for src_ref, r0, row in ((x_ref, 0, pl.program_id(0)), (ctx_ref, seq, ctx_row)):
        m = mods_ref[row]
        shift, scale = m[0:1, :], m[1:2, :]

        def body(i, carry):
            src = pl.ds(pl.multiple_of(i * TOKEN_BLOCK, TOKEN_BLOCK), TOKEN_BLOCK)
            dst = pl.ds(pl.multiple_of(r0 + i * TOKEN_BLOCK, TOKEN_BLOCK), TOKEN_BLOCK)
            x = src_ref[src, :]
            xs_ref[dst, :] = x
            h_ref[dst, :] = _rms_mod(x, nw, shift, scale).astype(BF16)
            return carry

        lax.fori_loop(0, src_ref.shape[0] // TOKEN_BLOCK, body, 0)


def _prenorm(x, ctx, mods, nw):
    bsz, seq, d = x.shape
    n_ctx = ctx.shape[1]
    t = seq + n_ctx
    return pl.pallas_call(
        _prenorm_kernel,
        grid=(bsz,),
        in_specs=[pl.BlockSpec((None, seq, d), lambda b: (b, 0, 0)),
                  pl.BlockSpec((None, n_ctx, d), lambda b: (b, 0, 0)),
                  pl.BlockSpec(mods.shape, lambda b: (0, 0, 0)),
                  pl.BlockSpec((1, d), lambda b: (0, 0))],
        out_specs=[pl.BlockSpec((None, t, d), lambda b: (b, 0, 0)), pl.BlockSpec((None, t, d), lambda b: (b, 0, 0))],
        out_shape=[jax.ShapeDtypeStruct((bsz, t, d), F32), jax.ShapeDtypeStruct((bsz, t, d), BF16)],
        compiler_params=_cparams(("parallel",)),
    )(x, ctx, mods, nw)


def _proj_kernel(h_ref, w_ref, *rest, epilogue, n_extra):
    extras, outs = rest[:n_extra], rest[n_extra:]
    epilogue(_dot(h_ref[...], w_ref[...].astype(BF16)), extras, outs)


def _ep_plain(acc, extras, outs):
    outs[0][...] = acc.astype(outs[0].dtype)


def _ep_silu(acc, extras, outs):
    outs[0][...] = _silu(acc).astype(outs[0].dtype)


def _ep_rope(acc, extras, outs, *, scale=None):
    c_row, s_row, c_col, s_col = (r[...] for r in extras)
    for s in range(acc.shape[1] // LANES):
        slab = acc[:, s * LANES:(s + 1) * LANES]
        if scale is not None:
            slab = slab * scale
        cos, sin = (c_row, s_row) if s % 2 == 0 else (c_col, s_col)
        rot = slab * cos + pltpu.roll(slab, LANES // 2, 1) * sin
        outs[0][:, s * LANES:(s + 1) * LANES] = rot.astype(outs[0].dtype)


def _ep_gate(acc, extras, outs, *, layer):
    raw = extras[0][...]
    e = jnp.exp(raw - jnp.max(raw, axis=0, keepdims=True))
    sm = e / jnp.sum(e, axis=0, keepdims=True)
    lb = -sm[0:1, :]
    for l in range(layer + 1):
        lb = lb + sm[l:l + 1, :]
    f = lb + (1.0 - lb) * jax.nn.sigmoid(acc)
    outs[0][...] = 1.0 - f
    outs[1][...] = jnp.log(f)


def _proj(h, w, col_block0, ncol, epilogue, out_dtypes, extras=(), extra_specs=()):
    bsz, t, d = h.shape
    tm, tn = t // 2, PROJ_COLS
    in_specs = [
        pl.BlockSpec((None, tm, d), lambda j, b, i: (b, i, 0)),
        pl.BlockSpec((d, tn), lambda j, b, i: (0, col_block0 + j)),
    ] + list(extra_specs)
    out_specs = [pl.BlockSpec((None, tm, tn), lambda j, b, i: (b, i, j)) for _ in out_dtypes]
    out_shape = [jax.ShapeDtypeStruct((bsz, t, tn * ncol), dt) for dt in out_dtypes]
    return pl.pallas_call(
        functools.partial(_proj_kernel, epilogue=epilogue, n_extra=len(extras)),
        grid=(ncol, bsz, 2), in_specs=in_specs, out_specs=out_specs, out_shape=out_shape,
        compiler_params=_cparams(("parallel", "parallel", "parallel")),
    )(h, w, *extras)


def _ret_tables():
    c = RET_CHUNK
    j = np.arange(2 * RET_HEADS, dtype=np.float64)
    lg = np.log1p(-np.exp2(-5.0 - j / 2))
    lg_f, lg_b = lg[0::2], lg[1::2]
    i = np.arange(c, dtype=np.float64)
    diff = i[:, None] - i[None, :]
    mats = np.zeros((RET_HEADS, c, c), np.float64)
    tabs = np.zeros((RET_HEADS, c, LANES), np.float64)
    for h in range(RET_HEADS):
        mats[h] = np.where(diff >= 0, np.exp(diff * lg_f[h]), 0.0) + np.where(diff <= 0, np.exp(-diff * lg_b[h]), 0.0)
        tabs[h, :, 0] = np.exp((i + 1) * lg_f[h])
        tabs[h, :, 1] = np.exp((c - 1 - i) * lg_f[h])
        tabs[h, :, 2] = np.exp((c - i) * lg_b[h])
        tabs[h, :, 3] = np.exp(i * lg_b[h])
        tabs[h, :, 4] = np.exp(c * lg_f[h])
        tabs[h, :, 5] = np.exp(c * lg_b[h])
    return jnp.asarray(mats, F32), jnp.asarray(tabs, F32)


def _ret_kernel(q_ref, k_ref, v_ref, m_ref, tab_ref, out_ref, o_ref, vt_ref, sf_ref, sb_ref, *, n_chunks):
    c = RET_CHUNK
    decay = m_ref[...]
    tab = tab_ref[...]
    qf, kfs, qb, kbs = tab[:, 0:1], tab[:, 1:2], tab[:, 2:3], tab[:, 3:4]
    end_f, end_b = tab[0:1, 4:5], tab[0:1, 5:6]

    def rows(ci):
        return slice(ci * c, (ci + 1) * c)

    for ci in range(n_chunks):
        q, k, v = q_ref[rows(ci), :], k_ref[rows(ci), :], v_ref[rows(ci), :]
        scores = (_dot_nt(q, k) * decay).astype(BF16)
        o_ref[rows(ci), :] = _dot(scores, v)
        vt_ref[ci] = v.astype(F32).T.astype(BF16)

    ctx = n_chunks - 1
    order_f = [ctx] + list(range(ctx))
    order_b = [ctx] + list(range(ctx - 1, -1, -1))
    sf_ref[...] = jnp.zeros_like(sf_ref)
    sb_ref[...] = jnp.zeros_like(sb_ref)
    for step in range(n_chunks):
        for order, s_ref, qd, kd, end in ((order_f, sf_ref, qf, kfs, end_f), (order_b, sb_ref, qb, kbs, end_b)):
            ci = order[step]
            if step > 0:
                o_ref[rows(ci), :] += _dot_nt(q_ref[rows(ci), :], s_ref[...].astype(BF16)) * qd
            if step < n_chunks - 1:
                ks = (k_ref[rows(ci), :].astype(F32) * kd).astype(BF16)
                upd = _dot(vt_ref[ci], ks)
                s_ref[...] = upd if step == 0 else s_ref[...] * end + upd
    for ci in range(n_chunks):
        out_ref[rows(ci), :] = o_ref[rows(ci), :].astype(out_ref.dtype)


def _retention(q, k, v):
    bsz, t, _ = q.shape
    n_chunks = t // RET_CHUNK
    mats, tabs = _ret_tables()
    return pl.pallas_call(
        functools.partial(_ret_kernel, n_chunks=n_chunks),
        grid=(bsz, RET_HEADS),
        in_specs=[
            pl.BlockSpec((None, t, RET_DK), lambda b, h: (b, 0, h)),
            pl.BlockSpec((None, t, RET_DK), lambda b, h: (b, 0, h)),
            pl.BlockSpec((None, t, RET_DV), lambda b, h: (b, 0, h)),
            pl.BlockSpec((None, RET_CHUNK, RET_CHUNK), lambda b, h: (h, 0, 0)),
            pl.BlockSpec((None, RET_CHUNK, LANES), lambda b, h: (h, 0, 0)),
        ],
        out_specs=pl.BlockSpec((None, t, RET_DV), lambda b, h: (b, 0, h)),
        out_shape=jax.ShapeDtypeStruct((bsz, t, RET_HEADS * RET_DV), BF16),
        scratch_shapes=[
            pltpu.VMEM((t, RET_DV), F32),
            pltpu.VMEM((n_chunks, RET_DV, RET_CHUNK), BF16),
            pltpu.VMEM((RET_DV, RET_DK), F32),
            pltpu.VMEM((RET_DV, RET_DK), F32),
        ],
        compiler_params=_cparams(("parallel", "parallel")),
    )(q, k, v, mats, tabs)


def _hg_levels(block):
    return [HG_CHUNK >> (i + 1) for i in range(int(np.log2(HG_CHUNK // block)))]


def _hg_masks():
    c = HG_CHUNK
    r = np.arange(c)[:, None]
    col = np.arange(c)[None, :]
    tables = {}
    for lower in (True, False):
        for s in _hg_levels(HG_SUB):
            same = (r // (2 * s)) == (col // (2 * s))
            rh, ch = (r // s) % 2, (col // s) % 2
            tables["level", lower, s] = same & ((rh == 1) & (ch == 0) if lower else (rh == 0) & (ch == 1))
            far = (rh == 1) if lower else (rh == 0)
            tables["sign", lower, s] = np.broadcast_to(np.where(far, 1.0, -1.0), (c, c))
        for block in (HG_SUB, HG_WIDE):
            tables["diag", lower, block] = ((r // block) == (col // block)) & ((col <= r) if lower else (col >= r))
    index = {name: i for i, name in enumerate(tables)}
    return index, np.stack([np.asarray(t, np.float32) for t in tables.values()])


def _hg_tables():
    c, sub = HG_CHUNK, HG_SUB
    r = np.arange(c)[:, None]
    col = np.arange(c)[None, :]
    tri = np.stack([col <= r, col >= r]).astype(np.float32)
    place = np.stack([np.broadcast_to((np.arange(c) % sub) == j, (c, c)) for j in range(sub)]).astype(np.float32)
    return jnp.asarray(_hg_masks()[1]), jnp.asarray(tri, BF16), jnp.asarray(place, BF16)


def _split3(g):
    hi = g.astype(BF16)
    r1 = g - hi.astype(F32)
    mid = r1.astype(BF16)
    lo = (r1 - mid.astype(F32)).astype(BF16)
    return hi, mid, lo


def _hg_intra(q, k, b, masks, place, lower, block, pairwise=False):
    c, sub = HG_CHUNK, HG_SUB
    index = _hg_masks()[0]
    total = jnp.zeros((c, c), F32)
    for s in _hg_levels(block):
        pieces = []
        for m in range(0, c, 2 * s):
            r = m + s - 1 if lower else m + s
            pieces.append(jnp.broadcast_to(b[r:r + 1, :], (2 * s, b.shape[1])))
        ref = pieces[0] if len(pieces) == 1 else jnp.concatenate(pieces, axis=0)
        w = jnp.exp2((b - ref) * masks[index["sign", lower, s]])
        total = total + _dot_nt((q * w).astype(BF16), (k * w).astype(BF16)) * masks[index["level", lower, s]]
    diag_mask = masks[index["diag", lower, block]]
    if not pairwise:
        firsts = [u * block if lower else u * block + block - 1 for u in range(c // block)]
        ref = jnp.concatenate([jnp.broadcast_to(b[r:r + 1, :], (block, b.shape[1])) for r in firsts], axis=0)
        d = b - ref
        p = _dot_nt((q * jnp.exp2(d)).astype(BF16), (k * jnp.exp2(-d)).astype(BF16))
        return total + p * diag_mask, jnp.max(-d)
    assert block == sub
    shape3 = (c // sub, sub, q.shape[1])
    q3, b3 = q.reshape(shape3), b.reshape(shape3)
    a3 = (jnp.log2(jnp.maximum(k, 0.0)) - b).reshape(shape3)
    acc = jnp.zeros((c, c), F32)
    for jj in range(sub):
        aj = jnp.broadcast_to(a3[:, jj:jj + 1, :], shape3)
        p = q3 * jnp.exp2(jnp.minimum(b3 + aj, 0.0))
        acc = acc + _dot(p.reshape(c, q.shape[1]).astype(BF16), place[jj])
    return total + acc * diag_mask, None


def _hg_kernel(q_ref, kf_ref, kb_ref, gf_ref, gb_ref, v_ref, mask_ref, tri_ref, place_ref, out_ref,
               o_ref, bf_ref, bb_ref, vt_ref, s_ref, *, n_chunks):
    c = HG_CHUNK
    masks = [mask_ref[i] for i in range(mask_ref.shape[0])]
    place = [place_ref[i] for i in range(place_ref.shape[0])]
    tri_lo, tri_up = tri_ref[0], tri_ref[1]

    def cumsum(tri, g):
        hi, mid, lo = _split3(g)
        return (_dot(tri, hi) + _dot(tri, mid) + _dot(tri, lo)) * LOG2_E

    def chunk_rows(ci):
        return pl.ds(ci * c if isinstance(ci, int) else pl.multiple_of(ci * c, c), c)

    slabs = [slice(hh * HG_DK, (hh + 1) * HG_DK) for hh in range(HG_HEADS_PER_STEP)]

    def finish(ci, hh, scores, v):
        o_ref[chunk_rows(ci), slabs[hh]] = _dot(scores, v)
        vt_ref[hh, ci] = v.astype(F32).T.astype(BF16)

    first = chunk_rows(0)
    bf_ref[first, :] = jnp.concatenate([cumsum(tri_lo, gf_ref[first, sl]) for sl in slabs], axis=1)
    bb_ref[first, :] = jnp.concatenate([cumsum(tri_up, gb_ref[first, sl]) for sl in slabs], axis=1)
    s_ref[first, :] = jnp.zeros((c, s_ref.shape[1]), BF16)

    def intra(ci, carry):
        prev, nxt = jnp.maximum(ci - 1, 0), jnp.minimum(ci + 1, n_chunks - 1)
        rows, rows_p, rows_n = chunk_rows(ci), chunk_rows(prev), chunk_rows(nxt)
        loaded = []
        for sl in slabs:
            loaded.append((bf_ref[rows, sl], bb_ref[rows, sl], q_ref[rows, sl].astype(F32), kf_ref[rows, sl],
                           kb_ref[rows, sl], s_ref[rows_p, sl], v_ref[rows_p, sl], gf_ref[rows_n, sl],
                           gb_ref[rows_n, sl]))
        span = 0.0
        for hh, (bf, bb, q, kf, kb, s_prev, v_prev, gf_n, gb_n) in enumerate(loaded):
            finish(prev, hh, s_prev, v_prev)
            bf_ref[rows_n, slabs[hh]] = cumsum(tri_lo, gf_n)
            bb_ref[rows_n, slabs[hh]] = cumsum(tri_up, gb_n)
            lo, span_lo = _hg_intra(q, kf, bf, masks, place, True, HG_WIDE)
            up, span_up = _hg_intra(q, kb, bb, masks, place, False, HG_WIDE)
            s_ref[rows, slabs[hh]] = (lo + up).astype(BF16)
            span = jnp.maximum(span, jnp.maximum(span_lo, span_up))

        def redo(block, pairwise):
            worst = 0.0
            for sl in slabs:
                q = q_ref[rows, sl].astype(F32)
                lo, span_lo = _hg_intra(q, kf_ref[rows, sl], bf_ref[rows, sl], masks, place, True, block, pairwise)
                up, span_up = _hg_intra(q, kb_ref[rows, sl], bb_ref[rows, sl], masks, place, False, block, pairwise)
                s_ref[rows, sl] = (lo + up).astype(BF16)
                if not pairwise:
                    worst = jnp.maximum(worst, jnp.maximum(span_lo, span_up))
            return worst

        @pl.when(span > HG_FAST_SPAN)
        def _():
            @pl.when(redo(HG_SUB, False) > HG_FAST_SPAN)
            def _():
                redo(HG_SUB, True)

        return carry

    lax.fori_loop(0, n_chunks, intra, 0)
    last = chunk_rows(n_chunks - 1)
    for hh, sl in enumerate(slabs):
        finish(n_chunks - 1, hh, s_ref[last, sl], v_ref[last, sl])

    n_ctx = TOKEN_BLOCK // c
    n_lat = n_chunks - n_ctx

    assert all((n_lat + s if s < n_ctx else s - n_ctx) != n_chunks - 1 - s for s in range(n_chunks))

    def scan(step, carry):
        cf = jnp.where(step < n_ctx, n_lat + step, step - n_ctx)
        cb = n_chunks - 1 - step
        jobs = []
        for hh, sl in enumerate(slabs):
            for ci, k_ref, b_ref, end_row, st in ((cf, kf_ref, bf_ref, c - 1, carry[2 * hh]),
                                                  (cb, kb_ref, bb_ref, 0, carry[2 * hh + 1])):
                rows = chunk_rows(ci)
                jobs.append((rows, sl, q_ref[rows, sl].astype(F32), b_ref[rows, sl], k_ref[rows, sl], vt_ref[hh, ci],
                             o_ref[rows, sl], st, end_row))
        states, updates = [], []
        for rows, sl, q, b, k, vt, o_old, st, end_row in jobs:
            qs = (q * jnp.exp2(b)).astype(BF16)
            updates.append((rows, sl, o_old + _dot_nt(qs, st.astype(BF16))))
            b_end = b[end_row:end_row + 1, :]
            ks = (k * jnp.exp2(b_end - b)).astype(BF16)
            states.append(st * jnp.exp2(b_end) + _dot(vt, ks))
        for rows, sl, o_new in updates:
            o_ref[rows, sl] = o_new
        return tuple(states)

    zero = jnp.zeros((HG_DV, HG_DK), F32)
    lax.fori_loop(0, n_chunks, scan, (zero,) * (2 * HG_HEADS_PER_STEP))

    def emit(ci, carry):
        out_ref[chunk_rows(ci), :] = o_ref[chunk_rows(ci), :].astype(out_ref.dtype)
        return carry

    lax.fori_loop(0, n_chunks, emit, 0)


def _hgrn2(q, kf, kb, gf, gb, v):
    bsz, t, _ = q.shape
    n_chunks = t // HG_CHUNK
    hps = HG_HEADS_PER_STEP
    masks, tri, place = _hg_tables()
    head = lambda b, h: (b, 0, h)
    spec = pl.BlockSpec((None, t, hps * HG_DK), head)
    return pl.pallas_call(
        functools.partial(_hg_kernel, n_chunks=n_chunks),
        grid=(bsz, HG_HEADS // hps),
        in_specs=[spec, spec, spec, spec, spec, spec,
                  pl.BlockSpec(masks.shape, lambda b, h: (0, 0, 0)),
                  pl.BlockSpec(tri.shape, lambda b, h: (0, 0, 0)),
                  pl.BlockSpec(place.shape, lambda b, h: (0, 0, 0))],
        out_specs=pl.BlockSpec((None, t, hps * HG_DV), head),
        out_shape=jax.ShapeDtypeStruct((bsz, t, HG_HEADS * HG_DV), BF16),
        scratch_shapes=[
            pltpu.VMEM((t, hps * HG_DV), F32),
            pltpu.VMEM((t, hps * HG_DK), F32),
            pltpu.VMEM((t, hps * HG_DK), F32),
            pltpu.VMEM((hps, n_chunks, HG_DV, HG_CHUNK), BF16),
            pltpu.VMEM((t, hps * HG_CHUNK), BF16),
        ],
        compiler_params=_cparams(("parallel", "parallel")),
    )(q, kf, kb, gf, gb, v, masks, tri, place)


def _readout_kernel(o_ref, zg_ref, x_ref, mods_ref, wout_ref, gn_ref, nw_ref, wra_ref, wrb_ref,
                    xo_ref, hb_ref, aff_ref, *, n_heads, dv, affine):
    gn = gn_ref[...] if affine else None
    n_sub = x_ref.shape[0] // TOKEN_BLOCK
    ctx_row = mods_ref.shape[0] - 1
    on_last_block = pl.program_id(1) == pl.num_programs(1) - 1
    for s in range(n_sub):
        rows = slice(s * TOKEN_BLOCK, (s + 1) * TOKEN_BLOCK)
        mod_row = jnp.where(on_last_block, ctx_row, pl.program_id(0)) if s == n_sub - 1 else pl.program_id(0)
        mod = mods_ref[mod_row]
        o = o_ref[rows, :].astype(F32)
        parts = []
        for h in range(n_heads):
            oh = o[:, h * dv:(h + 1) * dv]
            ms = jnp.mean(oh * oh, axis=-1, keepdims=True)
            on = oh * lax.rsqrt(ms + EPS)
            parts.append(on if gn is None else on * gn)
        z = zg_ref[rows, :].astype(F32) * jnp.concatenate(parts, axis=1)
        x = x_ref[rows, :] + mod[2:3, :] * _dot(z.astype(BF16), wout_ref[...])
        xo_ref[rows, :] = x
        h2 = _rms_mod(x, nw_ref[...], mod[3:4, :], mod[4:5, :])
        h_hi = h2.astype(BF16)
        hb_ref[rows, :] = h_hi
        h_lo = (h2 - h_hi.astype(F32)).astype(BF16)
        both = _dot(h_hi, wra_ref[...])
        logits = both[:, :LANES] + both[:, LANES:] + _dot(h_lo, wrb_ref[...])
        lane = lax.broadcasted_iota(jnp.int32, logits.shape, 1)
        logits = jnp.where(lane < N_EXPERTS, logits, -1e30)
        e = jnp.exp(logits - jnp.max(logits, axis=-1, keepdims=True))
        aff_ref[rows, :] = e / jnp.sum(e, axis=-1, keepdims=True)


def _readout(o, zg, x, mods, w_out, g_norm, nw_ffn, w_router, n_heads, dv):
    affine = g_norm is not None
    if not affine:
        g_norm = jnp.ones((1, dv), F32)
    w_router_hi = w_router.astype(BF16)
    w_router_lo = (w_router - w_router_hi.astype(F32)).astype(BF16)
    w_router_hi_lo = jnp.concatenate([w_router_hi, w_router_lo], axis=1)
    bsz, t, d = x.shape
    hv = n_heads * dv
    tm = READOUT_SUBBLOCKS * TOKEN_BLOCK
    assert t % tm == 0
    tok = lambda b, i: (b, i, 0)
    full2 = lambda b, i: (0, 0)
    return pl.pallas_call(
        functools.partial(_readout_kernel, n_heads=n_heads, dv=dv, affine=affine),
        grid=(bsz, t // tm),
        in_specs=[
            pl.BlockSpec((None, tm, hv), tok),
            pl.BlockSpec((None, tm, hv), tok),
            pl.BlockSpec((None, tm, d), tok),
            pl.BlockSpec(mods.shape, lambda b, i: (0, 0, 0)),
            pl.BlockSpec((hv, d), full2),
            pl.BlockSpec((1, dv), full2),
            pl.BlockSpec((1, d), full2),
            pl.BlockSpec((d, 2 * LANES), full2),
            pl.BlockSpec((d, LANES), full2),
        ],
        out_specs=[
            pl.BlockSpec((None, tm, d), tok),
            pl.BlockSpec((None, tm, d), tok),
            pl.BlockSpec((None, tm, LANES), tok),
        ],
        out_shape=[
            jax.ShapeDtypeStruct((bsz, t, d), F32),
            jax.ShapeDtypeStruct((bsz, t, d), BF16),
            jax.ShapeDtypeStruct((bsz, t, LANES), F32),
        ],
        compiler_params=_cparams(("parallel", "parallel")),
    )(o, zg, x, mods, w_out, g_norm, nw_ffn, w_router_hi_lo, w_router_hi)


def _excl_cumsum_rows(x01, tri_strict):
    blk = tri_strict.shape[0]
    carry = jnp.zeros((1, x01.shape[1]), F32)
    out = []
    for r0 in range(0, x01.shape[0], blk):
        xb = x01[r0:r0 + blk, :]
        out.append(_dot(tri_strict, xb.astype(BF16)) + carry)
        carry = carry + jnp.sum(xb, axis=0, keepdims=True)
    return out[0] if len(out) == 1 else jnp.concatenate(out, axis=0)


def _route_kernel(aff_ref, packed_ref, tri_ref, pos_ref, pos_rows_ref, aff_rows_ref, starts_ref, *, segments):
    tri_strict = tri_ref[...]
    per_row = LANES // N_EXPERTS

    def expert_major(a):
        return a.T[0:N_EXPERTS, :]

    for r0, n, cap in segments:
        aff = aff_ref[r0:r0 + n, :]
        aff_rows_ref[:, r0:r0 + n] = expert_major(aff)
        if cap == 0:
            pos_ref[r0:r0 + n, :] = jnp.full((n, LANES), -1.0, F32)
            pos_rows_ref[:, r0:r0 + n] = jnp.full((N_EXPERTS, n), -1.0, F32)
            continue
        packed = packed_ref[r0 // per_row:(r0 + n) // per_row, :]

        def as_float(pattern):
            return lax.bitcast_convert_type(pattern, F32)

        def refine(thr, low_bit, n_bits):
            counts = []
            for digit in range(1, 1 << n_bits):
                cand = thr | jnp.left_shift(jnp.int32(digit), low_bit)
                counts.append(jnp.sum(jnp.where(packed >= as_float(cand), 1.0, 0.0), axis=0, keepdims=True))
            cnt = jnp.concatenate(counts + [jnp.zeros((1, LANES), F32)], axis=0)
            shift = LANES // 2
            while shift >= N_EXPERTS:
                cnt = cnt + pltpu.roll(cnt, shift, 1)
                shift //= 2
            digit = jnp.sum(jnp.where(cnt >= cap, 1.0, 0.0), axis=0, keepdims=True).astype(jnp.int32)
            return thr | jnp.left_shift(digit, low_bit)

        thr = refine(jnp.zeros((1, LANES), jnp.int32), 28, 3)
        thr = lax.fori_loop(0, 7, lambda i, th: refine(th, 24 - 4 * i, 4), thr)
        beyond = as_float(thr + 1)
        above = jnp.where(aff >= beyond, 1.0, 0.0)
        tied = jnp.where((aff >= as_float(thr)) & (aff < beyond), 1.0, 0.0)
        need = cap - jnp.sum(above, axis=0, keepdims=True)
        tie_rank = _excl_cumsum_rows(tied, tri_strict)
        sel = above + tied * jnp.where(tie_rank < need, 1.0, 0.0)
        before = _excl_cumsum_rows(sel, tri_strict)
        pos = jnp.where(sel > 0.0, before, -1.0)
        pos_ref[r0:r0 + n, :] = pos
        pos_rows_ref[:, r0:r0 + n] = expert_major(pos)
        if r0 == 0:
            marks = [before[i:i + 1, :] for i in range(0, n, TOKEN_BLOCK)] + [jnp.sum(sel, axis=0, keepdims=True)]
            marks += [jnp.zeros((1, LANES), F32)] * (starts_ref.shape[0] - len(marks))
            starts_ref[...] = jnp.concatenate(marks, axis=0)


def _route(aff, segments):
    bsz, t, _ = aff.shape
    r = np.arange(TOKEN_BLOCK)
    tri_strict = jnp.asarray((r[None, :] < r[:, None]).astype(np.float32), BF16)
    per_row = LANES // N_EXPERTS
    packed = aff[:, :, :N_EXPERTS].reshape(bsz, t // per_row, LANES)
    return pl.pallas_call(
        functools.partial(_route_kernel, segments=segments),
        grid=(bsz,),
        in_specs=[pl.BlockSpec((None, t, LANES), lambda b: (b, 0, 0)),
                  pl.BlockSpec((None, t // per_row, LANES), lambda b: (b, 0, 0)),
                  pl.BlockSpec(tri_strict.shape, lambda b: (0, 0))],
        out_specs=[pl.BlockSpec((None, t, LANES), lambda b: (b, 0, 0)),
                   pl.BlockSpec((None, N_EXPERTS, t), lambda b: (b, 0, 0)),
                   pl.BlockSpec((None, N_EXPERTS, t), lambda b: (b, 0, 0)),
                   pl.BlockSpec((None, 16, LANES), lambda b: (b, 0, 0))],
        out_shape=[jax.ShapeDtypeStruct((bsz, t, LANES), F32),
                   jax.ShapeDtypeStruct((bsz, N_EXPERTS, t), F32),
                   jax.ShapeDtypeStruct((bsz, N_EXPERTS, t), F32),
                   jax.ShapeDtypeStruct((bsz, 16, LANES), F32)],
        compiler_params=_cparams(("parallel",)),
    )(aff, packed, tri_strict)


def _onehot(cond):
    return jnp.where(cond, 1.0, 0.0).astype(BF16)


SLOT_WINDOW = 128
GATHER_GROUP = 4


def _slot_window(starts_ref, cap, blk, e):
    first = jnp.minimum((starts_ref[blk, e] >> 4) << 4, cap - SLOT_WINDOW)
    return first, starts_ref[blk + 1, e] <= first + SLOT_WINDOW


def _row_sum(a):
    folded = a[:, 0:LANES]
    for c0 in range(LANES, a.shape[1], LANES):
        folded = folded + a[:, c0:c0 + LANES]
    return jnp.sum(folded, axis=1, keepdims=True)


def _gather_kernel(starts_ref, hb_ref, aff_ref, pos_ref, xe_ref, gate_ref, *, segments):
    (_, n_lat, cap_lat), (ctx_r0, n_ctx, cap_ctx) = segments
    n_blocks = n_lat // TOKEN_BLOCK

    def pick(e, r0, n, slot0, cap):
        hit = pos_ref[e:e + 1, r0:r0 + n] == lax.broadcasted_iota(jnp.int32, (cap, n), 0).astype(F32)
        xe_ref[e, slot0:slot0 + cap, :] = _dot(_onehot(hit), hb_ref[r0:r0 + n, :]).astype(BF16)
        gate = _row_sum(jnp.where(hit, aff_ref[e:e + 1, r0:r0 + n], 0.0))
        gate_ref[e, slot0:slot0 + cap, :] = jnp.broadcast_to(gate, (cap, LANES))

    fits = jnp.bool_(True)
    for blk in range(n_blocks):
        for e in range(N_EXPERTS):
            fits = jnp.logical_and(fits, _slot_window(starts_ref, cap_lat, blk, e)[1])

    @pl.when(fits)
    def _():
        xe_ref[:, 0:cap_lat, :] = jnp.zeros((N_EXPERTS, cap_lat, xe_ref.shape[2]), BF16)
        gate_ref[:, 0:cap_lat, :] = jnp.zeros((N_EXPERTS, cap_lat, LANES), F32)
        slot = lax.broadcasted_iota(jnp.int32, (SLOT_WINDOW, TOKEN_BLOCK), 0).astype(F32)
        for blk in range(n_blocks):
            toks = slice(blk * TOKEN_BLOCK, (blk + 1) * TOKEN_BLOCK)
            for e0 in range(0, N_EXPERTS, GATHER_GROUP):
                group = range(e0, e0 + GATHER_GROUP)
                firsts = [_slot_window(starts_ref, cap_lat, blk, e)[0] for e in group]
                hits = [pos_ref[e:e + 1, toks] - first.astype(F32) == slot for e, first in zip(group, firsts)]
                part = _dot(jnp.concatenate([_onehot(hit) for hit in hits], axis=0), hb_ref[toks, :])
                for i, (e, first, hit) in enumerate(zip(group, firsts, hits)):
                    rows = pl.ds(pl.multiple_of(first, 16), SLOT_WINDOW)
                    xe_ref[e, rows, :] += part[i * SLOT_WINDOW:(i + 1) * SLOT_WINDOW, :].astype(BF16)
                    gate = _row_sum(jnp.where(hit, aff_ref[e:e + 1, toks], 0.0))
                    gate_ref[e, rows, :] += jnp.broadcast_to(gate, (SLOT_WINDOW, LANES))

    @pl.when(jnp.logical_not(fits))
    def _():
        for e in range(N_EXPERTS):
            pick(e, 0, n_lat, 0, cap_lat)

    if cap_ctx > 0:
        for e in range(N_EXPERTS):
            pick(e, ctx_r0, n_ctx, cap_lat, cap_ctx)


def _gather(hb, aff_rows, pos_rows, starts, segments):
    bsz, t, d = hb.shape
    rows = sum(cap for _, _, cap in segments)
    assert segments[0][2] >= SLOT_WINDOW and starts.shape[1] == segments[0][1] // TOKEN_BLOCK + 1
    per_sample = lambda b: (b, 0, 0)
    slots = lambda b: (0, b, 0)
    return pl.pallas_call(
        functools.partial(_gather_kernel, segments=segments),
        grid=(bsz,),
        in_specs=[pl.BlockSpec((None,) + starts.shape[1:], per_sample, memory_space=pltpu.SMEM),
                  pl.BlockSpec((None, t, d), per_sample),
                  pl.BlockSpec((None, N_EXPERTS, t), per_sample),
                  pl.BlockSpec((None, N_EXPERTS, t), per_sample)],
        out_specs=[pl.BlockSpec((N_EXPERTS, rows, d), slots), pl.BlockSpec((N_EXPERTS, rows, LANES), slots)],
        out_shape=[jax.ShapeDtypeStruct((N_EXPERTS, bsz * rows, d), BF16),
                   jax.ShapeDtypeStruct((N_EXPERTS, bsz * rows, LANES), F32)],
        compiler_params=_cparams(("parallel",)),
    )(starts, hb, aff_rows, pos_rows)


def _ffn_kernel(xe_ref, gate_ref, wg_ref, wu_ref, wd_ref, y_ref, acc_ref, *, row_chunk):
    f = pl.program_id(1)
    last = pl.num_programs(1) - 1

    def step(mode):
        wg = wg_ref[...].astype(BF16)
        wu = wu_ref[...].astype(BF16)
        wd = wd_ref[...].astype(BF16)
        for r0 in range(0, xe_ref.shape[0], row_chunk):
            rows = slice(r0, r0 + row_chunk)
            xe = xe_ref[rows, :]
            hidden = (_silu(_dot(xe, wg)) * _dot(xe, wu)).astype(BF16)
            part = _dot(hidden, wd)
            if mode == "first":
                acc_ref[rows, :] = part
            elif mode == "middle":
                acc_ref[rows, :] += part
            else:
                y_ref[rows, :] = ((acc_ref[rows, :] + part) * gate_ref[rows, 0:1]).astype(y_ref.dtype)

    pl.when(f == 0)(lambda: step("first"))
    pl.when((f > 0) & (f < last))(lambda: step("middle"))
    pl.when(f == last)(lambda: step("last"))


def _ffn(xe, gate, w_gate, w_up, w_down, layer):
    n_rows, d = xe.shape[1], xe.shape[2]
    ff = w_gate.shape[-1]
    row_chunk = n_rows // 2 if n_rows % 2 == 0 and (n_rows // 2) % 16 == 0 else n_rows
    return pl.pallas_call(
        functools.partial(_ffn_kernel, row_chunk=row_chunk),
        grid=(N_EXPERTS, ff // FF_TILE),
        in_specs=[
            pl.BlockSpec((None, n_rows, d), lambda e, f: (e, 0, 0)),
            pl.BlockSpec((None, n_rows, LANES), lambda e, f: (e, 0, 0)),
            pl.BlockSpec((None, None, d, FF_TILE), lambda e, f: (layer, e, 0, f)),
            pl.BlockSpec((None, None, d, FF_TILE), lambda e, f: (layer, e, 0, f)),
            pl.BlockSpec((None, None, FF_TILE, d), lambda e, f: (layer, e, f, 0)),
        ],
        out_specs=pl.BlockSpec((None, n_rows, d), lambda e, f: (e, 0, 0)),
        out_shape=jax.ShapeDtypeStruct((N_EXPERTS, n_rows, d), BF16),
        scratch_shapes=[pltpu.VMEM((n_rows, d), F32)],
        compiler_params=_cparams(("parallel", "arbitrary")),
    )(xe, gate, w_gate, w_up, w_down)


def _scatter_kernel(starts_ref, x_ref, y_ref, pos_ref, mods_ref, nw_ref, *rest, segments, final, with_next):
    if with_next:
        next_mods_ref, next_nw_ref, out_ref, h_ref = rest
    else:
        out_ref, = rest
    d = y_ref.shape[-1]
    ctx_row = mods_ref.shape[0] - 1
    n_sub = x_ref.shape[0] // TOKEN_BLOCK
    on_last_block = pl.program_id(1) == pl.num_programs(1) - 1
    cap_lat = segments[0][2]

    def window(sub_block, e):
        return _slot_window(starts_ref, cap_lat, pl.program_id(1) * n_sub + sub_block, e)

    def block(seg_id, rows, windowed=None):
        cap = segments[seg_id][2]
        slot0 = sum(seg[2] for seg in segments[:seg_id])
        mod_row = ctx_row if seg_id == 1 else pl.program_id(0)
        gate = mods_ref[mod_row][5:6, :]
        pos = pos_ref[rows, :]
        if windowed is None:
            y_all = y_ref[:, slot0:slot0 + cap, :].reshape(N_EXPERTS * cap, d)
            slot = lax.broadcasted_iota(jnp.int32, (TOKEN_BLOCK, cap), 1).astype(F32)
            onehots = jnp.concatenate([_onehot(pos[:, e:e + 1] == slot) for e in range(N_EXPERTS)], axis=1)
        else:
            slot = lax.broadcasted_iota(jnp.int32, (TOKEN_BLOCK, SLOT_WINDOW), 1).astype(F32)
            firsts = [window(windowed, e)[0] for e in range(N_EXPERTS)]
            y_all = jnp.concatenate(
                [y_ref[e, pl.ds(pl.multiple_of(slot0 + firsts[e], 16), SLOT_WINDOW), :] for e in range(N_EXPERTS)], axis=0)
            onehots = jnp.concatenate(
                [_onehot(pos[:, e:e + 1] - firsts[e].astype(F32) == slot) for e in range(N_EXPERTS)], axis=1)
        x = x_ref[rows, :] + gate * _dot(onehots, y_all)
        if final:
            x = x * lax.rsqrt(jnp.mean(x * x, axis=-1, keepdims=True) + EPS) * nw_ref[...]
        out_ref[rows, :] = x
        if with_next:
            m = next_mods_ref[mod_row]
            h_ref[rows, :] = _rms_mod(x, next_nw_ref[...], m[0:1, :], m[1:2, :]).astype(BF16)

    has_ctx = segments[1][2] > 0
    n_lat_blocks = segments[0][1] // TOKEN_BLOCK

    def step(use_windows):
        for s in range(n_sub):
            rows = slice(s * TOKEN_BLOCK, (s + 1) * TOKEN_BLOCK)
            lat = functools.partial(block, 0, rows, s if use_windows else None)
            if has_ctx and s == n_sub - 1:
                pl.when(jnp.logical_not(on_last_block))(lat)
                pl.when(on_last_block)(functools.partial(block, 1, rows))
            else:
                lat()

    fits = jnp.bool_(True)
    for s in range(n_sub):
        is_lat = pl.program_id(1) * n_sub + s < n_lat_blocks
        for e in range(N_EXPERTS):
            safe = window(jnp.where(is_lat, s, 0) if has_ctx and s == n_sub - 1 else s, e)[1]
            fits = jnp.logical_and(fits, jnp.logical_or(safe, jnp.logical_not(is_lat)))
    pl.when(fits)(lambda: step(True))
    pl.when(jnp.logical_not(fits))(lambda: step(False))


def _scatter(x, y, pos, starts, mods, nw_final, segments, final, next_norm=None):
    bsz, t, d = x.shape
    assert segments[0][2] >= SLOT_WINDOW and starts.shape[1] == segments[0][1] // TOKEN_BLOCK + 1
    n_slots = y.shape[1] // bsz
    out_rows = sum(n for _, n, cap in segments if cap > 0)
    tm = (3 if segments[1][2] > 0 else 2) * TOKEN_BLOCK
    assert out_rows % tm == 0
    tok = lambda b, i: (b, i, 0)
    const2, const3 = (lambda b, i: (0, 0)), (lambda b, i: (0, 0, 0))
    in_specs = [
        pl.BlockSpec((None,) + starts.shape[1:], lambda b, i: (b, 0, 0), memory_space=pltpu.SMEM),
        pl.BlockSpec((None, tm, d), tok),
        pl.BlockSpec((N_EXPERTS, n_slots, d), lambda b, i: (0, b, 0)),
        pl.BlockSpec((None, tm, LANES), tok),
        pl.BlockSpec(mods.shape, const3),
        pl.BlockSpec((1, d), const2),
    ]
    args = [starts, x, y, pos, mods, nw_final]
    out_specs = [pl.BlockSpec((None, tm, d), tok)]
    out_shape = [jax.ShapeDtypeStruct((bsz, out_rows, d), F32)]
    if next_norm is not None:
        in_specs += [pl.BlockSpec(next_norm[0].shape, const3), pl.BlockSpec((1, d), const2)]
        args += list(next_norm)
        out_specs.append(pl.BlockSpec((None, tm, d), tok))
        out_shape.append(jax.ShapeDtypeStruct((bsz, out_rows, d), BF16))
    return pl.pallas_call(
        functools.partial(_scatter_kernel, segments=segments, final=final, with_next=next_norm is not None),
        grid=(bsz, out_rows // tm),
        in_specs=in_specs, out_specs=out_specs, out_shape=out_shape,
        compiler_params=_cparams(("parallel", "parallel")),
    )(*args)


def _moe(x, hb, aff, mods, w_gate, w_up, w_down, layer, nw_final, with_ctx, final, next_norm):
    bsz, t, d = x.shape
    seq = t - TOKEN_BLOCK
    cap_lat = EC_CAPACITY * seq // N_EXPERTS
    cap_ctx = EC_CAPACITY * TOKEN_BLOCK // N_EXPERTS if with_ctx else 0
    segments = ((0, seq, cap_lat), (seq, TOKEN_BLOCK, cap_ctx))
    pos, pos_rows, aff_rows, starts = _route(aff, segments)
    starts = starts[:, :seq // TOKEN_BLOCK + 1, :N_EXPERTS].astype(jnp.int32)
    xe, gate = _gather(hb, aff_rows, pos_rows, starts, segments)
    y = _ffn(xe, gate, w_gate, w_up, w_down, layer)
    return _scatter(x, y, pos, starts, mods, nw_final, segments, final, next_norm)


def _rope_tables(seq, n_ctx):
    half = RET_DK // 2
    t = jnp.arange(seq)
    inv = ROPE_THETA ** (-jnp.arange(0, half, 2, dtype=F32) / half)
    tabs = []
    for pos in ((t // GRID_W).astype(F32), (t % GRID_W).astype(F32)):
        ang = pos[:, None] * inv
        cos, sin = jnp.cos(ang), jnp.sin(ang)
        tabs.append(jnp.concatenate([cos, cos], axis=1))
        tabs.append(jnp.concatenate([-sin, sin], axis=1))
    ident = [jnp.ones, jnp.zeros, jnp.ones, jnp.zeros]
    return [jnp.concatenate([tab, fn((n_ctx, 2 * (half // 2)), F32)], axis=0) for tab, fn in zip(tabs, ident)]


def kernel(x, c, ctx, c_ctx, w_ada, b_ada, norm_mix, norm_ffn, ret_w_in, ret_w_out, hg_w_in, hg_g_norm,
           hg_lower_bounds, hg_w_out, moe_router, moe_w_gate, moe_w_up, moe_w_down, norm_final):
    bsz, seq, d = x.shape
    n_ctx = ctx.shape[1]
    depth = w_ada.shape[0]
    assert n_ctx == TOKEN_BLOCK and seq % TOKEN_BLOCK == 0 and d == RET_HEADS * RET_DK == HG_HEADS * HG_DK

    cc = jnp.concatenate([c, c_ctx[None, :], jnp.zeros((16 - bsz - 1, d), F32)], axis=0)
    mods_all = _ada(cc, w_ada, b_ada)[:, :bsz + 1].reshape(depth, bsz + 1, 6, d)
    t = seq + n_ctx
    rope = _rope_tables(seq, n_ctx)
    rope_specs = [pl.BlockSpec((t // 2, LANES), lambda j, b, i: (i, 0))] * 4
    w_router = jnp.pad(moe_router, ((0, 0), (0, 0), (0, LANES - N_EXPERTS)))

    xs, h = _prenorm(x, ctx, mods_all[0], norm_mix[0][None, :])
    for layer in range(depth):
        last = layer == depth - 1
        slot = layer // 2
        mods = mods_all[layer]
        if layer % 2 == 0:
            assert RET_HEADS * RET_DK == PROJ_COLS
            w = ret_w_in[slot]
            assert np.log2(RET_DK ** -0.5) % 1 == 0
            q, = _proj(h, w, 0, 1, _ep_rope, [BF16], rope, rope_specs)
            k, = _proj(h, w, 1, 1, functools.partial(_ep_rope, scale=RET_DK ** -0.5), [BF16], rope, rope_specs)
            v, = _proj(h, w, 2, 2, _ep_plain, [BF16])
            zg, = _proj(h, w, 4, 2, _ep_silu, [BF16])
            o = _retention(q, k, v)
            w_out, g_norm, n_heads, dv = ret_w_out[slot].astype(BF16), None, RET_HEADS, RET_DV
        else:
            assert HG_HEADS * HG_DK == PROJ_COLS
            w = hg_w_in[slot]
            lb_specs = [pl.BlockSpec((depth, PROJ_COLS), lambda j, b, i: (0, 0))]
            ep_gate = functools.partial(_ep_gate, layer=layer)
            q, = _proj(h, w, 0, 1, _ep_silu, [BF16])
            kf, gf = _proj(h, w, 1, 1, ep_gate, [F32, F32], (hg_lower_bounds[:, 0, :],), lb_specs)
            kb, gb = _proj(h, w, 2, 1, ep_gate, [F32, F32], (hg_lower_bounds[:, 1, :],), lb_specs)
            v, = _proj(h, w, 3, 1, _ep_plain, [BF16])
            zg, = _proj(h, w, 4, 1, _ep_silu, [BF16])
            o = _hgrn2(q, kf, kb, gf, gb, v)
            w_out, g_norm, n_heads, dv = hg_w_out[slot].astype(BF16), hg_g_norm[slot][None, :], HG_HEADS, HG_DV
        xs, hb, aff = _readout(o, zg, xs, mods, w_out, g_norm, norm_ffn[layer][None, :], w_router[layer],
                                       n_heads, dv)
        next_norm = None if last else (mods_all[layer + 1], norm_mix[layer + 1][None, :])
        res = _moe(xs, hb, aff, mods, moe_w_gate, moe_w_up, moe_w_down, layer, norm_final[None, :],
                   with_ctx=not last, final=last, next_norm=next_norm)
        xs, h = res if next_norm is not None else (res[0], None)
    return xs
```

```python
import functools

import numpy as np
import jax
import jax.numpy as jnp
from jax import lax
from jax.experimental import pallas as pl
from jax.experimental.pallas import tpu as pltpu

F32 = jnp.float32
BF16 = jnp.bfloat16
EPS = 1e-6
ROPE_THETA = 10000.0
GRID_W = 64

LANES = 128
TOKEN_BLOCK = 256
RET_HEADS, RET_DK, RET_DV, RET_CHUNK = 4, 256, 512, 256
HG_HEADS, HG_DK, HG_DV, HG_CHUNK, HG_SUB = 8, 128, 128, 128, 8
HG_WIDE = HG_CHUNK
HG_HEADS_PER_STEP = 2
HG_FAST_SPAN = 96.0
READOUT_SUBBLOCKS = 3
LOG2_E = 1.4426950408889634
N_EXPERTS = 16
EC_CAPACITY = 2
FF_TILE = 256
PROJ_COLS = 1024
VMEM_LIMIT = 56 * 1024 * 1024

HIGHEST = lax.Precision.HIGHEST


def _cparams(sem):
    return pltpu.CompilerParams(dimension_semantics=sem, vmem_limit_bytes=VMEM_LIMIT)


def _dot(a, b, **kw):
    return jnp.dot(a, b, preferred_element_type=F32, **kw)


def _dot_nt(a, b):
    return lax.dot_general(a, b, (((1,), (1,)), ((), ())), preferred_element_type=F32)


def _silu(x):
    return x * jax.nn.sigmoid(x)


def _rms_mod(x, nw, shift, scale):
    ms = jnp.mean(x * x, axis=-1, keepdims=True)
    h = x * lax.rsqrt(ms + EPS) * nw
    return h * (1.0 + scale) + shift


def _ada_kernel(c_ref, w_ref, b_ref, o_ref):
    o_ref[...] = _dot(_silu(c_ref[...]), w_ref[...], precision=HIGHEST) + b_ref[...]


def _ada(cc, w_ada, b_ada):
    depth, d, d6 = w_ada.shape
    rows = cc.shape[0]
    return pl.pallas_call(
        _ada_kernel,
        grid=(depth, d6 // d),
        in_specs=[
            pl.BlockSpec((rows, d), lambda l, j: (0, 0)),
            pl.BlockSpec((None, d, d), lambda l, j: (l, 0, j)),
            pl.BlockSpec((None, 1, d), lambda l, j: (l, 0, j)),
        ],
        out_specs=pl.BlockSpec((None, rows, d), lambda l, j: (l, 0, j)),
        out_shape=jax.ShapeDtypeStruct((depth, rows, d6), F32),
        compiler_params=_cparams(("parallel", "parallel")),
    )(cc, w_ada, b_ada.reshape(depth, 1, d6))


def _prenorm_kernel(x_ref, ctx_ref, mods_ref, nw_ref, xs_ref, h_ref):
    ctx_row = mods_ref.shape[0] - 1
    nw = nw_ref[...]
    seq = x_ref.shape[0]
    for src_ref, r0, row in ((x_ref, 0, pl.program_id(0)), (ctx_ref, seq, ctx_row)):
        m = mods_ref[row]
        shift, scale = m[0:1, :], m[1:2, :]

        def body(i, carry):
            src = pl.ds(pl.multiple_of(i * TOKEN_BLOCK, TOKEN_BLOCK), TOKEN_BLOCK)
            dst = pl.ds(pl.multiple_of(r0 + i * TOKEN_BLOCK, TOKEN_BLOCK), TOKEN_BLOCK)
            x = src_ref[src, :]
            xs_ref[dst, :] = x
            h_ref[dst, :] = _rms_mod(x, nw, shift, scale).astype(BF16)
            return carry

        lax.fori_loop(0, src_ref.shape[0] // TOKEN_BLOCK, body, 0)


def _prenorm(x, ctx, mods, nw):
    bsz, seq, d = x.shape
    n_ctx = ctx.shape[1]
    t = seq + n_ctx
    return pl.pallas_call(
        _prenorm_kernel,
        grid=(bsz,),
        in_specs=[pl.BlockSpec((None, seq, d), lambda b: (b, 0, 0)),
                  pl.BlockSpec((None, n_ctx, d), lambda b: (b, 0, 0)),
                  pl.BlockSpec(mods.shape, lambda b: (0, 0, 0)),
                  pl.BlockSpec((1, d), lambda b: (0, 0))],
        out_specs=[pl.BlockSpec((None, t, d), lambda b: (b, 0, 0)), pl.BlockSpec((None, t, d), lambda b: (b, 0, 0))],
        out_shape=[jax.ShapeDtypeStruct((bsz, t, d), F32), jax.ShapeDtypeStruct((bsz, t, d), BF16)],
        compiler_params=_cparams(("parallel",)),
    )(x, ctx, mods, nw)


def _proj_kernel(h_ref, w_ref, *rest, epilogue, n_extra):
    extras, outs = rest[:n_extra], rest[n_extra:]
    epilogue(_dot(h_ref[...], w_ref[...].astype(BF16)), extras, outs)


def _ep_plain(acc, extras, outs):
    outs[0][...] = acc.astype(outs[0].dtype)


def _ep_silu(acc, extras, outs):
    outs[0][...] = _silu(acc).astype(outs[0].dtype)


def _ep_rope(acc, extras, outs, *, key_scale):
    c_row, s_row, c_col, s_col = (r[...] for r in extras)
    scale = jnp.where(pl.program_id(0) == 1, key_scale, 1.0).astype(F32)
    for s in range(acc.shape[1] // LANES):
        slab = acc[:, s * LANES:(s + 1) * LANES] * scale
        cos, sin = (c_row, s_row) if s % 2 == 0 else (c_col, s_col)
        rot = slab * cos + pltpu.roll(slab, LANES // 2, 1) * sin
        outs[0][:, s * LANES:(s + 1) * LANES] = rot.astype(outs[0].dtype)


def _ep_gate(acc, extras, outs, *, layer):
    raw = extras[0][...]
    e = jnp.exp(raw - jnp.max(raw, axis=0, keepdims=True))
    sm = e / jnp.sum(e, axis=0, keepdims=True)
    lb = -sm[0:1, :]
    for l in range(layer + 1):
        lb = lb + sm[l:l + 1, :]
    f = lb + (1.0 - lb) * jax.nn.sigmoid(acc)
    outs[0][...] = 1.0 - f
    outs[1][...] = jnp.log(f)


def _proj(h, w, col_block0, ncol, epilogue, out_dtypes, extras=(), extra_specs=()):
    bsz, t, d = h.shape
    tm, tn = t // 2, PROJ_COLS
    in_specs = [
        pl.BlockSpec((None, tm, d), lambda j, b, i: (b, i, 0)),
        pl.BlockSpec((d, tn), lambda j, b, i: (0, col_block0 + j)),
    ] + list(extra_specs)
    out_specs = [pl.BlockSpec((None, tm, tn), lambda j, b, i: (b, i, j)) for _ in out_dtypes]
    out_shape = [jax.ShapeDtypeStruct((bsz, t, tn * ncol), dt) for dt in out_dtypes]
    return pl.pallas_call(
        functools.partial(_proj_kernel, epilogue=epilogue, n_extra=len(extras)),
        grid=(ncol, bsz, 2), in_specs=in_specs, out_specs=out_specs, out_shape=out_shape,
        compiler_params=_cparams(("parallel", "parallel", "parallel")),
    )(h, w, *extras)


def _ret_tables():
    c = RET_CHUNK
    j = np.arange(2 * RET_HEADS, dtype=np.float64)
    lg = np.log1p(-np.exp2(-5.0 - j / 2))
    lg_f, lg_b = lg[0::2], lg[1::2]
    i = np.arange(c, dtype=np.float64)
    diff = i[:, None] - i[None, :]
    mats = np.zeros((RET_HEADS, c, c), np.float64)
    tabs = np.zeros((RET_HEADS, c, LANES), np.float64)
    for h in range(RET_HEADS):
        mats[h] = np.where(diff >= 0, np.exp(diff * lg_f[h]), 0.0) + np.where(diff <= 0, np.exp(-diff * lg_b[h]), 0.0)
        tabs[h, :, 0] = np.exp((i + 1) * lg_f[h])
        tabs[h, :, 1] = np.exp((c - 1 - i) * lg_f[h])
        tabs[h, :, 2] = np.exp((c - i) * lg_b[h])
        tabs[h, :, 3] = np.exp(i * lg_b[h])
        tabs[h, :, 4] = np.exp(c * lg_f[h])
        tabs[h, :, 5] = np.exp(c * lg_b[h])
    return jnp.asarray(mats, F32), jnp.asarray(tabs, F32)


def _ret_kernel(q_ref, k_ref, v_ref, m_ref, tab_ref, out_ref, o_ref, vt_ref, sf_ref, sb_ref, *, n_chunks):
    c = RET_CHUNK
    decay = m_ref[...]
    tab = tab_ref[...]
    qf, kfs, qb, kbs = tab[:, 0:1], tab[:, 1:2], tab[:, 2:3], tab[:, 3:4]
    end_f, end_b = tab[0:1, 4:5], tab[0:1, 5:6]

    def rows(ci):
        return slice(ci * c, (ci + 1) * c)

    for ci in range(n_chunks):
        q, k, v = q_ref[rows(ci), :], k_ref[rows(ci), :], v_ref[rows(ci), :]
        scores = (_dot_nt(q, k) * decay).astype(BF16)
        o_ref[rows(ci), :] = _dot(scores, v)
        vt_ref[ci] = v.astype(F32).T.astype(BF16)

    ctx = n_chunks - 1
    order_f = [ctx] + list(range(ctx))
    order_b = [ctx] + list(range(ctx - 1, -1, -1))
    sf_ref[...] = jnp.zeros_like(sf_ref)
    sb_ref[...] = jnp.zeros_like(sb_ref)
    for step in range(n_chunks):
        for order, s_ref, qd, kd, end in ((order_f, sf_ref, qf, kfs, end_f), (order_b, sb_ref, qb, kbs, end_b)):
            ci = order[step]
            if step > 0:
                o_ref[rows(ci), :] += _dot_nt(q_ref[rows(ci), :], s_ref[...].astype(BF16)) * qd
            if step < n_chunks - 1:
                ks = (k_ref[rows(ci), :].astype(F32) * kd).astype(BF16)
                upd = _dot(vt_ref[ci], ks)
                s_ref[...] = upd if step == 0 else s_ref[...] * end + upd
    for ci in range(n_chunks):
        out_ref[rows(ci), :] = o_ref[rows(ci), :].astype(out_ref.dtype)


def _retention(qk, v):
    bsz, t, _ = qk.shape
    n_chunks = t // RET_CHUNK
    mats, tabs = _ret_tables()
    return pl.pallas_call(
        functools.partial(_ret_kernel, n_chunks=n_chunks),
        grid=(bsz, RET_HEADS),
        in_specs=[
            pl.BlockSpec((None, t, RET_DK), lambda b, h: (b, 0, h)),
            pl.BlockSpec((None, t, RET_DK), lambda b, h: (b, 0, RET_HEADS + h)),
            pl.BlockSpec((None, t, RET_DV), lambda b, h: (b, 0, h)),
            pl.BlockSpec((None, RET_CHUNK, RET_CHUNK), lambda b, h: (h, 0, 0)),
            pl.BlockSpec((None, RET_CHUNK, LANES), lambda b, h: (h, 0, 0)),
        ],
        out_specs=pl.BlockSpec((None, t, RET_DV), lambda b, h: (b, 0, h)),
        out_shape=jax.ShapeDtypeStruct((bsz, t, RET_HEADS * RET_DV), BF16),
        scratch_shapes=[
            pltpu.VMEM((t, RET_DV), F32),
            pltpu.VMEM((n_chunks, RET_DV, RET_CHUNK), BF16),
            pltpu.VMEM((RET_DV, RET_DK), F32),
            pltpu.VMEM((RET_DV, RET_DK), F32),
        ],
        compiler_params=_cparams(("parallel", "parallel")),
    )(qk, qk, v, mats, tabs)


def _hg_levels(block):
    return [HG_CHUNK >> (i + 1) for i in range(int(np.log2(HG_CHUNK // block)))]


def _hg_masks():
    c = HG_CHUNK
    r = np.arange(c)[:, None]
    col = np.arange(c)[None, :]
    tables = {}
    for lower in (True, False):
        for s in _hg_levels(HG_SUB):
            same = (r // (2 * s)) == (col // (2 * s))
            rh, ch = (r // s) % 2, (col // s) % 2
            tables["level", lower, s] = same & ((rh == 1) & (ch == 0) if lower else (rh == 0) & (ch == 1))
            far = (rh == 1) if lower else (rh == 0)
            tables["sign", lower, s] = np.broadcast_to(np.where(far, 1.0, -1.0), (c, c))
        for block in (HG_SUB, HG_WIDE):
            tables["diag", lower, block] = ((r // block) == (col // block)) & ((col <= r) if lower else (col >= r))
    index = {name: i for i, name in enumerate(tables)}
    return index, np.stack([np.asarray(t, np.float32) for t in tables.values()])


def _hg_tables():
    c, sub = HG_CHUNK, HG_SUB
    r = np.arange(c)[:, None]
    col = np.arange(c)[None, :]
    tri = np.stack([col <= r, col >= r]).astype(np.float32)
    place = np.stack([np.broadcast_to((np.arange(c) % sub) == j, (c, c)) for j in range(sub)]).astype(np.float32)
    return jnp.asarray(_hg_masks()[1]), jnp.asarray(tri, BF16), jnp.asarray(place, BF16)


def _split3(g):
    hi = g.astype(BF16)
    r1 = g - hi.astype(F32)
    mid = r1.astype(BF16)
    lo = (r1 - mid.astype(F32)).astype(BF16)
    return hi, mid, lo


def _hg_intra(q, k, b, masks, place, lower, block, pairwise=False):
    c, sub = HG_CHUNK, HG_SUB
    index = _hg_masks()[0]
    total = jnp.zeros((c, c), F32)
    for s in _hg_levels(block):
        pieces = []
        for m in range(0, c, 2 * s):
            r = m + s - 1 if lower else m + s
            pieces.append(jnp.broadcast_to(b[r:r + 1, :], (2 * s, b.shape[1])))
        ref = pieces[0] if len(pieces) == 1 else jnp.concatenate(pieces, axis=0)
        w = jnp.exp2((b - ref) * masks[index["sign", lower, s]])
        total = total + _dot_nt((q * w).astype(BF16), (k * w).astype(BF16)) * masks[index["level", lower, s]]
    diag_mask = masks[index["diag", lower, block]]
    if not pairwise:
        firsts = [u * block if lower else u * block + block - 1 for u in range(c // block)]
        ref = jnp.concatenate([jnp.broadcast_to(b[r:r + 1, :], (block, b.shape[1])) for r in firsts], axis=0)
        d = b - ref
        p = _dot_nt((q * jnp.exp2(d)).astype(BF16), (k * jnp.exp2(-d)).astype(BF16))
        return total + p * diag_mask, jnp.max(-d)
    assert block == sub
    shape3 = (c // sub, sub, q.shape[1])
    q3, b3 = q.reshape(shape3), b.reshape(shape3)
    a3 = (jnp.log2(jnp.maximum(k, 0.0)) - b).reshape(shape3)
    acc = jnp.zeros((c, c), F32)
    for jj in range(sub):
        aj = jnp.broadcast_to(a3[:, jj:jj + 1, :], shape3)
        p = q3 * jnp.exp2(jnp.minimum(b3 + aj, 0.0))
        acc = acc + _dot(p.reshape(c, q.shape[1]).astype(BF16), place[jj])
    return total + acc * diag_mask, None


def _hg_kernel(q_ref, kf_ref, kb_ref, gf_ref, gb_ref, v_ref, mask_ref, tri_ref, place_ref, out_ref,
               o_ref, bf_ref, bb_ref, vt_ref, s_ref, *, n_chunks):
    c = HG_CHUNK
    masks = [mask_ref[i] for i in range(mask_ref.shape[0])]
    place = [place_ref[i] for i in range(place_ref.shape[0])]
    tri_lo, tri_up = tri_ref[0], tri_ref[1]

    def cumsum(tri, g):
        hi, mid, lo = _split3(g)
        return (_dot(tri, hi) + _dot(tri, mid) + _dot(tri, lo)) * LOG2_E

    def chunk_rows(ci):
        return pl.ds(ci * c if isinstance(ci, int) else pl.multiple_of(ci * c, c), c)

    slabs = [slice(hh * HG_DK, (hh + 1) * HG_DK) for hh in range(HG_HEADS_PER_STEP)]

    def finish(ci, hh, scores, v):
        o_ref[chunk_rows(ci), slabs[hh]] = _dot(scores, v)
        vt_ref[hh, ci] = v.astype(F32).T.astype(BF16)

    first = chunk_rows(0)
    bf_ref[first, :] = jnp.concatenate([cumsum(tri_lo, gf_ref[first, sl]) for sl in slabs], axis=1)
    bb_ref[first, :] = jnp.concatenate([cumsum(tri_up, gb_ref[first, sl]) for sl in slabs], axis=1)
    s_ref[first, :] = jnp.zeros((c, s_ref.shape[1]), BF16)

    def intra(ci, carry):
        prev, nxt = jnp.maximum(ci - 1, 0), jnp.minimum(ci + 1, n_chunks - 1)
        rows, rows_p, rows_n = chunk_rows(ci), chunk_rows(prev), chunk_rows(nxt)
        loaded = []
        for sl in slabs:
            loaded.append((bf_ref[rows, sl], bb_ref[rows, sl], q_ref[rows, sl].astype(F32), kf_ref[rows, sl],
                           kb_ref[rows, sl], s_ref[rows_p, sl], v_ref[rows_p, sl], gf_ref[rows_n, sl],
                           gb_ref[rows_n, sl]))
        span = 0.0
        for hh, (bf, bb, q, kf, kb, s_prev, v_prev, gf_n, gb_n) in enumerate(loaded):
            finish(prev, hh, s_prev, v_prev)
            bf_ref[rows_n, slabs[hh]] = cumsum(tri_lo, gf_n)
            bb_ref[rows_n, slabs[hh]] = cumsum(tri_up, gb_n)
            lo, span_lo = _hg_intra(q, kf, bf, masks, place, True, HG_WIDE)
            up, span_up = _hg_intra(q, kb, bb, masks, place, False, HG_WIDE)
            s_ref[rows, slabs[hh]] = (lo + up).astype(BF16)
            span = jnp.maximum(span, jnp.maximum(span_lo, span_up))

        def redo(block, pairwise):
            worst = 0.0
            for sl in slabs:
                q = q_ref[rows, sl].astype(F32)
                lo, span_lo = _hg_intra(q, kf_ref[rows, sl], bf_ref[rows, sl], masks, place, True, block, pairwise)
                up, span_up = _hg_intra(q, kb_ref[rows, sl], bb_ref[rows, sl], masks, place, False, block, pairwise)
                s_ref[rows, sl] = (lo + up).astype(BF16)
                if not pairwise:
                    worst = jnp.maximum(worst, jnp.maximum(span_lo, span_up))
            return worst

        @pl.when(span > HG_FAST_SPAN)
        def _():
            @pl.when(redo(HG_SUB, False) > HG_FAST_SPAN)
            def _():
                redo(HG_SUB, True)

        return carry

    lax.fori_loop(0, n_chunks, intra, 0)
    last = chunk_rows(n_chunks - 1)
    for hh, sl in enumerate(slabs):
        finish(n_chunks - 1, hh, s_ref[last, sl], v_ref[last, sl])

    n_ctx = TOKEN_BLOCK // c
    n_lat = n_chunks - n_ctx

    assert all((n_lat + s if s < n_ctx else s - n_ctx) != n_chunks - 1 - s for s in range(n_chunks))

    def scan(step, carry):
        cf = jnp.where(step < n_ctx, n_lat + step, step - n_ctx)
        cb = n_chunks - 1 - step
        jobs = []
        for hh, sl in enumerate(slabs):
            for ci, k_ref, b_ref, end_row, st in ((cf, kf_ref, bf_ref, c - 1, carry[2 * hh]),
                                                  (cb, kb_ref, bb_ref, 0, carry[2 * hh + 1])):
                rows = chunk_rows(ci)
                jobs.append((rows, sl, q_ref[rows, sl].astype(F32), b_ref[rows, sl], k_ref[rows, sl], vt_ref[hh, ci],
                             o_ref[rows, sl], st, end_row))
        states, updates = [], []
        for rows, sl, q, b, k, vt, o_old, st, end_row in jobs:
            qs = (q * jnp.exp2(b)).astype(BF16)
            updates.append((rows, sl, o_old + _dot_nt(qs, st.astype(BF16))))
            b_end = b[end_row:end_row + 1, :]
            ks = (k * jnp.exp2(b_end - b)).astype(BF16)
            states.append(st * jnp.exp2(b_end) + _dot(vt, ks))
        for rows, sl, o_new in updates:
            o_ref[rows, sl] = o_new
        return tuple(states)

    zero = jnp.zeros((HG_DV, HG_DK), F32)
    lax.fori_loop(0, n_chunks, scan, (zero,) * (2 * HG_HEADS_PER_STEP))

    def emit(ci, carry):
        out_ref[chunk_rows(ci), :] = o_ref[chunk_rows(ci), :].astype(out_ref.dtype)
        return carry

    lax.fori_loop(0, n_chunks, emit, 0)


def _hgrn2(q, k_fb, g_fb, v):
    bsz, t, _ = q.shape
    n_chunks = t // HG_CHUNK
    hps = HG_HEADS_PER_STEP
    masks, tri, place = _hg_tables()
    head = lambda b, h: (b, 0, h)
    spec = pl.BlockSpec((None, t, hps * HG_DK), head)
    backward = pl.BlockSpec((None, t, hps * HG_DK), lambda b, h: (b, 0, HG_HEADS // hps + h))
    return pl.pallas_call(
        functools.partial(_hg_kernel, n_chunks=n_chunks),
        grid=(bsz, HG_HEADS // hps),
        in_specs=[spec, spec, backward, spec, backward, spec,
                  pl.BlockSpec(masks.shape, lambda b, h: (0, 0, 0)),
                  pl.BlockSpec(tri.shape, lambda b, h: (0, 0, 0)),
                  pl.BlockSpec(place.shape, lambda b, h: (0, 0, 0))],
        out_specs=pl.BlockSpec((None, t, hps * HG_DV), head),
        out_shape=jax.ShapeDtypeStruct((bsz, t, HG_HEADS * HG_DV), BF16),
        scratch_shapes=[
            pltpu.VMEM((t, hps * HG_DV), F32),
            pltpu.VMEM((t, hps * HG_DK), F32),
            pltpu.VMEM((t, hps * HG_DK), F32),
            pltpu.VMEM((hps, n_chunks, HG_DV, HG_CHUNK), BF16),
            pltpu.VMEM((t, hps * HG_CHUNK), BF16),
        ],
        compiler_params=_cparams(("parallel", "parallel")),
    )(q, k_fb, k_fb, g_fb, g_fb, v, masks, tri, place)


def _readout_kernel(o_ref, zg_ref, x_ref, mods_ref, wout_ref, gn_ref, nw_ref, wra_ref, wrb_ref,
                    xo_ref, hb_ref, aff_ref, *, n_heads, dv, affine):
    gn = gn_ref[...] if affine else None
    n_sub = x_ref.shape[0] // TOKEN_BLOCK
    ctx_row = mods_ref.shape[0] - 1
    on_last_block = pl.program_id(1) == pl.num_programs(1) - 1
    for s in range(n_sub):
        rows = slice(s * TOKEN_BLOCK, (s + 1) * TOKEN_BLOCK)
        mod_row = jnp.where(on_last_block, ctx_row, pl.program_id(0)) if s == n_sub - 1 else pl.program_id(0)
        mod = mods_ref[mod_row]
        o = o_ref[rows, :].astype(F32)
        parts = []
        for h in range(n_heads):
            oh = o[:, h * dv:(h + 1) * dv]
            ms = jnp.mean(oh * oh, axis=-1, keepdims=True)
            on = oh * lax.rsqrt(ms + EPS)
            parts.append(on if gn is None else on * gn)
        z = zg_ref[rows, :].astype(F32) * jnp.concatenate(parts, axis=1)
        x = x_ref[rows, :] + mod[2:3, :] * _dot(z.astype(BF16), wout_ref[...])
        xo_ref[rows, :] = x
        h2 = _rms_mod(x, nw_ref[...], mod[3:4, :], mod[4:5, :])
        h_hi = h2.astype(BF16)
        hb_ref[rows, :] = h_hi
        h_lo = (h2 - h_hi.astype(F32)).astype(BF16)
        both = _dot(h_hi, wra_ref[...])
        logits = both[:, :LANES] + both[:, LANES:] + _dot(h_lo, wrb_ref[...])
        lane = lax.broadcasted_iota(jnp.int32, logits.shape, 1)
        logits = jnp.where(lane < N_EXPERTS, logits, -1e30)
        e = jnp.exp(logits - jnp.max(logits, axis=-1, keepdims=True))
        aff_ref[rows, :] = e / jnp.sum(e, axis=-1, keepdims=True)


def _readout(o, zg, x, mods, w_out, g_norm, nw_ffn, w_router, n_heads, dv):
    affine = g_norm is not None
    if not affine:
        g_norm = jnp.ones((1, dv), F32)
    w_router_hi = w_router.astype(BF16)
    w_router_lo = (w_router - w_router_hi.astype(F32)).astype(BF16)
    w_router_hi_lo = jnp.concatenate([w_router_hi, w_router_lo], axis=1)
    bsz, t, d = x.shape
    hv = n_heads * dv
    tm = READOUT_SUBBLOCKS * TOKEN_BLOCK
    assert t % tm == 0
    tok = lambda b, i: (b, i, 0)
    full2 = lambda b, i: (0, 0)
    return pl.pallas_call(
        functools.partial(_readout_kernel, n_heads=n_heads, dv=dv, affine=affine),
        grid=(bsz, t // tm),
        in_specs=[
            pl.BlockSpec((None, tm, hv), tok),
            pl.BlockSpec((None, tm, hv), tok),
            pl.BlockSpec((None, tm, d), tok),
            pl.BlockSpec(mods.shape, lambda b, i: (0, 0, 0)),
            pl.BlockSpec((hv, d), full2),
            pl.BlockSpec((1, dv), full2),
            pl.BlockSpec((1, d), full2),
            pl.BlockSpec((d, 2 * LANES), full2),
            pl.BlockSpec((d, LANES), full2),
        ],
        out_specs=[
            pl.BlockSpec((None, tm, d), tok),
            pl.BlockSpec((None, tm, d), tok),
            pl.BlockSpec((None, tm, LANES), tok),
        ],
        out_shape=[
            jax.ShapeDtypeStruct((bsz, t, d), F32),
            jax.ShapeDtypeStruct((bsz, t, d), BF16),
            jax.ShapeDtypeStruct((bsz, t, LANES), F32),
        ],
        compiler_params=_cparams(("parallel", "parallel")),
    )(o, zg, x, mods, w_out, g_norm, nw_ffn, w_router_hi_lo, w_router_hi)


def _excl_cumsum_rows(x01, tri_strict):
    blk = tri_strict.shape[0]
    carry = jnp.zeros((1, x01.shape[1]), F32)
    out = []
    for r0 in range(0, x01.shape[0], blk):
        xb = x01[r0:r0 + blk, :]
        out.append(_dot(tri_strict, xb.astype(BF16)) + carry)
        carry = carry + jnp.sum(xb, axis=0, keepdims=True)
    return out[0] if len(out) == 1 else jnp.concatenate(out, axis=0)


def _route_kernel(aff_ref, packed_ref, tri_ref, pos_ref, pos_rows_ref, aff_rows_ref, starts_ref, *, segments):
    tri_strict = tri_ref[...]
    per_row = LANES // N_EXPERTS

    def expert_major(a):
        return a.T[0:N_EXPERTS, :]

    for r0, n, cap in segments:
        aff = aff_ref[r0:r0 + n, :]
        aff_rows_ref[:, r0:r0 + n] = expert_major(aff)
        if cap == 0:
            pos_ref[r0:r0 + n, :] = jnp.full((n, LANES), -1.0, F32)
            pos_rows_ref[:, r0:r0 + n] = jnp.full((N_EXPERTS, n), -1.0, F32)
            continue
        packed = packed_ref[r0 // per_row:(r0 + n) // per_row, :]

        def as_float(pattern):
            return lax.bitcast_convert_type(pattern, F32)

        def refine(thr, low_bit, n_bits):
            counts = []
            for digit in range(1, 1 << n_bits):
                cand = thr | jnp.left_shift(jnp.int32(digit), low_bit)
                counts.append(jnp.sum(jnp.where(packed >= as_float(cand), 1.0, 0.0), axis=0, keepdims=True))
            cnt = jnp.concatenate(counts + [jnp.zeros((1, LANES), F32)], axis=0)
            shift = LANES // 2
            while shift >= N_EXPERTS:
                cnt = cnt + pltpu.roll(cnt, shift, 1)
                shift //= 2
            digit = jnp.sum(jnp.where(cnt >= cap, 1.0, 0.0), axis=0, keepdims=True).astype(jnp.int32)
            return thr | jnp.left_shift(digit, low_bit)

        thr = refine(jnp.zeros((1, LANES), jnp.int32), 28, 3)
        thr = lax.fori_loop(0, 7, lambda i, th: refine(th, 24 - 4 * i, 4), thr)
        beyond = as_float(thr + 1)
        above = jnp.where(aff >= beyond, 1.0, 0.0)
        tied = jnp.where((aff >= as_float(thr)) & (aff < beyond), 1.0, 0.0)
        need = cap - jnp.sum(above, axis=0, keepdims=True)
        tie_rank = _excl_cumsum_rows(tied, tri_strict)
        sel = above + tied * jnp.where(tie_rank < need, 1.0, 0.0)
        before = _excl_cumsum_rows(sel, tri_strict)
        pos = jnp.where(sel > 0.0, before, -1.0)
        pos_ref[r0:r0 + n, :] = pos
        pos_rows_ref[:, r0:r0 + n] = expert_major(pos)
        if r0 == 0:
            marks = [before[i:i + 1, :] for i in range(0, n, TOKEN_BLOCK)] + [jnp.sum(sel, axis=0, keepdims=True)]
            marks += [jnp.zeros((1, LANES), F32)] * (starts_ref.shape[0] - len(marks))
            starts_ref[...] = jnp.concatenate(marks, axis=0)


def _route(aff, segments):
    bsz, t, _ = aff.shape
    r = np.arange(TOKEN_BLOCK)
    tri_strict = jnp.asarray((r[None, :] < r[:, None]).astype(np.float32), BF16)
    per_row = LANES // N_EXPERTS
    packed = aff[:, :, :N_EXPERTS].reshape(bsz, t // per_row, LANES)
    return pl.pallas_call(
        functools.partial(_route_kernel, segments=segments),
        grid=(bsz,),
        in_specs=[pl.BlockSpec((None, t, LANES), lambda b: (b, 0, 0)),
                  pl.BlockSpec((None, t // per_row, LANES), lambda b: (b, 0, 0)),
                  pl.BlockSpec(tri_strict.shape, lambda b: (0, 0))],
        out_specs=[pl.BlockSpec((None, t, LANES), lambda b: (b, 0, 0)),
                   pl.BlockSpec((None, N_EXPERTS, t), lambda b: (b, 0, 0)),
                   pl.BlockSpec((None, N_EXPERTS, t), lambda b: (b, 0, 0)),
                   pl.BlockSpec((None, 16, LANES), lambda b: (b, 0, 0))],
        out_shape=[jax.ShapeDtypeStruct((bsz, t, LANES), F32),
                   jax.ShapeDtypeStruct((bsz, N_EXPERTS, t), F32),
                   jax.ShapeDtypeStruct((bsz, N_EXPERTS, t), F32),
                   jax.ShapeDtypeStruct((bsz, 16, LANES), F32)],
        compiler_params=_cparams(("parallel",)),
    )(aff, packed, tri_strict)


def _onehot(cond):
    return jnp.where(cond, 1.0, 0.0).astype(BF16)


SLOT_WINDOW = 128
GATHER_GROUP = 4


def _slot_window(starts_ref, cap, blk, e):
    first = jnp.minimum((starts_ref[blk, e] >> 4) << 4, cap - SLOT_WINDOW)
    return first, starts_ref[blk + 1, e] <= first + SLOT_WINDOW


def _row_sum(a):
    folded = a[:, 0:LANES]
    for c0 in range(LANES, a.shape[1], LANES):
        folded = folded + a[:, c0:c0 + LANES]
    return jnp.sum(folded, axis=1, keepdims=True)


def _gather_kernel(starts_ref, hb_ref, aff_ref, pos_ref, xe_ref, gate_ref, *, segments):
    (_, n_lat, cap_lat), (ctx_r0, n_ctx, cap_ctx) = segments
    n_blocks = n_lat // TOKEN_BLOCK

    def pick(e, r0, n, slot0, cap):
        hit = pos_ref[e:e + 1, r0:r0 + n] == lax.broadcasted_iota(jnp.int32, (cap, n), 0).astype(F32)
        xe_ref[e, slot0:slot0 + cap, :] = _dot(_onehot(hit), hb_ref[r0:r0 + n, :]).astype(BF16)
        gate = _row_sum(jnp.where(hit, aff_ref[e:e + 1, r0:r0 + n], 0.0))
        gate_ref[e, slot0:slot0 + cap, :] = jnp.broadcast_to(gate, (cap, LANES))

    fits = jnp.bool_(True)
    for blk in range(n_blocks):
        for e in range(N_EXPERTS):
            fits = jnp.logical_and(fits, _slot_window(starts_ref, cap_lat, blk, e)[1])

    @pl.when(fits)
    def _():
        xe_ref[:, 0:cap_lat, :] = jnp.zeros((N_EXPERTS, cap_lat, xe_ref.shape[2]), BF16)
        gate_ref[:, 0:cap_lat, :] = jnp.zeros((N_EXPERTS, cap_lat, LANES), F32)
        slot = lax.broadcasted_iota(jnp.int32, (SLOT_WINDOW, TOKEN_BLOCK), 0).astype(F32)
        for blk in range(n_blocks):
            toks = slice(blk * TOKEN_BLOCK, (blk + 1) * TOKEN_BLOCK)
            for e0 in range(0, N_EXPERTS, GATHER_GROUP):
                group = range(e0, e0 + GATHER_GROUP)
                firsts = [_slot_window(starts_ref, cap_lat, blk, e)[0] for e in group]
                hits = [pos_ref[e:e + 1, toks] - first.astype(F32) == slot for e, first in zip(group, firsts)]
                part = _dot(jnp.concatenate([_onehot(hit) for hit in hits], axis=0), hb_ref[toks, :])
                for i, (e, first, hit) in enumerate(zip(group, firsts, hits)):
                    rows = pl.ds(pl.multiple_of(first, 16), SLOT_WINDOW)
                    xe_ref[e, rows, :] += part[i * SLOT_WINDOW:(i + 1) * SLOT_WINDOW, :].astype(BF16)
                    gate = _row_sum(jnp.where(hit, aff_ref[e:e + 1, toks], 0.0))
                    gate_ref[e, rows, :] += jnp.broadcast_to(gate, (SLOT_WINDOW, LANES))

    @pl.when(jnp.logical_not(fits))
    def _():
        for e in range(N_EXPERTS):
            pick(e, 0, n_lat, 0, cap_lat)

    if cap_ctx > 0:
        for e in range(N_EXPERTS):
            pick(e, ctx_r0, n_ctx, cap_lat, cap_ctx)


def _gather(hb, aff_rows, pos_rows, starts, segments):
    bsz, t, d = hb.shape
    rows = sum(cap for _, _, cap in segments)
    assert segments[0][2] >= SLOT_WINDOW and starts.shape[1] == segments[0][1] // TOKEN_BLOCK + 1
    per_sample = lambda b: (b, 0, 0)
    slots = lambda b: (0, b, 0)
    return pl.pallas_call(
        functools.partial(_gather_kernel, segments=segments),
        grid=(bsz,),
        in_specs=[pl.BlockSpec((None,) + starts.shape[1:], per_sample, memory_space=pltpu.SMEM),
                  pl.BlockSpec((None, t, d), per_sample),
                  pl.BlockSpec((None, N_EXPERTS, t), per_sample),
                  pl.BlockSpec((None, N_EXPERTS, t), per_sample)],
        out_specs=[pl.BlockSpec((N_EXPERTS, rows, d), slots), pl.BlockSpec((N_EXPERTS, rows, LANES), slots)],
        out_shape=[jax.ShapeDtypeStruct((N_EXPERTS, bsz * rows, d), BF16),
                   jax.ShapeDtypeStruct((N_EXPERTS, bsz * rows, LANES), F32)],
        compiler_params=_cparams(("parallel",)),
    )(starts, hb, aff_rows, pos_rows)


def _ffn_kernel(xe_ref, gate_ref, wg_ref, wu_ref, wd_ref, y_ref, acc_ref, *, row_chunk):
    f = pl.program_id(1)
    last = pl.num_programs(1) - 1

    def step(mode):
        wg = wg_ref[...].astype(BF16)
        wu = wu_ref[...].astype(BF16)
        wd = wd_ref[...].astype(BF16)
        for r0 in range(0, xe_ref.shape[0], row_chunk):
            rows = slice(r0, r0 + row_chunk)
            xe = xe_ref[rows, :]
            hidden = (_silu(_dot(xe, wg)) * _dot(xe, wu)).astype(BF16)
            part = _dot(hidden, wd)
            if mode == "first":
                acc_ref[rows, :] = part
            elif mode == "middle":
                acc_ref[rows, :] += part
            else:
                y_ref[rows, :] = ((acc_ref[rows, :] + part) * gate_ref[rows, 0:1]).astype(y_ref.dtype)

    pl.when(f == 0)(lambda: step("first"))
    pl.when((f > 0) & (f < last))(lambda: step("middle"))
    pl.when(f == last)(lambda: step("last"))


def _ffn(xe, gate, w_gate, w_up, w_down, layer):
    n_rows, d = xe.shape[1], xe.shape[2]
    ff = w_gate.shape[-1]
    row_chunk = n_rows // 2 if n_rows % 2 == 0 and (n_rows // 2) % 16 == 0 else n_rows
    return pl.pallas_call(
        functools.partial(_ffn_kernel, row_chunk=row_chunk),
        grid=(N_EXPERTS, ff // FF_TILE),
        in_specs=[
            pl.BlockSpec((None, n_rows, d), lambda e, f: (e, 0, 0)),
            pl.BlockSpec((None, n_rows, LANES), lambda e, f: (e, 0, 0)),
            pl.BlockSpec((None, None, d, FF_TILE), lambda e, f: (layer, e, 0, f)),
            pl.BlockSpec((None, None, d, FF_TILE), lambda e, f: (layer, e, 0, f)),
            pl.BlockSpec((None, None, FF_TILE, d), lambda e, f: (layer, e, f, 0)),
        ],
        out_specs=pl.BlockSpec((None, n_rows, d), lambda e, f: (e, 0, 0)),
        out_shape=jax.ShapeDtypeStruct((N_EXPERTS, n_rows, d), BF16),
        scratch_shapes=[pltpu.VMEM((n_rows, d), F32)],
        compiler_params=_cparams(("parallel", "arbitrary")),
    )(xe, gate, w_gate, w_up, w_down)


def _scatter_kernel(starts_ref, x_ref, y_ref, pos_ref, mods_ref, nw_ref, *rest, segments, final, with_next):
    if with_next:
        next_mods_ref, next_nw_ref, out_ref, h_ref = rest
    else:
        out_ref, = rest
    d = y_ref.shape[-1]
    ctx_row = mods_ref.shape[0] - 1
    n_sub = x_ref.shape[0] // TOKEN_BLOCK
    on_last_block = pl.program_id(1) == pl.num_programs(1) - 1
    cap_lat = segments[0][2]

    def window(sub_block, e):
        return _slot_window(starts_ref, cap_lat, pl.program_id(1) * n_sub + sub_block, e)

    def block(seg_id, rows, windowed=None):
        cap = segments[seg_id][2]
        slot0 = sum(seg[2] for seg in segments[:seg_id])
        mod_row = ctx_row if seg_id == 1 else pl.program_id(0)
        gate = mods_ref[mod_row][5:6, :]
        pos = pos_ref[rows, :]
        if windowed is None:
            y_all = y_ref[:, slot0:slot0 + cap, :].reshape(N_EXPERTS * cap, d)
            slot = lax.broadcasted_iota(jnp.int32, (TOKEN_BLOCK, cap), 1).astype(F32)
            onehots = jnp.concatenate([_onehot(pos[:, e:e + 1] == slot) for e in range(N_EXPERTS)], axis=1)
        else:
            slot = lax.broadcasted_iota(jnp.int32, (TOKEN_BLOCK, SLOT_WINDOW), 1).astype(F32)
            firsts = [window(windowed, e)[0] for e in range(N_EXPERTS)]
            y_all = jnp.concatenate(
                [y_ref[e, pl.ds(pl.multiple_of(slot0 + firsts[e], 16), SLOT_WINDOW), :] for e in range(N_EXPERTS)], axis=0)
            onehots = jnp.concatenate(
                [_onehot(pos[:, e:e + 1] - firsts[e].astype(F32) == slot) for e in range(N_EXPERTS)], axis=1)
        x = x_ref[rows, :] + gate * _dot(onehots, y_all)
        if final:
            x = x * lax.rsqrt(jnp.mean(x * x, axis=-1, keepdims=True) + EPS) * nw_ref[...]
        out_ref[rows, :] = x
        if with_next:
            m = next_mods_ref[mod_row]
            h_ref[rows, :] = _rms_mod(x, next_nw_ref[...], m[0:1, :], m[1:2, :]).astype(BF16)

    has_ctx = segments[1][2] > 0
    n_lat_blocks = segments[0][1] // TOKEN_BLOCK

    def step(use_windows):
        for s in range(n_sub):
            rows = slice(s * TOKEN_BLOCK, (s + 1) * TOKEN_BLOCK)
            lat = functools.partial(block, 0, rows, s if use_windows else None)
            if has_ctx and s == n_sub - 1:
                pl.when(jnp.logical_not(on_last_block))(lat)
                pl.when(on_last_block)(functools.partial(block, 1, rows))
            else:
                lat()

    fits = jnp.bool_(True)
    for s in range(n_sub):
        is_lat = pl.program_id(1) * n_sub + s < n_lat_blocks
        for e in range(N_EXPERTS):
            safe = window(jnp.where(is_lat, s, 0) if has_ctx and s == n_sub - 1 else s, e)[1]
            fits = jnp.logical_and(fits, jnp.logical_or(safe, jnp.logical_not(is_lat)))
    pl.when(fits)(lambda: step(True))
    pl.when(jnp.logical_not(fits))(lambda: step(False))


def _scatter(x, y, pos, starts, mods, nw_final, segments, final, next_norm=None):
    bsz, t, d = x.shape
    assert segments[0][2] >= SLOT_WINDOW and starts.shape[1] == segments[0][1] // TOKEN_BLOCK + 1
    n_slots = y.shape[1] // bsz
    out_rows = sum(n for _, n, cap in segments if cap > 0)
    tm = (3 if segments[1][2] > 0 else 2) * TOKEN_BLOCK
    assert out_rows % tm == 0
    tok = lambda b, i: (b, i, 0)
    const2, const3 = (lambda b, i: (0, 0)), (lambda b, i: (0, 0, 0))
    in_specs = [
        pl.BlockSpec((None,) + starts.shape[1:], lambda b, i: (b, 0, 0), memory_space=pltpu.SMEM),
        pl.BlockSpec((None, tm, d), tok),
        pl.BlockSpec((N_EXPERTS, n_slots, d), lambda b, i: (0, b, 0)),
        pl.BlockSpec((None, tm, LANES), tok),
        pl.BlockSpec(mods.shape, const3),
        pl.BlockSpec((1, d), const2),
    ]
    args = [starts, x, y, pos, mods, nw_final]
    out_specs = [pl.BlockSpec((None, tm, d), tok)]
    out_shape = [jax.ShapeDtypeStruct((bsz, out_rows, d), F32)]
    if next_norm is not None:
        in_specs += [pl.BlockSpec(next_norm[0].shape, const3), pl.BlockSpec((1, d), const2)]
        args += list(next_norm)
        out_specs.append(pl.BlockSpec((None, tm, d), tok))
        out_shape.append(jax.ShapeDtypeStruct((bsz, out_rows, d), BF16))
    return pl.pallas_call(
        functools.partial(_scatter_kernel, segments=segments, final=final, with_next=next_norm is not None),
        grid=(bsz, out_rows // tm),
        in_specs=in_specs, out_specs=out_specs, out_shape=out_shape,
        compiler_params=_cparams(("parallel", "parallel")),
    )(*args)


def _moe(x, hb, aff, mods, w_gate, w_up, w_down, layer, nw_final, with_ctx, final, next_norm):
    bsz, t, d = x.shape
    seq = t - TOKEN_BLOCK
    cap_lat = EC_CAPACITY * seq // N_EXPERTS
    cap_ctx = EC_CAPACITY * TOKEN_BLOCK // N_EXPERTS if with_ctx else 0
    segments = ((0, seq, cap_lat), (seq, TOKEN_BLOCK, cap_ctx))
    pos, pos_rows, aff_rows, starts = _route(aff, segments)
    starts = starts[:, :seq // TOKEN_BLOCK + 1, :N_EXPERTS].astype(jnp.int32)
    xe, gate = _gather(hb, aff_rows, pos_rows, starts, segments)
    y = _ffn(xe, gate, w_gate, w_up, w_down, layer)
    return _scatter(x, y, pos, starts, mods, nw_final, segments, final, next_norm)


def _rope_tables(seq, n_ctx):
    half = RET_DK // 2
    t = jnp.arange(seq)
    inv = ROPE_THETA ** (-jnp.arange(0, half, 2, dtype=F32) / half)
    tabs = []
    for pos in ((t // GRID_W).astype(F32), (t % GRID_W).astype(F32)):
        ang = pos[:, None] * inv
        cos, sin = jnp.cos(ang), jnp.sin(ang)
        tabs.append(jnp.concatenate([cos, cos], axis=1))
        tabs.append(jnp.concatenate([-sin, sin], axis=1))
    ident = [jnp.ones, jnp.zeros, jnp.ones, jnp.zeros]
    return [jnp.concatenate([tab, fn((n_ctx, 2 * (half // 2)), F32)], axis=0) for tab, fn in zip(tabs, ident)]


def kernel(x, c, ctx, c_ctx, w_ada, b_ada, norm_mix, norm_ffn, ret_w_in, ret_w_out, hg_w_in, hg_g_norm,
           hg_lower_bounds, hg_w_out, moe_router, moe_w_gate, moe_w_up, moe_w_down, norm_final):
    bsz, seq, d = x.shape
    n_ctx = ctx.shape[1]
    depth = w_ada.shape[0]
    assert n_ctx == TOKEN_BLOCK and seq % TOKEN_BLOCK == 0 and d == RET_HEADS * RET_DK == HG_HEADS * HG_DK

    cc = jnp.concatenate([c, c_ctx[None, :], jnp.zeros((16 - bsz - 1, d), F32)], axis=0)
    mods_all = _ada(cc, w_ada, b_ada)[:, :bsz + 1].reshape(depth, bsz + 1, 6, d)
    t = seq + n_ctx
    rope = _rope_tables(seq, n_ctx)
    rope_specs = [pl.BlockSpec((t // 2, LANES), lambda j, b, i: (i, 0))] * 4
    w_router = jnp.pad(moe_router, ((0, 0), (0, 0), (0, LANES - N_EXPERTS)))

    xs, h = _prenorm(x, ctx, mods_all[0], norm_mix[0][None, :])
    for layer in range(depth):
        last = layer == depth - 1
        slot = layer // 2
        mods = mods_all[layer]
        if layer % 2 == 0:
            assert RET_HEADS * RET_DK == PROJ_COLS
            w = ret_w_in[slot]
            assert np.log2(RET_DK ** -0.5) % 1 == 0
            qk, = _proj(h, w, 0, 2, functools.partial(_ep_rope, key_scale=RET_DK ** -0.5), [BF16], rope, rope_specs)
            v, = _proj(h, w, 2, 2, _ep_plain, [BF16])
            zg, = _proj(h, w, 4, 2, _ep_silu, [BF16])
            o = _retention(qk, v)
            w_out, g_norm, n_heads, dv = ret_w_out[slot].astype(BF16), None, RET_HEADS, RET_DV
        else:
            assert HG_HEADS * HG_DK == PROJ_COLS
            w = hg_w_in[slot]
            lb_specs = [pl.BlockSpec((depth, PROJ_COLS), lambda j, b, i: (0, j))]
            ep_gate = functools.partial(_ep_gate, layer=layer)
            q, = _proj(h, w, 0, 1, _ep_silu, [BF16])
            k_fb, g_fb = _proj(h, w, 1, 2, ep_gate, [F32, F32], (hg_lower_bounds.reshape(depth, 2 * PROJ_COLS),),
                               lb_specs)
            v, = _proj(h, w, 3, 1, _ep_plain, [BF16])
            zg, = _proj(h, w, 4, 1, _ep_silu, [BF16])
            o = _hgrn2(q, k_fb, g_fb, v)
            w_out, g_norm, n_heads, dv = hg_w_out[slot].astype(BF16), hg_g_norm[slot][None, :], HG_HEADS, HG_DV
        xs, hb, aff = _readout(o, zg, xs, mods, w_out, g_norm, norm_ffn[layer][None, :], w_router[layer],
                                       n_heads, dv)
        next_norm = None if last else (mods_all[layer + 1], norm_mix[layer + 1][None, :])
        res = _moe(xs, hb, aff, mods, moe_w_gate, moe_w_up, moe_w_down, layer, norm_final[None, :],
                   with_ctx=not last, final=last, next_norm=next_norm)
        xs, h = res if next_norm is not None else (res[0], None)
    return xs
```

```python
import functools

import numpy as np
import jax
import jax.numpy as jnp
from jax import lax
from jax.experimental import pallas as pl
from jax.experimental.pallas import tpu as pltpu

F32 = jnp.float32
BF16 = jnp.bfloat16
EPS = 1e-6
ROPE_THETA = 10000.0
GRID_W = 64

LANES = 128
TOKEN_BLOCK = 256
RET_HEADS, RET_DK, RET_DV, RET_CHUNK = 4, 256, 512, 256
HG_HEADS, HG_DK, HG_DV, HG_CHUNK, HG_SUB = 8, 128, 128, 128, 8
HG_WIDE = HG_CHUNK
HG_HEADS_PER_STEP = 2
HG_FAST_SPAN = 96.0
READOUT_SUBBLOCKS = 3
LOG2_E = 1.4426950408889634
N_EXPERTS = 16
EC_CAPACITY = 2
FF_TILE = 256
PROJ_COLS = 1024
VMEM_LIMIT = 56 * 1024 * 1024

HIGHEST = lax.Precision.HIGHEST


def _cparams(sem):
    return pltpu.CompilerParams(dimension_semantics=sem, vmem_limit_bytes=VMEM_LIMIT)


def _dot(a, b, **kw):
    return jnp.dot(a, b, preferred_element_type=F32, **kw)


def _dot_nt(a, b):
    return lax.dot_general(a, b, (((1,), (1,)), ((), ())), preferred_element_type=F32)


def _silu(x):
    return x * jax.nn.sigmoid(x)


def _rms_mod(x, nw, shift, scale):
    ms = jnp.mean(x * x, axis=-1, keepdims=True)
    h = x * lax.rsqrt(ms + EPS) * nw
    return h * (1.0 + scale) + shift


def _ada_kernel(c_ref, w_ref, b_ref, o_ref):
    o_ref[...] = _dot(_silu(c_ref[...]), w_ref[...], precision=HIGHEST) + b_ref[...]


def _ada(cc, w_ada, b_ada):
    depth, d, d6 = w_ada.shape
    rows = cc.shape[0]
    return pl.pallas_call(
        _ada_kernel,
        grid=(depth, d6 // d),
        in_specs=[
            pl.BlockSpec((rows, d), lambda l, j: (0, 0)),
            pl.BlockSpec((None, d, d), lambda l, j: (l, 0, j)),
            pl.BlockSpec((None, 1, d), lambda l, j: (l, 0, j)),
        ],
        out_specs=pl.BlockSpec((None, rows, d), lambda l, j: (l, 0, j)),
        out_shape=jax.ShapeDtypeStruct((depth, rows, d6), F32),
        compiler_params=_cparams(("parallel", "parallel")),
    )(cc, w_ada, b_ada.reshape(depth, 1, d6))


def _prenorm_kernel(x_ref, ctx_ref, mods_ref, nw_ref, xs_ref, h_ref):
    ctx_row = mods_ref.shape[0] - 1
    nw = nw_ref[...]
    seq = x_ref.shape[0]
    for src_ref, r0, row in ((x_ref, 0, pl.program_id(0)), (ctx_ref, seq, ctx_row)):
        m = mods_ref[row]
        shift, scale = m[0:1, :], m[1:2, :]

        def body(i, carry):
            src = pl.ds(pl.multiple_of(i * TOKEN_BLOCK, TOKEN_BLOCK), TOKEN_BLOCK)
            dst = pl.ds(pl.multiple_of(r0 + i * TOKEN_BLOCK, TOKEN_BLOCK), TOKEN_BLOCK)
            x = src_ref[src, :]
            xs_ref[dst, :] = x
            h_ref[dst, :] = _rms_mod(x, nw, shift, scale).astype(BF16)
            return carry

        lax.fori_loop(0, src_ref.shape[0] // TOKEN_BLOCK, body, 0)


def _prenorm(x, ctx, mods, nw):
    bsz, seq, d = x.shape
    n_ctx = ctx.shape[1]
    t = seq + n_ctx
    return pl.pallas_call(
        _prenorm_kernel,
        grid=(bsz,),
        in_specs=[pl.BlockSpec((None, seq, d), lambda b: (b, 0, 0)),
                  pl.BlockSpec((None, n_ctx, d), lambda b: (b, 0, 0)),
                  pl.BlockSpec(mods.shape, lambda b: (0, 0, 0)),
                  pl.BlockSpec((1, d), lambda b: (0, 0))],
        out_specs=[pl.BlockSpec((None, t, d), lambda b: (b, 0, 0)), pl.BlockSpec((None, t, d), lambda b: (b, 0, 0))],
        out_shape=[jax.ShapeDtypeStruct((bsz, t, d), F32), jax.ShapeDtypeStruct((bsz, t, d), BF16)],
        compiler_params=_cparams(("parallel",)),
    )(x, ctx, mods, nw)


def _proj_kernel(h_ref, w_ref, *rest, epilogue, n_extra):
    extras, outs = rest[:n_extra], rest[n_extra:]
    epilogue(_dot(h_ref[...], w_ref[...].astype(BF16)), extras, outs)


def _ep_plain(acc, extras, outs):
    outs[0][...] = acc.astype(outs[0].dtype)


def _ep_silu(acc, extras, outs):
    outs[0][...] = _silu(acc).astype(outs[0].dtype)


def _ep_value_gate(acc, extras, outs, *, gate_from):
    outs[0][...] = jnp.where(pl.program_id(0) >= gate_from, _silu(acc), acc).astype(outs[0].dtype)


def _ep_rope(acc, extras, outs, *, key_scale):
    c_row, s_row, c_col, s_col = (r[...] for r in extras)
    scale = jnp.where(pl.program_id(0) == 1, key_scale, 1.0).astype(F32)
    for s in range(acc.shape[1] // LANES):
        slab = acc[:, s * LANES:(s + 1) * LANES] * scale
        cos, sin = (c_row, s_row) if s % 2 == 0 else (c_col, s_col)
        rot = slab * cos + pltpu.roll(slab, LANES // 2, 1) * sin
        outs[0][:, s * LANES:(s + 1) * LANES] = rot.astype(outs[0].dtype)


def _ep_gate(acc, extras, outs, *, layer):
    raw = extras[0][...]
    e = jnp.exp(raw - jnp.max(raw, axis=0, keepdims=True))
    sm = e / jnp.sum(e, axis=0, keepdims=True)
    lb = -sm[0:1, :]
    for l in range(layer + 1):
        lb = lb + sm[l:l + 1, :]
    f = lb + (1.0 - lb) * jax.nn.sigmoid(acc)
    outs[0][...] = 1.0 - f
    outs[1][...] = jnp.log(f)


def _proj(h, w, col_block0, ncol, epilogue, out_dtypes, extras=(), extra_specs=()):
    bsz, t, d = h.shape
    tm, tn = t // 2, PROJ_COLS
    in_specs = [
        pl.BlockSpec((None, tm, d), lambda j, b, i: (b, i, 0)),
        pl.BlockSpec((d, tn), lambda j, b, i: (0, col_block0 + j)),
    ] + list(extra_specs)
    out_specs = [pl.BlockSpec((None, tm, tn), lambda j, b, i: (b, i, j)) for _ in out_dtypes]
    out_shape = [jax.ShapeDtypeStruct((bsz, t, tn * ncol), dt) for dt in out_dtypes]
    return pl.pallas_call(
        functools.partial(_proj_kernel, epilogue=epilogue, n_extra=len(extras)),
        grid=(ncol, bsz, 2), in_specs=in_specs, out_specs=out_specs, out_shape=out_shape,
        compiler_params=_cparams(("parallel", "parallel", "parallel")),
    )(h, w, *extras)


def _ret_tables():
    c = RET_CHUNK
    j = np.arange(2 * RET_HEADS, dtype=np.float64)
    lg = np.log1p(-np.exp2(-5.0 - j / 2))
    lg_f, lg_b = lg[0::2], lg[1::2]
    i = np.arange(c, dtype=np.float64)
    diff = i[:, None] - i[None, :]
    mats = np.zeros((RET_HEADS, c, c), np.float64)
    tabs = np.zeros((RET_HEADS, c, LANES), np.float64)
    for h in range(RET_HEADS):
        mats[h] = np.where(diff >= 0, np.exp(diff * lg_f[h]), 0.0) + np.where(diff <= 0, np.exp(-diff * lg_b[h]), 0.0)
        tabs[h, :, 0] = np.exp((i + 1) * lg_f[h])
        tabs[h, :, 1] = np.exp((c - 1 - i) * lg_f[h])
        tabs[h, :, 2] = np.exp((c - i) * lg_b[h])
        tabs[h, :, 3] = np.exp(i * lg_b[h])
        tabs[h, :, 4] = np.exp(c * lg_f[h])
        tabs[h, :, 5] = np.exp(c * lg_b[h])
    return jnp.asarray(mats, F32), jnp.asarray(tabs, F32)


def _ret_kernel(q_ref, k_ref, v_ref, m_ref, tab_ref, out_ref, o_ref, vt_ref, sf_ref, sb_ref, *, n_chunks):
    c = RET_CHUNK
    decay = m_ref[...]
    tab = tab_ref[...]
    qf, kfs, qb, kbs = tab[:, 0:1], tab[:, 1:2], tab[:, 2:3], tab[:, 3:4]
    end_f, end_b = tab[0:1, 4:5], tab[0:1, 5:6]

    def rows(ci):
        return slice(ci * c, (ci + 1) * c)

    for ci in range(n_chunks):
        q, k, v = q_ref[rows(ci), :], k_ref[rows(ci), :], v_ref[rows(ci), :]
        scores = (_dot_nt(q, k) * decay).astype(BF16)
        o_ref[rows(ci), :] = _dot(scores, v)
        vt_ref[ci] = v.astype(F32).T.astype(BF16)

    ctx = n_chunks - 1
    order_f = [ctx] + list(range(ctx))
    order_b = [ctx] + list(range(ctx - 1, -1, -1))
    sf_ref[...] = jnp.zeros_like(sf_ref)
    sb_ref[...] = jnp.zeros_like(sb_ref)
    for step in range(n_chunks):
        for order, s_ref, qd, kd, end in ((order_f, sf_ref, qf, kfs, end_f), (order_b, sb_ref, qb, kbs, end_b)):
            ci = order[step]
            if step > 0:
                o_ref[rows(ci), :] += _dot_nt(q_ref[rows(ci), :], s_ref[...].astype(BF16)) * qd
            if step < n_chunks - 1:
                ks = (k_ref[rows(ci), :].astype(F32) * kd).astype(BF16)
                upd = _dot(vt_ref[ci], ks)
                s_ref[...] = upd if step == 0 else s_ref[...] * end + upd
    for ci in range(n_chunks):
        out_ref[rows(ci), :] = o_ref[rows(ci), :].astype(out_ref.dtype)


def _retention(qk, v):
    bsz, t, _ = qk.shape
    n_chunks = t // RET_CHUNK
    mats, tabs = _ret_tables()
    return pl.pallas_call(
        functools.partial(_ret_kernel, n_chunks=n_chunks),
        grid=(bsz, RET_HEADS),
        in_specs=[
            pl.BlockSpec((None, t, RET_DK), lambda b, h: (b, 0, h)),
            pl.BlockSpec((None, t, RET_DK), lambda b, h: (b, 0, RET_HEADS + h)),
            pl.BlockSpec((None, t, RET_DV), lambda b, h: (b, 0, h)),
            pl.BlockSpec((None, RET_CHUNK, RET_CHUNK), lambda b, h: (h, 0, 0)),
            pl.BlockSpec((None, RET_CHUNK, LANES), lambda b, h: (h, 0, 0)),
        ],
        out_specs=pl.BlockSpec((None, t, RET_DV), lambda b, h: (b, 0, h)),
        out_shape=jax.ShapeDtypeStruct((bsz, t, RET_HEADS * RET_DV), BF16),
        scratch_shapes=[
            pltpu.VMEM((t, RET_DV), F32),
            pltpu.VMEM((n_chunks, RET_DV, RET_CHUNK), BF16),
            pltpu.VMEM((RET_DV, RET_DK), F32),
            pltpu.VMEM((RET_DV, RET_DK), F32),
        ],
        compiler_params=_cparams(("parallel", "parallel")),
    )(qk, qk, v, mats, tabs)


def _hg_levels(block):
    return [HG_CHUNK >> (i + 1) for i in range(int(np.log2(HG_CHUNK // block)))]


def _hg_masks():
    c = HG_CHUNK
    r = np.arange(c)[:, None]
    col = np.arange(c)[None, :]
    tables = {}
    for lower in (True, False):
        for s in _hg_levels(HG_SUB):
            same = (r // (2 * s)) == (col // (2 * s))
            rh, ch = (r // s) % 2, (col // s) % 2
            tables["level", lower, s] = same & ((rh == 1) & (ch == 0) if lower else (rh == 0) & (ch == 1))
            far = (rh == 1) if lower else (rh == 0)
            tables["sign", lower, s] = np.broadcast_to(np.where(far, 1.0, -1.0), (c, c))
        for block in (HG_SUB, HG_WIDE):
            tables["diag", lower, block] = ((r // block) == (col // block)) & ((col <= r) if lower else (col >= r))
    index = {name: i for i, name in enumerate(tables)}
    return index, np.stack([np.asarray(t, np.float32) for t in tables.values()])


def _hg_tables():
    c, sub = HG_CHUNK, HG_SUB
    r = np.arange(c)[:, None]
    col = np.arange(c)[None, :]
    tri = np.stack([col <= r, col >= r]).astype(np.float32)
    place = np.stack([np.broadcast_to((np.arange(c) % sub) == j, (c, c)) for j in range(sub)]).astype(np.float32)
    return jnp.asarray(_hg_masks()[1]), jnp.asarray(tri, BF16), jnp.asarray(place, BF16)


def _split3(g):
    hi = g.astype(BF16)
    r1 = g - hi.astype(F32)
    mid = r1.astype(BF16)
    lo = (r1 - mid.astype(F32)).astype(BF16)
    return hi, mid, lo


def _hg_intra(q, k, b, masks, place, lower, block, pairwise=False):
    c, sub = HG_CHUNK, HG_SUB
    index = _hg_masks()[0]
    total = jnp.zeros((c, c), F32)
    for s in _hg_levels(block):
        pieces = []
        for m in range(0, c, 2 * s):
            r = m + s - 1 if lower else m + s
            pieces.append(jnp.broadcast_to(b[r:r + 1, :], (2 * s, b.shape[1])))
        ref = pieces[0] if len(pieces) == 1 else jnp.concatenate(pieces, axis=0)
        w = jnp.exp2((b - ref) * masks[index["sign", lower, s]])
        total = total + _dot_nt((q * w).astype(BF16), (k * w).astype(BF16)) * masks[index["level", lower, s]]
    diag_mask = masks[index["diag", lower, block]]
    if not pairwise:
        firsts = [u * block if lower else u * block + block - 1 for u in range(c // block)]
        ref = jnp.concatenate([jnp.broadcast_to(b[r:r + 1, :], (block, b.shape[1])) for r in firsts], axis=0)
        d = b - ref
        p = _dot_nt((q * jnp.exp2(d)).astype(BF16), (k * jnp.exp2(-d)).astype(BF16))
        return total + p * diag_mask, jnp.max(-d)
    assert block == sub
    shape3 = (c // sub, sub, q.shape[1])
    q3, b3 = q.reshape(shape3), b.reshape(shape3)
    a3 = (jnp.log2(jnp.maximum(k, 0.0)) - b).reshape(shape3)
    acc = jnp.zeros((c, c), F32)
    for jj in range(sub):
        aj = jnp.broadcast_to(a3[:, jj:jj + 1, :], shape3)
        p = q3 * jnp.exp2(jnp.minimum(b3 + aj, 0.0))
        acc = acc + _dot(p.reshape(c, q.shape[1]).astype(BF16), place[jj])
    return total + acc * diag_mask, None


def _hg_kernel(q_ref, kf_ref, kb_ref, gf_ref, gb_ref, v_ref, mask_ref, tri_ref, place_ref, out_ref,
               o_ref, bf_ref, bb_ref, vt_ref, s_ref, *, n_chunks):
    c = HG_CHUNK
    masks = [mask_ref[i] for i in range(mask_ref.shape[0])]
    place = [place_ref[i] for i in range(place_ref.shape[0])]
    tri_lo, tri_up = tri_ref[0], tri_ref[1]

    def cumsum(tri, g):
        hi, mid, lo = _split3(g)
        return (_dot(tri, hi) + _dot(tri, mid) + _dot(tri, lo)) * LOG2_E

    def chunk_rows(ci):
        return pl.ds(ci * c if isinstance(ci, int) else pl.multiple_of(ci * c, c), c)

    slabs = [slice(hh * HG_DK, (hh + 1) * HG_DK) for hh in range(HG_HEADS_PER_STEP)]

    def finish(ci, hh, scores, v):
        o_ref[chunk_rows(ci), slabs[hh]] = _dot(scores, v)
        vt_ref[hh, ci] = v.astype(F32).T.astype(BF16)

    first = chunk_rows(0)
    bf_ref[first, :] = jnp.concatenate([cumsum(tri_lo, gf_ref[first, sl]) for sl in slabs], axis=1)
    bb_ref[first, :] = jnp.concatenate([cumsum(tri_up, gb_ref[first, sl]) for sl in slabs], axis=1)
    s_ref[first, :] = jnp.zeros((c, s_ref.shape[1]), BF16)

    def intra(ci, carry):
        prev, nxt = jnp.maximum(ci - 1, 0), jnp.minimum(ci + 1, n_chunks - 1)
        rows, rows_p, rows_n = chunk_rows(ci), chunk_rows(prev), chunk_rows(nxt)
        loaded = []
        for sl in slabs:
            loaded.append((bf_ref[rows, sl], bb_ref[rows, sl], q_ref[rows, sl].astype(F32), kf_ref[rows, sl],
                           kb_ref[rows, sl], s_ref[rows_p, sl], v_ref[rows_p, sl], gf_ref[rows_n, sl],
                           gb_ref[rows_n, sl]))
        span = 0.0
        for hh, (bf, bb, q, kf, kb, s_prev, v_prev, gf_n, gb_n) in enumerate(loaded):
            finish(prev, hh, s_prev, v_prev)
            bf_ref[rows_n, slabs[hh]] = cumsum(tri_lo, gf_n)
            bb_ref[rows_n, slabs[hh]] = cumsum(tri_up, gb_n)
            lo, span_lo = _hg_intra(q, kf, bf, masks, place, True, HG_WIDE)
            up, span_up = _hg_intra(q, kb, bb, masks, place, False, HG_WIDE)
            s_ref[rows, slabs[hh]] = (lo + up).astype(BF16)
            span = jnp.maximum(span, jnp.maximum(span_lo, span_up))

        def redo(block, pairwise):
            worst = 0.0
            for sl in slabs:
                q = q_ref[rows, sl].astype(F32)
                lo, span_lo = _hg_intra(q, kf_ref[rows, sl], bf_ref[rows, sl], masks, place, True, block, pairwise)
                up, span_up = _hg_intra(q, kb_ref[rows, sl], bb_ref[rows, sl], masks, place, False, block, pairwise)
                s_ref[rows, sl] = (lo + up).astype(BF16)
                if not pairwise:
                    worst = jnp.maximum(worst, jnp.maximum(span_lo, span_up))
            return worst

        @pl.when(span > HG_FAST_SPAN)
        def _():
            @pl.when(redo(HG_SUB, False) > HG_FAST_SPAN)
            def _():
                redo(HG_SUB, True)

        return carry

    lax.fori_loop(0, n_chunks, intra, 0)
    last = chunk_rows(n_chunks - 1)
    for hh, sl in enumerate(slabs):
        finish(n_chunks - 1, hh, s_ref[last, sl], v_ref[last, sl])

    n_ctx = TOKEN_BLOCK // c
    n_lat = n_chunks - n_ctx

    assert all((n_lat + s if s < n_ctx else s - n_ctx) != n_chunks - 1 - s for s in range(n_chunks))

    def scan(step, carry):
        cf = jnp.where(step < n_ctx, n_lat + step, step - n_ctx)
        cb = n_chunks - 1 - step
        jobs = []
        for hh, sl in enumerate(slabs):
            for ci, k_ref, b_ref, end_row, st in ((cf, kf_ref, bf_ref, c - 1, carry[2 * hh]),
                                                  (cb, kb_ref, bb_ref, 0, carry[2 * hh + 1])):
                rows = chunk_rows(ci)
                jobs.append((rows, sl, q_ref[rows, sl].astype(F32), b_ref[rows, sl], k_ref[rows, sl], vt_ref[hh, ci],
                             o_ref[rows, sl], st, end_row))
        states, updates = [], []
        for rows, sl, q, b, k, vt, o_old, st, end_row in jobs:
            qs = (q * jnp.exp2(b)).astype(BF16)
            updates.append((rows, sl, o_old + _dot_nt(qs, st.astype(BF16))))
            b_end = b[end_row:end_row + 1, :]
            ks = (k * jnp.exp2(b_end - b)).astype(BF16)
            states.append(st * jnp.exp2(b_end) + _dot(vt, ks))
        for rows, sl, o_new in updates:
            o_ref[rows, sl] = o_new
        return tuple(states)

    zero = jnp.zeros((HG_DV, HG_DK), F32)
    lax.fori_loop(0, n_chunks, scan, (zero,) * (2 * HG_HEADS_PER_STEP))

    def emit(ci, carry):
        out_ref[chunk_rows(ci), :] = o_ref[chunk_rows(ci), :].astype(out_ref.dtype)
        return carry

    lax.fori_loop(0, n_chunks, emit, 0)


def _hgrn2(q, k_fb, g_fb, v):
    bsz, t, _ = q.shape
    n_chunks = t // HG_CHUNK
    hps = HG_HEADS_PER_STEP
    masks, tri, place = _hg_tables()
    head = lambda b, h: (b, 0, h)
    spec = pl.BlockSpec((None, t, hps * HG_DK), head)
    backward = pl.BlockSpec((None, t, hps * HG_DK), lambda b, h: (b, 0, HG_HEADS // hps + h))
    return pl.pallas_call(
        functools.partial(_hg_kernel, n_chunks=n_chunks),
        grid=(bsz, HG_HEADS // hps),
        in_specs=[spec, spec, backward, spec, backward, spec,
                  pl.BlockSpec(masks.shape, lambda b, h: (0, 0, 0)),
                  pl.BlockSpec(tri.shape, lambda b, h: (0, 0, 0)),
                  pl.BlockSpec(place.shape, lambda b, h: (0, 0, 0))],
        out_specs=pl.BlockSpec((None, t, hps * HG_DV), head),
        out_shape=jax.ShapeDtypeStruct((bsz, t, HG_HEADS * HG_DV), BF16),
        scratch_shapes=[
            pltpu.VMEM((t, hps * HG_DV), F32),
            pltpu.VMEM((t, hps * HG_DK), F32),
            pltpu.VMEM((t, hps * HG_DK), F32),
            pltpu.VMEM((hps, n_chunks, HG_DV, HG_CHUNK), BF16),
            pltpu.VMEM((t, hps * HG_CHUNK), BF16),
        ],
        compiler_params=_cparams(("parallel", "parallel")),
    )(q, k_fb, k_fb, g_fb, g_fb, v, masks, tri, place)


def _readout_kernel(o_ref, zg_ref, x_ref, mods_ref, wout_ref, gn_ref, nw_ref, wra_ref, wrb_ref,
                    xo_ref, hb_ref, aff_ref, *, n_heads, dv, affine):
    gn = gn_ref[...] if affine else None
    n_sub = x_ref.shape[0] // TOKEN_BLOCK
    ctx_row = mods_ref.shape[0] - 1
    on_last_block = pl.program_id(1) == pl.num_programs(1) - 1
    for s in range(n_sub):
        rows = slice(s * TOKEN_BLOCK, (s + 1) * TOKEN_BLOCK)
        mod_row = jnp.where(on_last_block, ctx_row, pl.program_id(0)) if s == n_sub - 1 else pl.program_id(0)
        mod = mods_ref[mod_row]
        o = o_ref[rows, :].astype(F32)
        parts = []
        for h in range(n_heads):
            oh = o[:, h * dv:(h + 1) * dv]
            ms = jnp.mean(oh * oh, axis=-1, keepdims=True)
            on = oh * lax.rsqrt(ms + EPS)
            parts.append(on if gn is None else on * gn)
        z = zg_ref[rows, :].astype(F32) * jnp.concatenate(parts, axis=1)
        x = x_ref[rows, :] + mod[2:3, :] * _dot(z.astype(BF16), wout_ref[...])
        xo_ref[rows, :] = x
        h2 = _rms_mod(x, nw_ref[...], mod[3:4, :], mod[4:5, :])
        h_hi = h2.astype(BF16)
        hb_ref[rows, :] = h_hi
        h_lo = (h2 - h_hi.astype(F32)).astype(BF16)
        both = _dot(h_hi, wra_ref[...])
        logits = both[:, :LANES] + both[:, LANES:] + _dot(h_lo, wrb_ref[...])
        lane = lax.broadcasted_iota(jnp.int32, logits.shape, 1)
        logits = jnp.where(lane < N_EXPERTS, logits, -1e30)
        e = jnp.exp(logits - jnp.max(logits, axis=-1, keepdims=True))
        aff_ref[rows, :] = e / jnp.sum(e, axis=-1, keepdims=True)


def _readout(o, zg, x, mods, w_out, g_norm, nw_ffn, w_router, n_heads, dv):
    affine = g_norm is not None
    if not affine:
        g_norm = jnp.ones((1, dv), F32)
    w_router_hi = w_router.astype(BF16)
    w_router_lo = (w_router - w_router_hi.astype(F32)).astype(BF16)
    w_router_hi_lo = jnp.concatenate([w_router_hi, w_router_lo], axis=1)
    bsz, t, d = x.shape
    hv = n_heads * dv
    tm = READOUT_SUBBLOCKS * TOKEN_BLOCK
    assert t % tm == 0
    tok = lambda b, i: (b, i, 0)
    full2 = lambda b, i: (0, 0)
    return pl.pallas_call(
        functools.partial(_readout_kernel, n_heads=n_heads, dv=dv, affine=affine),
        grid=(bsz, t // tm),
        in_specs=[
            pl.BlockSpec((None, tm, hv), tok),
            pl.BlockSpec((None, tm, hv), lambda b, i: (b, i, zg.shape[2] // hv - 1)),
            pl.BlockSpec((None, tm, d), tok),
            pl.BlockSpec(mods.shape, lambda b, i: (0, 0, 0)),
            pl.BlockSpec((hv, d), full2),
            pl.BlockSpec((1, dv), full2),
            pl.BlockSpec((1, d), full2),
            pl.BlockSpec((d, 2 * LANES), full2),
            pl.BlockSpec((d, LANES), full2),
        ],
        out_specs=[
            pl.BlockSpec((None, tm, d), tok),
            pl.BlockSpec((None, tm, d), tok),
            pl.BlockSpec((None, tm, LANES), tok),
        ],
        out_shape=[
            jax.ShapeDtypeStruct((bsz, t, d), F32),
            jax.ShapeDtypeStruct((bsz, t, d), BF16),
            jax.ShapeDtypeStruct((bsz, t, LANES), F32),
        ],
        compiler_params=_cparams(("parallel", "parallel")),
    )(o, zg, x, mods, w_out, g_norm, nw_ffn, w_router_hi_lo, w_router_hi)


def _excl_cumsum_rows(x01, tri_strict):
    blk = tri_strict.shape[0]
    carry = jnp.zeros((1, x01.shape[1]), F32)
    out = []
    for r0 in range(0, x01.shape[0], blk):
        xb = x01[r0:r0 + blk, :]
        out.append(_dot(tri_strict, xb.astype(BF16)) + carry)
        carry = carry + jnp.sum(xb, axis=0, keepdims=True)
    return out[0] if len(out) == 1 else jnp.concatenate(out, axis=0)


def _route_kernel(aff_ref, packed_ref, tri_ref, pos_ref, pos_rows_ref, aff_rows_ref, starts_ref, *, segments):
    tri_strict = tri_ref[...]
    per_row = LANES // N_EXPERTS

    def expert_major(a):
        return a.T[0:N_EXPERTS, :]

    for r0, n, cap in segments:
        aff = aff_ref[r0:r0 + n, :]
        aff_rows_ref[:, r0:r0 + n] = expert_major(aff)
        if cap == 0:
            pos_ref[r0:r0 + n, :] = jnp.full((n, LANES), -1.0, F32)
            pos_rows_ref[:, r0:r0 + n] = jnp.full((N_EXPERTS, n), -1.0, F32)
            continue
        packed = packed_ref[r0 // per_row:(r0 + n) // per_row, :]

        def as_float(pattern):
            return lax.bitcast_convert_type(pattern, F32)

        def refine(thr, low_bit, n_bits):
            counts = []
            for digit in range(1, 1 << n_bits):
                cand = thr | jnp.left_shift(jnp.int32(digit), low_bit)
                counts.append(jnp.sum(jnp.where(packed >= as_float(cand), 1.0, 0.0), axis=0, keepdims=True))
            cnt = jnp.concatenate(counts + [jnp.zeros((1, LANES), F32)], axis=0)
            shift = LANES // 2
            while shift >= N_EXPERTS:
                cnt = cnt + pltpu.roll(cnt, shift, 1)
                shift //= 2
            digit = jnp.sum(jnp.where(cnt >= cap, 1.0, 0.0), axis=0, keepdims=True).astype(jnp.int32)
            return thr | jnp.left_shift(digit, low_bit)

        thr = refine(jnp.zeros((1, LANES), jnp.int32), 28, 3)
        thr = lax.fori_loop(0, 7, lambda i, th: refine(th, 24 - 4 * i, 4), thr)
        beyond = as_float(thr + 1)
        above = jnp.where(aff >= beyond, 1.0, 0.0)
        tied = jnp.where((aff >= as_float(thr)) & (aff < beyond), 1.0, 0.0)
        need = cap - jnp.sum(above, axis=0, keepdims=True)
        tie_rank = _excl_cumsum_rows(tied, tri_strict)
        sel = above + tied * jnp.where(tie_rank < need, 1.0, 0.0)
        before = _excl_cumsum_rows(sel, tri_strict)
        pos = jnp.where(sel > 0.0, before, -1.0)
        pos_ref[r0:r0 + n, :] = pos
        pos_rows_ref[:, r0:r0 + n] = expert_major(pos)
        if r0 == 0:
            marks = [before[i:i + 1, :] for i in range(0, n, TOKEN_BLOCK)] + [jnp.sum(sel, axis=0, keepdims=True)]
            marks += [jnp.zeros((1, LANES), F32)] * (starts_ref.shape[0] - len(marks))
            starts_ref[...] = jnp.concatenate(marks, axis=0)


def _route(aff, segments):
    bsz, t, _ = aff.shape
    r = np.arange(TOKEN_BLOCK)
    tri_strict = jnp.asarray((r[None, :] < r[:, None]).astype(np.float32), BF16)
    per_row = LANES // N_EXPERTS
    packed = aff[:, :, :N_EXPERTS].reshape(bsz, t // per_row, LANES)
    return pl.pallas_call(
        functools.partial(_route_kernel, segments=segments),
        grid=(bsz,),
        in_specs=[pl.BlockSpec((None, t, LANES), lambda b: (b, 0, 0)),
                  pl.BlockSpec((None, t // per_row, LANES), lambda b: (b, 0, 0)),
                  pl.BlockSpec(tri_strict.shape, lambda b: (0, 0))],
        out_specs=[pl.BlockSpec((None, t, LANES), lambda b: (b, 0, 0)),
                   pl.BlockSpec((None, N_EXPERTS, t), lambda b: (b, 0, 0)),
                   pl.BlockSpec((None, N_EXPERTS, t), lambda b: (b, 0, 0)),
                   pl.BlockSpec((None, 16, LANES), lambda b: (b, 0, 0))],
        out_shape=[jax.ShapeDtypeStruct((bsz, t, LANES), F32),
                   jax.ShapeDtypeStruct((bsz, N_EXPERTS, t), F32),
                   jax.ShapeDtypeStruct((bsz, N_EXPERTS, t), F32),
                   jax.ShapeDtypeStruct((bsz, 16, LANES), F32)],
        compiler_params=_cparams(("parallel",)),
    )(aff, packed, tri_strict)


def _onehot(cond):
    return jnp.where(cond, 1.0, 0.0).astype(BF16)


SLOT_WINDOW = 128
GATHER_GROUP = 4


def _slot_window(starts_ref, cap, blk, e):
    first = jnp.minimum((starts_ref[blk, e] >> 4) << 4, cap - SLOT_WINDOW)
    return first, starts_ref[blk + 1, e] <= first + SLOT_WINDOW


def _row_sum(a):
    folded = a[:, 0:LANES]
    for c0 in range(LANES, a.shape[1], LANES):
        folded = folded + a[:, c0:c0 + LANES]
    return jnp.sum(folded, axis=1, keepdims=True)


def _gather_kernel(starts_ref, hb_ref, aff_ref, pos_ref, xe_ref, gate_ref, *, segments):
    (_, n_lat, cap_lat), (ctx_r0, n_ctx, cap_ctx) = segments
    n_blocks = n_lat // TOKEN_BLOCK

    def pick(e, r0, n, slot0, cap):
        hit = pos_ref[e:e + 1, r0:r0 + n] == lax.broadcasted_iota(jnp.int32, (cap, n), 0).astype(F32)
        xe_ref[e, slot0:slot0 + cap, :] = _dot(_onehot(hit), hb_ref[r0:r0 + n, :]).astype(BF16)
        gate = _row_sum(jnp.where(hit, aff_ref[e:e + 1, r0:r0 + n], 0.0))
        gate_ref[e, slot0:slot0 + cap, :] = jnp.broadcast_to(gate, (cap, LANES))

    fits = jnp.bool_(True)
    for blk in range(n_blocks):
        for e in range(N_EXPERTS):
            fits = jnp.logical_and(fits, _slot_window(starts_ref, cap_lat, blk, e)[1])

    @pl.when(fits)
    def _():
        xe_ref[:, 0:cap_lat, :] = jnp.zeros((N_EXPERTS, cap_lat, xe_ref.shape[2]), BF16)
        gate_ref[:, 0:cap_lat, :] = jnp.zeros((N_EXPERTS, cap_lat, LANES), F32)
        slot = lax.broadcasted_iota(jnp.int32, (SLOT_WINDOW, TOKEN_BLOCK), 0).astype(F32)
        for blk in range(n_blocks):
            toks = slice(blk * TOKEN_BLOCK, (blk + 1) * TOKEN_BLOCK)
            for e0 in range(0, N_EXPERTS, GATHER_GROUP):
                group = range(e0, e0 + GATHER_GROUP)
                firsts = [_slot_window(starts_ref, cap_lat, blk, e)[0] for e in group]
                hits = [pos_ref[e:e + 1, toks] - first.astype(F32) == slot for e, first in zip(group, firsts)]
                part = _dot(jnp.concatenate([_onehot(hit) for hit in hits], axis=0), hb_ref[toks, :])
                for i, (e, first, hit) in enumerate(zip(group, firsts, hits)):
                    rows = pl.ds(pl.multiple_of(first, 16), SLOT_WINDOW)
                    xe_ref[e, rows, :] += part[i * SLOT_WINDOW:(i + 1) * SLOT_WINDOW, :].astype(BF16)
                    gate = _row_sum(jnp.where(hit, aff_ref[e:e + 1, toks], 0.0))
                    gate_ref[e, rows, :] += jnp.broadcast_to(gate, (SLOT_WINDOW, LANES))

    @pl.when(jnp.logical_not(fits))
    def _():
        for e in range(N_EXPERTS):
            pick(e, 0, n_lat, 0, cap_lat)

    if cap_ctx > 0:
        for e in range(N_EXPERTS):
            pick(e, ctx_r0, n_ctx, cap_lat, cap_ctx)


def _gather(hb, aff_rows, pos_rows, starts, segments):
    bsz, t, d = hb.shape
    rows = sum(cap for _, _, cap in segments)
    assert segments[0][2] >= SLOT_WINDOW and starts.shape[1] == segments[0][1] // TOKEN_BLOCK + 1
    per_sample = lambda b: (b, 0, 0)
    slots = lambda b: (0, b, 0)
    return pl.pallas_call(
        functools.partial(_gather_kernel, segments=segments),
        grid=(bsz,),
        in_specs=[pl.BlockSpec((None,) + starts.shape[1:], per_sample, memory_space=pltpu.SMEM),
                  pl.BlockSpec((None, t, d), per_sample),
                  pl.BlockSpec((None, N_EXPERTS, t), per_sample),
                  pl.BlockSpec((None, N_EXPERTS, t), per_sample)],
        out_specs=[pl.BlockSpec((N_EXPERTS, rows, d), slots), pl.BlockSpec((N_EXPERTS, rows, LANES), slots)],
        out_shape=[jax.ShapeDtypeStruct((N_EXPERTS, bsz * rows, d), BF16),
                   jax.ShapeDtypeStruct((N_EXPERTS, bsz * rows, LANES), F32)],
        compiler_params=_cparams(("parallel",)),
    )(starts, hb, aff_rows, pos_rows)


def _ffn_kernel(xe_ref, gate_ref, wg_ref, wu_ref, wd_ref, y_ref, acc_ref, *, row_chunk):
    f = pl.program_id(1)
    last = pl.num_programs(1) - 1

    def step(mode):
        wg = wg_ref[...].astype(BF16)
        wu = wu_ref[...].astype(BF16)
        wd = wd_ref[...].astype(BF16)
        for r0 in range(0, xe_ref.shape[0], row_chunk):
            rows = slice(r0, r0 + row_chunk)
            xe = xe_ref[rows, :]
            hidden = (_silu(_dot(xe, wg)) * _dot(xe, wu)).astype(BF16)
            part = _dot(hidden, wd)
            if mode == "first":
                acc_ref[rows, :] = part
            elif mode == "middle":
                acc_ref[rows, :] += part
            else:
                y_ref[rows, :] = ((acc_ref[rows, :] + part) * gate_ref[rows, 0:1]).astype(y_ref.dtype)

    pl.when(f == 0)(lambda: step("first"))
    pl.when((f > 0) & (f < last))(lambda: step("middle"))
    pl.when(f == last)(lambda: step("last"))


def _ffn(xe, gate, w_gate, w_up, w_down, layer):
    n_rows, d = xe.shape[1], xe.shape[2]
    ff = w_gate.shape[-1]
    row_chunk = n_rows // 2 if n_rows % 2 == 0 and (n_rows // 2) % 16 == 0 else n_rows
    return pl.pallas_call(
        functools.partial(_ffn_kernel, row_chunk=row_chunk),
        grid=(N_EXPERTS, ff // FF_TILE),
        in_specs=[
            pl.BlockSpec((None, n_rows, d), lambda e, f: (e, 0, 0)),
            pl.BlockSpec((None, n_rows, LANES), lambda e, f: (e, 0, 0)),
            pl.BlockSpec((None, None, d, FF_TILE), lambda e, f: (layer, e, 0, f)),
            pl.BlockSpec((None, None, d, FF_TILE), lambda e, f: (layer, e, 0, f)),
            pl.BlockSpec((None, None, FF_TILE, d), lambda e, f: (layer, e, f, 0)),
        ],
        out_specs=pl.BlockSpec((None, n_rows, d), lambda e, f: (e, 0, 0)),
        out_shape=jax.ShapeDtypeStruct((N_EXPERTS, n_rows, d), BF16),
        scratch_shapes=[pltpu.VMEM((n_rows, d), F32)],
        compiler_params=_cparams(("parallel", "arbitrary")),
    )(xe, gate, w_gate, w_up, w_down)


def _scatter_kernel(starts_ref, x_ref, y_ref, pos_ref, mods_ref, nw_ref, *rest, segments, final, with_next):
    if with_next:
        next_mods_ref, next_nw_ref, out_ref, h_ref = rest
    else:
        out_ref, = rest
    d = y_ref.shape[-1]
    ctx_row = mods_ref.shape[0] - 1
    n_sub = x_ref.shape[0] // TOKEN_BLOCK
    on_last_block = pl.program_id(1) == pl.num_programs(1) - 1
    cap_lat = segments[0][2]

    def window(sub_block, e):
        return _slot_window(starts_ref, cap_lat, pl.program_id(1) * n_sub + sub_block, e)

    def block(seg_id, rows, windowed=None):
        cap = segments[seg_id][2]
        slot0 = sum(seg[2] for seg in segments[:seg_id])
        mod_row = ctx_row if seg_id == 1 else pl.program_id(0)
        gate = mods_ref[mod_row][5:6, :]
        pos = pos_ref[rows, :]
        if windowed is None:
            y_all = y_ref[:, slot0:slot0 + cap, :].reshape(N_EXPERTS * cap, d)
            slot = lax.broadcasted_iota(jnp.int32, (TOKEN_BLOCK, cap), 1).astype(F32)
            onehots = jnp.concatenate([_onehot(pos[:, e:e + 1] == slot) for e in range(N_EXPERTS)], axis=1)
        else:
            slot = lax.broadcasted_iota(jnp.int32, (TOKEN_BLOCK, SLOT_WINDOW), 1).astype(F32)
            firsts = [window(windowed, e)[0] for e in range(N_EXPERTS)]
            y_all = jnp.concatenate(
                [y_ref[e, pl.ds(pl.multiple_of(slot0 + firsts[e], 16), SLOT_WINDOW), :] for e in range(N_EXPERTS)], axis=0)
            onehots = jnp.concatenate(
                [_onehot(pos[:, e:e + 1] - firsts[e].astype(F32) == slot) for e in range(N_EXPERTS)], axis=1)
        x = x_ref[rows, :] + gate * _dot(onehots, y_all)
        if final:
            x = x * lax.rsqrt(jnp.mean(x * x, axis=-1, keepdims=True) + EPS) * nw_ref[...]
        out_ref[rows, :] = x
        if with_next:
            m = next_mods_ref[mod_row]
            h_ref[rows, :] = _rms_mod(x, next_nw_ref[...], m[0:1, :], m[1:2, :]).astype(BF16)

    has_ctx = segments[1][2] > 0
    n_lat_blocks = segments[0][1] // TOKEN_BLOCK

    def step(use_windows):
        for s in range(n_sub):
            rows = slice(s * TOKEN_BLOCK, (s + 1) * TOKEN_BLOCK)
            lat = functools.partial(block, 0, rows, s if use_windows else None)
            if has_ctx and s == n_sub - 1:
                pl.when(jnp.logical_not(on_last_block))(lat)
                pl.when(on_last_block)(functools.partial(block, 1, rows))
            else:
                lat()

    fits = jnp.bool_(True)
    for s in range(n_sub):
        is_lat = pl.program_id(1) * n_sub + s < n_lat_blocks
        for e in range(N_EXPERTS):
            safe = window(jnp.where(is_lat, s, 0) if has_ctx and s == n_sub - 1 else s, e)[1]
            fits = jnp.logical_and(fits, jnp.logical_or(safe, jnp.logical_not(is_lat)))
    pl.when(fits)(lambda: step(True))
    pl.when(jnp.logical_not(fits))(lambda: step(False))


def _scatter(x, y, pos, starts, mods, nw_final, segments, final, next_norm=None):
    bsz, t, d = x.shape
    assert segments[0][2] >= SLOT_WINDOW and starts.shape[1] == segments[0][1] // TOKEN_BLOCK + 1
    n_slots = y.shape[1] // bsz
    out_rows = sum(n for _, n, cap in segments if cap > 0)
    tm = (3 if segments[1][2] > 0 else 2) * TOKEN_BLOCK
    assert out_rows % tm == 0
    tok = lambda b, i: (b, i, 0)
    const2, const3 = (lambda b, i: (0, 0)), (lambda b, i: (0, 0, 0))
    in_specs = [
        pl.BlockSpec((None,) + starts.shape[1:], lambda b, i: (b, 0, 0), memory_space=pltpu.SMEM),
        pl.BlockSpec((None, tm, d), tok),
        pl.BlockSpec((N_EXPERTS, n_slots, d), lambda b, i: (0, b, 0)),
        pl.BlockSpec((None, tm, LANES), tok),
        pl.BlockSpec(mods.shape, const3),
        pl.BlockSpec((1, d), const2),
    ]
    args = [starts, x, y, pos, mods, nw_final]
    out_specs = [pl.BlockSpec((None, tm, d), tok)]
    out_shape = [jax.ShapeDtypeStruct((bsz, out_rows, d), F32)]
    if next_norm is not None:
        in_specs += [pl.BlockSpec(next_norm[0].shape, const3), pl.BlockSpec((1, d), const2)]
        args += list(next_norm)
        out_specs.append(pl.BlockSpec((None, tm, d), tok))
        out_shape.append(jax.ShapeDtypeStruct((bsz, out_rows, d), BF16))
    return pl.pallas_call(
        functools.partial(_scatter_kernel, segments=segments, final=final, with_next=next_norm is not None),
        grid=(bsz, out_rows // tm),
        in_specs=in_specs, out_specs=out_specs, out_shape=out_shape,
        compiler_params=_cparams(("parallel", "parallel")),
    )(*args)


def _moe(x, hb, aff, mods, w_gate, w_up, w_down, layer, nw_final, with_ctx, final, next_norm):
    bsz, t, d = x.shape
    seq = t - TOKEN_BLOCK
    cap_lat = EC_CAPACITY * seq // N_EXPERTS
    cap_ctx = EC_CAPACITY * TOKEN_BLOCK // N_EXPERTS if with_ctx else 0
    segments = ((0, seq, cap_lat), (seq, TOKEN_BLOCK, cap_ctx))
    pos, pos_rows, aff_rows, starts = _route(aff, segments)
    starts = starts[:, :seq // TOKEN_BLOCK + 1, :N_EXPERTS].astype(jnp.int32)
    xe, gate = _gather(hb, aff_rows, pos_rows, starts, segments)
    y = _ffn(xe, gate, w_gate, w_up, w_down, layer)
    return _scatter(x, y, pos, starts, mods, nw_final, segments, final, next_norm)


def _rope_tables(seq, n_ctx):
    half = RET_DK // 2
    t = jnp.arange(seq)
    inv = ROPE_THETA ** (-jnp.arange(0, half, 2, dtype=F32) / half)
    tabs = []
    for pos in ((t // GRID_W).astype(F32), (t % GRID_W).astype(F32)):
        ang = pos[:, None] * inv
        cos, sin = jnp.cos(ang), jnp.sin(ang)
        tabs.append(jnp.concatenate([cos, cos], axis=1))
        tabs.append(jnp.concatenate([-sin, sin], axis=1))
    ident = [jnp.ones, jnp.zeros, jnp.ones, jnp.zeros]
    return [jnp.concatenate([tab, fn((n_ctx, 2 * (half // 2)), F32)], axis=0) for tab, fn in zip(tabs, ident)]


def kernel(x, c, ctx, c_ctx, w_ada, b_ada, norm_mix, norm_ffn, ret_w_in, ret_w_out, hg_w_in, hg_g_norm,
           hg_lower_bounds, hg_w_out, moe_router, moe_w_gate, moe_w_up, moe_w_down, norm_final):
    bsz, seq, d = x.shape
    n_ctx = ctx.shape[1]
    depth = w_ada.shape[0]
    assert n_ctx == TOKEN_BLOCK and seq % TOKEN_BLOCK == 0 and d == RET_HEADS * RET_DK == HG_HEADS * HG_DK

    cc = jnp.concatenate([c, c_ctx[None, :], jnp.zeros((16 - bsz - 1, d), F32)], axis=0)
    mods_all = _ada(cc, w_ada, b_ada)[:, :bsz + 1].reshape(depth, bsz + 1, 6, d)
    t = seq + n_ctx
    rope = _rope_tables(seq, n_ctx)
    rope_specs = [pl.BlockSpec((t // 2, LANES), lambda j, b, i: (i, 0))] * 4
    w_router = jnp.pad(moe_router, ((0, 0), (0, 0), (0, LANES - N_EXPERTS)))

    xs, h = _prenorm(x, ctx, mods_all[0], norm_mix[0][None, :])
    for layer in range(depth):
        last = layer == depth - 1
        slot = layer // 2
        mods = mods_all[layer]
        if layer % 2 == 0:
            assert RET_HEADS * RET_DK == PROJ_COLS
            w = ret_w_in[slot]
            assert np.log2(RET_DK ** -0.5) % 1 == 0
            qk, = _proj(h, w, 0, 2, functools.partial(_ep_rope, key_scale=RET_DK ** -0.5), [BF16], rope, rope_specs)
            v_zg, = _proj(h, w, 2, 4, functools.partial(_ep_value_gate, gate_from=2), [BF16])
            o = _retention(qk, v_zg)
            w_out, g_norm, n_heads, dv = ret_w_out[slot].astype(BF16), None, RET_HEADS, RET_DV
        else:
            assert HG_HEADS * HG_DK == PROJ_COLS
            w = hg_w_in[slot]
            lb_specs = [pl.BlockSpec((depth, PROJ_COLS), lambda j, b, i: (0, j))]
            ep_gate = functools.partial(_ep_gate, layer=layer)
            q, = _proj(h, w, 0, 1, _ep_silu, [BF16])
            k_fb, g_fb = _proj(h, w, 1, 2, ep_gate, [F32, F32], (hg_lower_bounds.reshape(depth, 2 * PROJ_COLS),),
                               lb_specs)
            v_zg, = _proj(h, w, 3, 2, functools.partial(_ep_value_gate, gate_from=1), [BF16])
            o = _hgrn2(q, k_fb, g_fb, v_zg)
            w_out, g_norm, n_heads, dv = hg_w_out[slot].astype(BF16), hg_g_norm[slot][None, :], HG_HEADS, HG_DV
        xs, hb, aff = _readout(o, v_zg, xs, mods, w_out, g_norm, norm_ffn[layer][None, :], w_router[layer],
                                       n_heads, dv)
        next_norm = None if last else (mods_all[layer + 1], norm_mix[layer + 1][None, :])
        res = _moe(xs, hb, aff, mods, moe_w_gate, moe_w_up, moe_w_down, layer, norm_final[None, :],
                   with_ctx=not last, final=last, next_norm=next_norm)
        xs, h = res if next_norm is not None else (res[0], None)
    return xs
```
